```python
import jax, jax.numpy as jnp
from jax import lax
import numpy as np

D_MODEL = 1024
BATCH = 8
SEQ = 4096
DEPTH = 2

HEAD_DIM = 64
N_HEADS_FOX = 4
N_HEADS_NSA = 4
N_HEADS_SWA = 4
N_KV_SWA = 2
N_HEADS_DIL = 4
N_HEADS_MIX = N_HEADS_FOX + N_HEADS_NSA + N_HEADS_SWA + N_HEADS_DIL
D_MIX = N_HEADS_MIX * HEAD_DIM
D_FF = 2816
BLOCK = 128
NSA_CMP_LEN = 32
NSA_CMP_STRIDE = 16
NSA_CMP_HIDDEN = 128
NSA_SEL_LEN = 64
NSA_TOPN = 16
NSA_WINDOW = 512
SWA_WINDOW = 128
DIL_PAIRS = ((128, 1), (512, 4), (2048, 16))
RMS_EPS = 1e-6

IN_SPLITS = (
    ('fox_q', N_HEADS_FOX * HEAD_DIM), ('fox_k', N_HEADS_FOX * HEAD_DIM),
    ('fox_v', N_HEADS_FOX * HEAD_DIM), ('fox_f', N_HEADS_FOX),
    ('nsa_q', N_HEADS_NSA * HEAD_DIM),
    ('nsa_k_cmp', HEAD_DIM), ('nsa_v_cmp', HEAD_DIM),
    ('nsa_k_slc', HEAD_DIM), ('nsa_v_slc', HEAD_DIM),
    ('nsa_k_win', HEAD_DIM), ('nsa_v_win', HEAD_DIM),
    ('nsa_gate', 3 * N_HEADS_NSA),
    ('swa_q', N_HEADS_SWA * HEAD_DIM), ('swa_k', N_KV_SWA * HEAD_DIM), ('swa_v', N_KV_SWA * HEAD_DIM),
    ('dil_q', N_HEADS_DIL * HEAD_DIM), ('dil_k', N_HEADS_DIL * HEAD_DIM), ('dil_v', N_HEADS_DIL * HEAD_DIM),
)
D_IN = sum(w for _, w in IN_SPLITS)

kernel_name = 'hymba_style_fox_nsa_swa_dilated_macaron'


def rms_norm(x, g):
    xf = x.astype(jnp.float32)
    y = xf * lax.rsqrt(jnp.mean(xf * xf, axis=-1, keepdims=True) + RMS_EPS)
    return (y * g.astype(jnp.float32)).astype(x.dtype)


def swiglu(x, w_gate, w_up, w_down):
    return (jax.nn.silu(x @ w_gate) * (x @ w_up)) @ w_down


def split_columns(z):
    parts = {}
    off = 0
    for name, w in IN_SPLITS:
        parts[name] = z[..., off:off + w]
        off += w
    return parts


def to_heads(t, n):
    b, s, _ = t.shape
    return t.reshape(b, s, n, HEAD_DIM).transpose(0, 2, 1, 3)


def alibi_slopes():
    n = N_HEADS_SWA + N_HEADS_NSA + N_HEADS_DIL
    return jnp.asarray(2.0 ** (-8.0 * np.arange(1, n + 1) / n), jnp.float32)


def masked_softmax(s, mask):
    s = jnp.where(mask, s, -jnp.inf)
    m = jnp.max(s, axis=-1, keepdims=True)
    m = jnp.where(jnp.isfinite(m), m, 0.0)
    e = jnp.where(mask, jnp.exp(s - m), 0.0)
    return e / jnp.maximum(jnp.sum(e, axis=-1, keepdims=True), 1e-30)


def banded_attention(q, k, v, max_dist, slopes):
    b, h, l, hd = q.shape
    n_prev = -(-max_dist // BLOCK)
    nb = -(-l // BLOCK)
    pad = nb * BLOCK - l
    qb = jnp.pad(q, ((0, 0), (0, 0), (0, pad), (0, 0))).reshape(b, h, nb, BLOCK, hd)

    def windows(t):
        tp = jnp.pad(t, ((0, 0), (0, 0), (n_prev * BLOCK, pad), (0, 0))).reshape(b, h, nb + n_prev, BLOCK, hd)
        return jnp.concatenate([tp[:, :, j:j + nb] for j in range(n_prev + 1)], axis=3)

    kw, vw = windows(k), windows(v)
    width = (n_prev + 1) * BLOCK
    qpos = jnp.arange(nb)[:, None, None] * BLOCK + jnp.arange(BLOCK)[None, :, None]
    kpos = jnp.arange(nb)[:, None, None] * BLOCK + jnp.arange(width)[None, None, :] - n_prev * BLOCK
    dist = qpos - kpos
    mask = (dist >= 0) & (dist <= max_dist) & (kpos >= 0)
    s = jnp.einsum('bhnqd,bhnkd->bhnqk', qb, kw).astype(jnp.float32) * (hd ** -0.5)
    s = s - slopes.astype(jnp.float32)[:, None, None, None] * dist.astype(jnp.float32)
    s = jnp.where(mask, s, -jnp.inf)
    m = jnp.max(s, axis=-1, keepdims=True)
    e = jnp.exp(s - m)
    den = jnp.sum(e, axis=-1, keepdims=True)
    out = jnp.einsum('bhnqk,bhnkd->bhnqd', (e / den).astype(v.dtype), vw)
    lse = (m + jnp.log(den))[..., 0]
    out = out.reshape(b, h, nb * BLOCK, hd)[:, :, :l]
    lse = lse.reshape(b, h, nb * BLOCK)[:, :, :l]
    return out, lse


def forgetting_attention(q, k, v, log_f):
    b, h, s_len, hd = q.shape
    c = jnp.cumsum(log_f.astype(jnp.float32), axis=-1)
    nb = s_len // BLOCK
    qb = q.reshape(b, h, nb, BLOCK, hd).transpose(2, 0, 1, 3, 4)
    cb = c.reshape(b, h, nb, BLOCK).transpose(2, 0, 1, 3)
    kpos = jnp.arange(s_len)

    def one_block(args):
        i, q_i, c_i = args
        s = jnp.einsum('bhqd,bhkd->bhqk', q_i, k).astype(jnp.float32) * (hd ** -0.5)
        s = s + c_i[..., :, None] - c[..., None, :]
        qpos = i * BLOCK + jnp.arange(BLOCK)
        s = jnp.where(kpos[None, :] <= qpos[:, None], s, -jnp.inf)
        p = jax.nn.softmax(s, axis=-1)
        return jnp.einsum('bhqk,bhkd->bhqd', p.astype(v.dtype), v)

    out = lax.map(one_block, (jnp.arange(nb), qb, cb))
    return out.transpose(1, 2, 0, 3, 4).reshape(b, h, s_len, hd)


def nsa_attention(q, k_cmp, v_cmp, k_slc, v_slc, k_win, v_win, gate_logits, cmp_pe, cmp_w1, cmp_w2, slopes):
    b, h, s_len, hd = q.shape
    scale = hd ** -0.5
    t_pos = jnp.arange(s_len)
    sl = slopes.astype(jnp.float32)

    n_chunk = s_len // NSA_CMP_STRIDE
    n_cmp = n_chunk - 1

    def compress(t, idx):
        ch = t.reshape(b, n_chunk, NSA_CMP_STRIDE, hd)
        blocks = jnp.concatenate([ch[:, :-1], ch[:, 1:]], axis=2) + cmp_pe[idx]
        hid = jax.nn.silu(jnp.einsum('bnld,lde->bne', blocks, cmp_w1[idx]))
        return hid @ cmp_w2[idx]

    kc, vc = compress(k_cmp, 0), compress(v_cmp, 1)
    cmp_start = jnp.arange(n_cmp) * NSA_CMP_STRIDE
    cmp_dist = t_pos[:, None] - (cmp_start + NSA_CMP_LEN - 1)[None, :]
    s_cmp = jnp.einsum('bhtd,bnd->bhtn', q, kc).astype(jnp.float32) * scale
    s_cmp = s_cmp - sl[:, None, None] * cmp_dist.astype(jnp.float32)
    p_cmp = masked_softmax(s_cmp, cmp_dist >= 0)
    o_cmp = jnp.einsum('bhtn,bnd->bhtd', p_cmp.astype(vc.dtype), vc)

    n_sel = s_len // NSA_SEL_LEN
    sel_start = jnp.arange(n_sel) * NSA_SEL_LEN
    overlap = ((cmp_start[:, None] <= sel_start[None, :] + NSA_SEL_LEN - 1)
               & (cmp_start[:, None] + NSA_CMP_LEN - 1 >= sel_start[None, :])).astype(jnp.float32)
    imp = jnp.einsum('bhtn,nj->btj', p_cmp, overlap)
    cur = t_pos // NSA_SEL_LEN
    jj = jnp.arange(n_sel)
    causal_sel = sel_start[None, :] <= t_pos[:, None]
    forced = (jj[None, :] == 0) | (jj[None, :] == cur[:, None]) | (jj[None, :] == cur[:, None] - 1)
    score = jnp.where(causal_sel, jnp.where(forced, jnp.inf, imp), -jnp.inf)
    n_top = min(NSA_TOPN, n_sel)
    top_val, top_idx = lax.top_k(score, n_top)
    sel_ok = top_val > -jnp.inf

    ks_blocks = k_slc.reshape(b, n_sel, NSA_SEL_LEN, hd)
    vs_blocks = v_slc.reshape(b, n_sel, NSA_SEL_LEN, hd)
    nb = s_len // BLOCK
    qb = q.reshape(b, h, nb, BLOCK, hd).transpose(2, 0, 1, 3, 4)
    ib = top_idx.reshape(b, nb, BLOCK, n_top).transpose(1, 0, 2, 3)
    okb = sel_ok.reshape(b, nb, BLOCK, n_top).transpose(1, 0, 2, 3)
    gather = jax.vmap(lambda blocks, ix: blocks[ix])
    n_keys = n_top * NSA_SEL_LEN

    def one_block(args):
        i, q_i, idx_i, ok_i = args
        kg = gather(ks_blocks, idx_i).reshape(b, BLOCK, n_keys, hd)
        vg = gather(vs_blocks, idx_i).reshape(b, BLOCK, n_keys, hd)
        kpos = (idx_i[..., None] * NSA_SEL_LEN + jnp.arange(NSA_SEL_LEN)).reshape(b, BLOCK, n_keys)
        qpos = i * BLOCK + jnp.arange(BLOCK)
        dist = qpos[None, :, None] - kpos
        mask = (dist >= 0) & jnp.repeat(ok_i, NSA_SEL_LEN, axis=-1)
        s = jnp.einsum('bhqd,bqkd->bhqk', q_i, kg).astype(jnp.float32) * scale
        s = s - sl[None, :, None, None] * dist[:, None].astype(jnp.float32)
        p = masked_softmax(s, mask[:, None])
        return jnp.einsum('bhqk,bqkd->bhqd', p.astype(vg.dtype), vg)

    o_slc = lax.map(one_block, (jnp.arange(nb), qb, ib, okb))
    o_slc = o_slc.transpose(1, 2, 0, 3, 4).reshape(b, h, s_len, hd)

    kw = jnp.broadcast_to(k_win[:, None], (b, h, s_len, hd))
    vw = jnp.broadcast_to(v_win[:, None], (b, h, s_len, hd))
    o_win, _ = banded_attention(q, kw, vw, NSA_WINDOW - 1, slopes)

    g = jax.nn.sigmoid(gate_logits).reshape(b, s_len, h, 3).transpose(0, 2, 1, 3).astype(q.dtype)
    return g[..., 0:1] * o_cmp + g[..., 1:2] * o_slc + g[..., 2:3] * o_win


def sink_window_attention(q, k, v, sinks, slopes):
    rep = q.shape[1] // k.shape[1]
    k = jnp.repeat(k, rep, axis=1)
    v = jnp.repeat(v, rep, axis=1)
    out, lse = banded_attention(q, k, v, SWA_WINDOW - 1, slopes)
    keep = jax.nn.sigmoid(lse - sinks.astype(jnp.float32)[None, :, None])
    return out * keep[..., None].astype(out.dtype)


def dilated_attention(q, k, v, slopes):
    b, h, s_len, hd = q.shape
    outs, lses = [], []
    for window, d in DIL_PAIRS:
        def strided(t):
            return t.reshape(b, h, s_len // d, d, hd).transpose(0, 1, 3, 2, 4).reshape(b, h * d, s_len // d, hd)
        o, l = banded_attention(strided(q), strided(k), strided(v), window // d, jnp.repeat(slopes, d) * d)
        outs.append(o.reshape(b, h, d, s_len // d, hd).transpose(0, 1, 3, 2, 4).reshape(b, h, s_len, hd))
        lses.append(l.reshape(b, h, d, s_len // d).transpose(0, 1, 3, 2).reshape(b, h, s_len))
    w = jax.nn.softmax(jnp.stack(lses, axis=0), axis=0)
    return jnp.einsum('gbhs,gbhsd->bhsd', w.astype(q.dtype), jnp.stack(outs, axis=0))


def hybrid_mixer(hn, w_in, fox_b_f, nsa_cmp_pe, nsa_cmp_w1, nsa_cmp_w2, swa_sinks, w_out):
    b, s_len, _ = hn.shape
    z = split_columns(hn @ w_in)
    slopes = alibi_slopes()
    sl_swa = slopes[:N_HEADS_SWA]
    sl_nsa = slopes[N_HEADS_SWA:N_HEADS_SWA + N_HEADS_NSA]
    sl_dil = slopes[N_HEADS_SWA + N_HEADS_NSA:]

    log_f = jax.nn.log_sigmoid((z['fox_f'] + fox_b_f).astype(jnp.float32)).transpose(0, 2, 1)
    o_a = forgetting_attention(to_heads(z['fox_q'], N_HEADS_FOX), to_heads(z['fox_k'], N_HEADS_FOX),
                               to_heads(z['fox_v'], N_HEADS_FOX), log_f)
    o_b = nsa_attention(to_heads(z['nsa_q'], N_HEADS_NSA), z['nsa_k_cmp'], z['nsa_v_cmp'],
                        z['nsa_k_slc'], z['nsa_v_slc'], z['nsa_k_win'], z['nsa_v_win'], z['nsa_gate'],
                        nsa_cmp_pe, nsa_cmp_w1, nsa_cmp_w2, sl_nsa)
    o_c = sink_window_attention(to_heads(z['swa_q'], N_HEADS_SWA), to_heads(z['swa_k'], N_KV_SWA),
                                to_heads(z['swa_v'], N_KV_SWA), swa_sinks, sl_swa)
    o_d = dilated_attention(to_heads(z['dil_q'], N_HEADS_DIL), to_heads(z['dil_k'], N_HEADS_DIL),
                            to_heads(z['dil_v'], N_HEADS_DIL), sl_dil)
    o = jnp.concatenate([o_a, o_b.astype(o_a.dtype), o_c.astype(o_a.dtype), o_d.astype(o_a.dtype)], axis=1)
    o = o.transpose(0, 2, 1, 3).reshape(b, s_len, D_MIX).astype(hn.dtype)
    return o @ w_out


def setup_inputs(seed: int = 0) -> dict:
    key = jax.random.key(seed)
    ks = jax.random.split(key, 20)
    f32 = jnp.float32
    L, D, F = DEPTH, D_MODEL, D_FF

    def nrm(k, shape, scale):
        return jax.random.normal(k, shape, f32) * scale

    def gain(k, shape):
        return 1.0 + 0.02 * jax.random.normal(k, shape, f32)

    return {
        'x': nrm(ks[0], (BATCH, SEQ, D), 1.0),
        'norm_ffn1': gain(ks[1], (L, D)),
        'ffn1_w_gate': nrm(ks[2], (L, D, F), D ** -0.5),
        'ffn1_w_up': nrm(ks[3], (L, D, F), D ** -0.5),
        'ffn1_w_down': nrm(ks[4], (L, F, D), F ** -0.5),
        'norm_mix': gain(ks[5], (L, D)),
        'w_in': nrm(ks[6], (L, D, D_IN), D ** -0.5),
        'fox_b_f': 3.0 + 0.1 * jax.random.normal(ks[7], (L, N_HEADS_FOX), f32),
        'nsa_cmp_pe': nrm(ks[8], (L, 2, NSA_CMP_LEN, HEAD_DIM), 0.02),
        'nsa_cmp_w1': nrm(ks[9], (L, 2, NSA_CMP_LEN, HEAD_DIM, NSA_CMP_HIDDEN), (NSA_CMP_LEN * HEAD_DIM) ** -0.5),
        'nsa_cmp_w2': nrm(ks[10], (L, 2, NSA_CMP_HIDDEN, HEAD_DIM), NSA_CMP_HIDDEN ** -0.5),
        'swa_sinks': nrm(ks[11], (L, N_HEADS_SWA), 0.5),
        'w_out': nrm(ks[12], (L, D_MIX, D), D_MIX ** -0.5),
        'norm_ffn2': gain(ks[13], (L, D)),
        'ffn2_w_gate': nrm(ks[14], (L, D, F), D ** -0.5),
        'ffn2_w_up': nrm(ks[15], (L, D, F), D ** -0.5),
        'ffn2_w_down': nrm(ks[16], (L, F, D), F ** -0.5),
        'norm_final': gain(ks[17], (D,)),
    }


def reference(x, norm_ffn1, ffn1_w_gate, ffn1_w_up, ffn1_w_down, norm_mix, w_in, fox_b_f,
              nsa_cmp_pe, nsa_cmp_w1, nsa_cmp_w2, swa_sinks, w_out,
              norm_ffn2, ffn2_w_gate, ffn2_w_up, ffn2_w_down, norm_final):
    for l in range(DEPTH):
        x = x + 0.5 * swiglu(rms_norm(x, norm_ffn1[l]), ffn1_w_gate[l], ffn1_w_up[l], ffn1_w_down[l])
        x = x + hybrid_mixer(rms_norm(x, norm_mix[l]), w_in[l], fox_b_f[l], nsa_cmp_pe[l],
                             nsa_cmp_w1[l], nsa_cmp_w2[l], swa_sinks[l], w_out[l])
        x = x + 0.5 * swiglu(rms_norm(x, norm_ffn2[l]), ffn2_w_gate[l], ffn2_w_up[l], ffn2_w_down[l])
    return rms_norm(x, norm_final)
```

```python
import functools

import numpy as np
import jax
import jax.numpy as jnp
from jax import lax
from jax.experimental import pallas as pl
from jax.experimental.pallas import tpu as pltpu

F32 = jnp.float32
BF16 = jnp.bfloat16

HEAD_DIM = 64
LANES = 128
N_HEADS = 4
NSA_CMP_LEN = 32
NSA_CMP_STRIDE = 16
NSA_CMP_HIDDEN = 128
NSA_SEL_LEN = 64
NSA_TOPN = 16
NSA_WINDOW = 512
SWA_WINDOW = 128
DIL_PAIRS = ((128, 1), (512, 4), (2048, 16))
RMS_EPS = 1e-6
VMEM_LIMIT = 56 * 1024 * 1024

IN_SPLITS = (
    ('fox_q', 256), ('fox_k', 256), ('fox_v', 256), ('fox_f', 4),
    ('nsa_q', 256), ('nsa_k_cmp', 64), ('nsa_v_cmp', 64), ('nsa_k_slc', 64), ('nsa_v_slc', 64),
    ('nsa_k_win', 64), ('nsa_v_win', 64), ('nsa_gate', 12),
    ('swa_q', 256), ('swa_k', 128), ('swa_v', 128),
    ('dil_q', 256), ('dil_k', 256), ('dil_v', 256),
)

U_NSA_Q = 0
U_FOX_Q, U_FOX_K, U_FOX_V = 4, 6, 8
U_SWA_Q, U_SWA_K, U_SWA_V = 10, 12, 14
U_CMP = 16
U_SLC_KV, U_SLC_VK = 17, 18
U_WIN_KV, U_WIN_VK = 19, 20
N_MAIN = 21
N_DIL = 6
N_UNITS = N_MAIN + N_DIL + 1
GATE_LANE0 = 4


def _alibi_slopes():
    n = 3 * N_HEADS
    s = 2.0 ** (-8.0 * np.arange(1, n + 1) / n)
    return s[:4], s[4:8], s[8:]


def _w_in_layout():
    off, o = {}, 0
    for name, w in IN_SPLITS:
        off[name] = o
        o += w
    cols, scale = [], []

    def seg(name, start, width, sc=1.0):
        cols.extend(range(off[name] + start, off[name] + start + width))
        scale.extend([sc] * width)

    def zeros(n):
        cols.extend([0] * n)
        scale.extend([0.0] * n)

    qs = HEAD_DIM ** -0.5
    for h in range(4):
        seg('nsa_q', 64 * h, 64, qs)
        zeros(64)
    seg('fox_q', 0, 256, qs)
    seg('fox_k', 0, 256)
    seg('fox_v', 0, 256)
    seg('swa_q', 0, 256, qs)
    for name in ('swa_k', 'swa_v'):
        for kv in range(2):
            seg(name, 64 * kv, 64)
            seg(name, 64 * kv, 64)
    seg('nsa_k_cmp', 0, 64)
    seg('nsa_v_cmp', 0, 64)
    for a, b in (('nsa_k_slc', 'nsa_v_slc'), ('nsa_v_slc', 'nsa_k_slc'),
                 ('nsa_k_win', 'nsa_v_win'), ('nsa_v_win', 'nsa_k_win')):
        seg(a, 0, 64)
        seg(b, 0, 64)
    seg('dil_q', 0, 256, qs)
    seg('dil_k', 0, 256)
    seg('dil_v', 0, 256)
    seg('fox_f', 0, 4)
    seg('nsa_gate', 0, 12)
    zeros(LANES - 16)
    assert len(cols) == N_UNITS * LANES
    return np.asarray(cols, np.int32), np.asarray(scale, np.float32)


def _rms(x, g):
    ms = jnp.mean(x * x, axis=-1, keepdims=True)
    return x * lax.rsqrt(ms + RMS_EPS) * g


def _dot_nt(a, b):
    return lax.dot_general(a, b, (((1,), (1,)), ((), ())), preferred_element_type=F32)


def _params(sem, vmem=None):
    return pltpu.CompilerParams(dimension_semantics=sem, vmem_limit_bytes=vmem)


def _ffn_body(x_ref, g_ref, wg_ref, wu_ref, wd_ref, *rest, final):
    o_ref = rest[-1]
    x = x_ref[...]
    xn = _rms(x, g_ref[...]).astype(BF16)
    a = jnp.dot(xn, wg_ref[...], preferred_element_type=F32)
    b = jnp.dot(xn, wu_ref[...], preferred_element_type=F32)
    h = (a * jax.nn.sigmoid(a) * b).astype(BF16)
    y = x + 0.5 * jnp.dot(h, wd_ref[...], preferred_element_type=F32)
    if final:
        y = _rms(y, rest[0][...])
    o_ref[...] = y


def _ffn(x, g, wg, wu, wd, g_final=None):
    n, d = x.shape
    f = wg.shape[1]
    tm = 512 if n % 512 == 0 else n
    const = lambda i: (0, 0)
    once = pl.Buffered(1)
    in_specs = [
        pl.BlockSpec((tm, d), lambda i: (i, 0)),
        pl.BlockSpec((1, d), const),
        pl.BlockSpec((d, f), const, pipeline_mode=once),
        pl.BlockSpec((d, f), const, pipeline_mode=once),
        pl.BlockSpec((f, d), const, pipeline_mode=once),
    ]
    args = [x, g.reshape(1, d), wg, wu, wd]
    if g_final is not None:
        in_specs.append(pl.BlockSpec((1, d), const))
        args.append(g_final.reshape(1, d))
    return pl.pallas_call(
        functools.partial(_ffn_body, final=g_final is not None),
        out_shape=jax.ShapeDtypeStruct((n, d), F32),
        grid=(n // tm,),
        in_specs=in_specs,
        out_specs=pl.BlockSpec((tm, d), lambda i: (i, 0)),
        compiler_params=_params(("parallel",), VMEM_LIMIT),
        name="ffn",
    )(*args)


def _inproj_body(x_ref, g_ref, w_ref, zm_ref, zd_ref, zs_ref):
    xn = _rms(x_ref[...], g_ref[...]).astype(BF16)
    z = jnp.dot(xn, w_ref[...], preferred_element_type=F32)
    zm_ref[...] = z[:, :N_MAIN * LANES].astype(BF16)
    zd_ref[...] = z[:, N_MAIN * LANES:(N_MAIN + N_DIL) * LANES].astype(BF16)
    zs_ref[...] = z[:, (N_MAIN + N_DIL) * LANES:]


def _inproj(x, g, w):
    n, d = x.shape
    tm = 512 if n % 512 == 0 else n
    const = lambda i: (0, 0)
    row = lambda i: (i, 0)
    return pl.pallas_call(
        _inproj_body,
        out_shape=(jax.ShapeDtypeStruct((n, N_MAIN * LANES), BF16),
                   jax.ShapeDtypeStruct((n, N_DIL * LANES), BF16),
                   jax.ShapeDtypeStruct((n, LANES), F32)),
        grid=(n // tm,),
        in_specs=[pl.BlockSpec((tm, d), row), pl.BlockSpec((1, d), const),
                  pl.BlockSpec((d, N_UNITS * LANES), const, pipeline_mode=pl.Buffered(1))],
        out_specs=(pl.BlockSpec((tm, N_MAIN * LANES), row), pl.BlockSpec((tm, N_DIL * LANES), row),
                   pl.BlockSpec((tm, LANES), row)),
        compiler_params=_params(("parallel",), VMEM_LIMIT),
        name="inproj",
    )(x, g.reshape(1, d), w)


def _cumsum_body(x_ref, b_ref, u_ref, c_ref, *, tk):
    x = x_ref[...] + b_ref[...]
    lf = jnp.minimum(x, 0.0) - jnp.log(1.0 + jnp.exp(-jnp.abs(x)))
    u = u_ref[...]
    carry = jnp.zeros((8, 1), F32)
    per = tk // LANES
    for j in range(x.shape[1] // LANES):
        seg = lf[:, j * LANES:(j + 1) * LANES]
        cs = jnp.dot(seg, u, precision=lax.Precision.HIGHEST, preferred_element_type=F32) + carry
        c_ref[j // per, :, (j % per) * LANES:(j % per + 1) * LANES] = cs
        carry = cs[:, LANES - 1:LANES]


def _fox_cumsum(f_rows, b_rows, tk):
    bsz, _, _, s = f_rows.shape
    u = jnp.asarray(np.triu(np.ones((LANES, LANES), np.float32)))
    return pl.pallas_call(
        functools.partial(_cumsum_body, tk=tk),
        out_shape=jax.ShapeDtypeStruct((bsz, 2, s // tk, 8, tk), F32),
        grid=(bsz, 2),
        in_specs=[pl.BlockSpec((None, None, 8, s), lambda b, p: (b, p, 0, 0)),
                  pl.BlockSpec((None, 8, 1), lambda b, p: (p, 0, 0)),
                  pl.BlockSpec((LANES, LANES), lambda b, p: (0, 0))],
        out_specs=pl.BlockSpec((None, None, s // tk, 8, tk), lambda b, p: (b, p, 0, 0, 0)),
        compiler_params=_params(("parallel", "parallel")),
        name="fox_cumsum",
    )(f_rows, b_rows, u)


def _online_step(carry, s, v):
    m, l, acc = carry
    m_new = jnp.maximum(m, jnp.max(s, axis=-1, keepdims=True))
    alpha = jnp.exp(m - m_new)
    p = jnp.exp(s - m_new)
    l = alpha * l + jnp.sum(p, axis=-1, keepdims=True)
    acc = alpha * acc + jnp.dot(p.astype(BF16), v, preferred_element_type=F32)
    return m_new, l, acc


def _fox_body(q_ref, k_ref, v_ref, c_ref, o_ref, *, t):
    i = pl.program_id(2)
    q = q_ref[...]
    lane = lax.broadcasted_iota(jnp.int32, (t, LANES), 1)
    row = lax.broadcasted_iota(jnp.int32, (t, t), 0)
    col = lax.broadcasted_iota(jnp.int32, (t, t), 1)
    outs = []
    for hh in range(2):
        half = (lane < HEAD_DIM) if hh == 0 else (lane >= HEAD_DIM)
        qm = jnp.where(half, q, jnp.zeros_like(q))

        def scores(j, qm=qm, hh=hh):
            k0 = pl.multiple_of(j * t, t)
            s = _dot_nt(qm, k_ref[pl.ds(k0, t), :])
            return s - c_ref[j, hh:hh + 1, :], v_ref[pl.ds(k0, t), :]

        def body(j, carry, scores=scores):
            s, v = scores(j)
            return _online_step(carry, s, v)

        init = (jnp.full((t, 1), -jnp.inf, F32), jnp.zeros((t, 1), F32), jnp.zeros((t, LANES), F32))
        carry = lax.fori_loop(0, i, body, init)
        s, v = scores(i)
        _, l, acc = _online_step(carry, jnp.where(col <= row, s, -jnp.inf), v)
        outs.append(acc * (1.0 / l))
    o_ref[...] = jnp.where(lane < HEAD_DIM, outs[0], outs[1]).astype(o_ref.dtype)


def _fox(zm, c, t):
    bsz, s, _ = zm.shape
    return pl.pallas_call(
        functools.partial(_fox_body, t=t),
        out_shape=jax.ShapeDtypeStruct((bsz, s, 2 * LANES), BF16),
        grid=(bsz, 2, s // t),
        in_specs=[pl.BlockSpec((None, t, LANES), lambda b, p, i: (b, i, U_FOX_Q + p)),
                  pl.BlockSpec((None, s, LANES), lambda b, p, i: (b, 0, U_FOX_K + p)),
                  pl.BlockSpec((None, s, LANES), lambda b, p, i: (b, 0, U_FOX_V + p)),
                  pl.BlockSpec((None, None, s // t, 8, t), lambda b, p, i: (b, p, 0, 0, 0))],
        out_specs=pl.BlockSpec((None, t, LANES), lambda b, p, i: (b, i, p)),
        compiler_params=_params(("parallel", "parallel", "arbitrary")),
        name="fox_attn",
    )(zm, zm, zm, c)


def _banded_body(*refs, tq, span, s_loc, wpad, max_dist, slopes, wide, with_lse):
    if wide:
        q_ref, kkv_ref, vk_ref = refs[:3]
    else:
        q_ref, k_ref, v_ref = refs[:3]
    o_ref = refs[3]
    i = pl.program_id(1)
    t0 = i * tq
    start = pl.multiple_of(jnp.clip(t0 - wpad, 0, s_loc - span), LANES)
    kpos = start + lax.broadcasted_iota(jnp.int32, (1, span), 1)
    tpos = t0 + lax.broadcasted_iota(jnp.int32, (tq, 1), 0)
    dist = tpos - kpos
    mask = (dist >= 0) & (dist <= max_dist)
    krel = (kpos - t0).astype(F32)
    trel = (tpos - t0).astype(F32)
    lane = lax.broadcasted_iota(jnp.int32, (tq, LANES), 1)
    lses = []
    for p in range(2):
        if not wide:
            q = q_ref[:, p * LANES:(p + 1) * LANES]
            k = k_ref[pl.ds(start, span), p * LANES:(p + 1) * LANES]
            v = v_ref[pl.ds(start, span), p * LANES:(p + 1) * LANES]
        outs = []
        for hh in range(2):
            h = 2 * p + hh
            if wide:
                qm = q_ref[:, h * LANES:(h + 1) * LANES]
                k = kkv_ref[pl.ds(start, span), :]
                v = vk_ref[pl.ds(start, span), :] if hh == 0 else k
            else:
                half = (lane < HEAD_DIM) if hh == 0 else (lane >= HEAD_DIM)
                qm = jnp.where(half, q, jnp.zeros_like(q))
            s = _dot_nt(qm, k) + float(slopes[h]) * krel
            s = jnp.where(mask, s, -jnp.inf)
            m = jnp.max(s, axis=-1, keepdims=True)
            e = jnp.exp(s - m)
            l = jnp.sum(e, axis=-1, keepdims=True)
            outs.append(jnp.dot(e.astype(BF16), v, preferred_element_type=F32) * (1.0 / l))
            if with_lse:
                lses.append(m + jnp.log(l) - float(slopes[h]) * trel)
        o_ref[:, p * LANES:(p + 1) * LANES] = jnp.where(lane < HEAD_DIM, outs[0], outs[1]).astype(o_ref.dtype)
    if with_lse:
        acc = jnp.zeros((tq, LANES), F32)
        for h in range(4):
            acc = jnp.where(lane == h, lses[h], acc)
        refs[4][...] = acc


def _banded(arrs, units, slopes, max_dist, wpad, tq, wide, with_lse, name):
    g, s_loc, _ = arrs[0].shape
    tq = min(tq, s_loc)
    span = min(tq + wpad, s_loc)
    qw = 4 * LANES if wide else 2 * LANES
    kw = LANES if wide else 2 * LANES
    uq, uk, uv = units
    in_specs = [pl.BlockSpec((None, tq, qw), lambda b, i: (b, i, uq)),
                pl.BlockSpec((None, s_loc, kw), lambda b, i: (b, 0, uk)),
                pl.BlockSpec((None, s_loc, kw), lambda b, i: (b, 0, uv))]
    out_shape = [jax.ShapeDtypeStruct((g, s_loc, 2 * LANES), BF16)]
    out_specs = [pl.BlockSpec((None, tq, 2 * LANES), lambda b, i: (b, i, 0))]
    if with_lse:
        out_shape.append(jax.ShapeDtypeStruct((g, s_loc, LANES), F32))
        out_specs.append(pl.BlockSpec((None, tq, LANES), lambda b, i: (b, i, 0)))
    body = functools.partial(_banded_body, tq=tq, span=span, s_loc=s_loc, wpad=wpad, max_dist=max_dist,
                             slopes=tuple(float(x) for x in slopes), wide=wide, with_lse=with_lse)
    return pl.pallas_call(
        body, out_shape=tuple(out_shape), grid=(g, s_loc // tq), in_specs=in_specs, out_specs=tuple(out_specs),
        compiler_params=_params(("parallel", "arbitrary")), name=name,
    )(*arrs)


def _compress_body(k_ref, v_ref, pe_ref, w1_ref, w2_ref, kv_ref, vk_ref):
    n = k_ref.shape[0]
    hid = []
    for idx, ref in enumerate((k_ref, v_ref)):
        ch = ref[...].astype(F32)
        a = jnp.dot((ch + pe_ref[idx, 0]).astype(BF16), w1_ref[idx, 0], preferred_element_type=F32)
        b = jnp.dot((ch + pe_ref[idx, 1]).astype(BF16), w1_ref[idx, 1], preferred_element_type=F32)
        pre = a + pltpu.roll(b, n - 1, 0)
        hid.append((pre * jax.nn.sigmoid(pre)).astype(BF16))
    kv_ref[...] = (jnp.dot(hid[0], w2_ref[0, 0], preferred_element_type=F32)
                   + jnp.dot(hid[1], w2_ref[0, 1], preferred_element_type=F32)).astype(BF16)
    vk_ref[...] = (jnp.dot(hid[0], w2_ref[1, 0], preferred_element_type=F32)
                   + jnp.dot(hid[1], w2_ref[1, 1], preferred_element_type=F32)).astype(BF16)


def _compress(kch, vch, pe, w1, w2):
    bsz, n, w = kch.shape
    full = lambda *shape: pl.BlockSpec(shape, lambda b: (0,) * len(shape))
    blk = pl.BlockSpec((None, n, w), lambda b: (b, 0, 0))
    out = pl.BlockSpec((None, n, LANES), lambda b: (b, 0, 0))
    return pl.pallas_call(
        _compress_body,
        out_shape=(jax.ShapeDtypeStruct((bsz, n, LANES), BF16),) * 2,
        grid=(bsz,),
        in_specs=[blk, blk, full(2, 2, 1, w), full(2, 2, w, NSA_CMP_HIDDEN), full(2, 2, NSA_CMP_HIDDEN, LANES)],
        out_specs=(out, out),
        compiler_params=_params(("parallel",)),
        name="nsa_compress",
    )(kch, vch, pe, w1, w2)


def _cmp_select_body(q_ref, kv_ref, vk_ref, ov_ref, o_ref, sb_ref, *, tq, slopes, n_top):
    i = pl.program_id(1)
    t0 = i * tq
    n_cmp = kv_ref.shape[0]
    tpos = t0 + lax.broadcasted_iota(jnp.int32, (tq, 1), 0)
    cend = NSA_CMP_STRIDE * lax.broadcasted_iota(jnp.int32, (1, n_cmp), 1) + (NSA_CMP_LEN - 1)
    dist = tpos - cend
    mask = dist >= 0
    distf = dist.astype(F32)
    lane = lax.broadcasted_iota(jnp.int32, (tq, LANES), 1)
    kv = kv_ref[...]
    vk = vk_ref[...]
    psum = jnp.zeros((tq, n_cmp), F32)
    outs = []
    for h in range(4):
        s = _dot_nt(q_ref[:, h * LANES:(h + 1) * LANES], kv) - float(slopes[h]) * distf
        s = jnp.where(mask, s, -jnp.inf)
        m = jnp.max(s, axis=-1, keepdims=True)
        m = jnp.where(m == -jnp.inf, 0.0, m)
        e = jnp.where(mask, jnp.exp(s - m), 0.0)
        p = e / jnp.maximum(jnp.sum(e, axis=-1, keepdims=True), 1e-30)
        outs.append(jnp.dot(p.astype(BF16), vk if h % 2 == 0 else kv, preferred_element_type=F32))
        psum = psum + p
    for pr in range(2):
        o_ref[:, pr * LANES:(pr + 1) * LANES] = jnp.where(lane < HEAD_DIM, outs[2 * pr], outs[2 * pr + 1]).astype(o_ref.dtype)

    imp = lax.dot_general(ov_ref[...], psum, (((1,), (1,)), ((), ())), precision=lax.Precision.HIGHEST,
                          preferred_element_type=F32)
    jrow = lax.broadcasted_iota(jnp.int32, (LANES, tq), 0)
    j = jrow - HEAD_DIM
    tl = t0 + lax.broadcasted_iota(jnp.int32, (LANES, tq), 1)
    cur = tl // NSA_SEL_LEN
    causal = (j >= 0) & (j * NSA_SEL_LEN <= tl)
    forced = (j == 0) | (j == cur) | (j == cur - 1)
    score = jnp.where(causal, jnp.where(forced, jnp.inf, imp), -jnp.inf)
    rank = jnp.zeros((LANES, tq), F32)
    for jp in range(HEAD_DIM, LANES):
        other = score[jp:jp + 1, :]
        rank = rank + jnp.where(other > score, 1.0, jnp.where((other == score) & (jrow > jp), 1.0, 0.0))
    keep = causal & (rank < float(n_top))
    bias_t = jnp.where(keep, 0.0, jnp.where(j >= 0, -1e30, 0.0))
    sb_ref[...] = bias_t.T.astype(sb_ref.dtype)


def _cmp_select(zm, kv_c, vk_c, overlap_t, tq, slopes, n_top):
    bsz, s, _ = zm.shape
    n_cmp = kv_c.shape[1]
    return pl.pallas_call(
        functools.partial(_cmp_select_body, tq=tq, slopes=tuple(float(x) for x in slopes), n_top=n_top),
        out_shape=(jax.ShapeDtypeStruct((bsz, s, 2 * LANES), BF16), jax.ShapeDtypeStruct((bsz, s, LANES), BF16)),
        grid=(bsz, s // tq),
        in_specs=[pl.BlockSpec((None, tq, 4 * LANES), lambda b, i: (b, i, U_NSA_Q // 4)),
                  pl.BlockSpec((None, n_cmp, LANES), lambda b, i: (b, 0, 0)),
                  pl.BlockSpec((None, n_cmp, LANES), lambda b, i: (b, 0, 0)),
                  pl.BlockSpec((LANES, n_cmp), lambda b, i: (0, 0))],
        out_specs=(pl.BlockSpec((None, tq, 2 * LANES), lambda b, i: (b, i, 0)),
                   pl.BlockSpec((None, tq, LANES), lambda b, i: (b, i, 0))),
        compiler_params=_params(("parallel", "parallel")),
        name="nsa_cmp_select",
    )(zm, kv_c, vk_c, overlap_t)


def _sel_body(q_ref, sb_ref, kkv_ref, vk_ref, o_ref, kp_ref, *, t, slopes):
    i = pl.program_id(1)
    s_len = kkv_ref.shape[0]

    @pl.when(i == 0)
    def _():
        pos = lax.broadcasted_iota(jnp.int32, (s_len, LANES), 0)
        ln = lax.broadcasted_iota(jnp.int32, (s_len, LANES), 1)
        onehot = jnp.where(pos // NSA_SEL_LEN == ln - HEAD_DIM, 1.0, 0.0).astype(BF16)
        kp_ref[...] = jnp.where(ln < HEAD_DIM, kkv_ref[...], onehot)

    lane = lax.broadcasted_iota(jnp.int32, (t, LANES), 1)
    sb = sb_ref[...]
    order = (0, 2, 1, 3)
    qs = jnp.concatenate([jnp.where(lane < HEAD_DIM, q_ref[:, h * LANES:(h + 1) * LANES], sb) for h in order], axis=0)
    row = lax.broadcasted_iota(jnp.int32, (t, t), 0)
    col = lax.broadcasted_iota(jnp.int32, (t, t), 1)
    krel0 = lax.broadcasted_iota(jnp.int32, (1, t), 1)

    def scores(j, diag):
        k0 = pl.multiple_of(j * t, t)
        s = _dot_nt(qs, kp_ref[pl.ds(k0, t), :])
        krel = (krel0 + (j - i) * t).astype(F32)
        parts = []
        for n, h in enumerate(order):
            sh = s[n * t:(n + 1) * t] + float(slopes[h]) * krel
            if diag:
                sh = jnp.where(col <= row, sh, -1e30)
            parts.append(sh)
        return jnp.concatenate(parts, axis=0), vk_ref[pl.ds(k0, t), :], kkv_ref[pl.ds(k0, t), :]

    def step(carry, s, v_even, v_odd):
        m, l, acc = carry
        m_new = jnp.maximum(m, jnp.max(s, axis=-1, keepdims=True))
        alpha = jnp.exp(m - m_new)
        p = jnp.exp(s - m_new)
        l = alpha * l + jnp.sum(p, axis=-1, keepdims=True)
        pb = p.astype(BF16)
        pv = jnp.concatenate([jnp.dot(pb[:2 * t], v_even, preferred_element_type=F32),
                              jnp.dot(pb[2 * t:], v_odd, preferred_element_type=F32)], axis=0)
        return m_new, l, alpha * acc + pv

    def body(j, carry):
        return step(carry, *scores(j, False))

    init = (jnp.full((4 * t, 1), -1e30, F32), jnp.zeros((4 * t, 1), F32), jnp.zeros((4 * t, LANES), F32))
    carry = lax.fori_loop(0, i, body, init)
    _, l, acc = step(carry, *scores(i, True))
    o = acc * (1.0 / l)
    for pr in range(2):
        o_ref[:, pr * LANES:(pr + 1) * LANES] = jnp.where(
            lane < HEAD_DIM, o[pr * t:(pr + 1) * t], o[(2 + pr) * t:(3 + pr) * t]).astype(o_ref.dtype)


def _sel(zm, sbias, t, slopes):
    bsz, s, _ = zm.shape
    return pl.pallas_call(
        functools.partial(_sel_body, t=t, slopes=tuple(float(x) for x in slopes)),
        out_shape=jax.ShapeDtypeStruct((bsz, s, 2 * LANES), BF16),
        grid=(bsz, s // t),
        in_specs=[pl.BlockSpec((None, t, 4 * LANES), lambda b, i: (b, i, U_NSA_Q // 4)),
                  pl.BlockSpec((None, t, LANES), lambda b, i: (b, i, 0)),
                  pl.BlockSpec((None, s, LANES), lambda b, i: (b, 0, U_SLC_KV)),
                  pl.BlockSpec((None, s, LANES), lambda b, i: (b, 0, U_SLC_VK))],
        out_specs=pl.BlockSpec((None, t, 2 * LANES), lambda b, i: (b, i, 0)),
        scratch_shapes=[pltpu.VMEM((s, LANES), BF16)],
        compiler_params=_params(("parallel", "arbitrary")),
        name="nsa_sel_attn",
    )(zm, sbias, zm, zm)


def _outproj_body(x_ref, ofox_ref, ocmp_ref, oslc_ref, owin_ref, zs_ref, oswa_ref, lswa_ref, sink_ref,
                  od1_ref, od4_ref, od16_ref, l1_ref, l4_ref, l16_ref, eg_ref, eh_ref, w_ref, o_ref):
    hi = lax.Precision.HIGHEST

    def spread(vals, e):
        return jnp.dot(vals, e, precision=hi, preferred_element_type=F32)

    gate = jax.nn.sigmoid(zs_ref[...])
    o_nsa = (spread(gate, eg_ref[0]) * ocmp_ref[...].astype(F32)
             + spread(gate, eg_ref[1]) * oslc_ref[...].astype(F32)
             + spread(gate, eg_ref[2]) * owin_ref[...].astype(F32))
    eh = eh_ref[...]
    keep = jax.nn.sigmoid(lswa_ref[...] - sink_ref[...])
    o_swa = spread(keep, eh) * oswa_ref[...].astype(F32)
    l1, l4, l16 = l1_ref[...], l4_ref[...], l16_ref[...]
    m = jnp.maximum(jnp.maximum(l1, l4), l16)
    e1, e4, e16 = jnp.exp(l1 - m), jnp.exp(l4 - m), jnp.exp(l16 - m)
    inv = 1.0 / (e1 + e4 + e16)
    o_dil = (spread(e1 * inv, eh) * od1_ref[...].astype(F32) + spread(e4 * inv, eh) * od4_ref[...].astype(F32)
             + spread(e16 * inv, eh) * od16_ref[...].astype(F32))
    y = x_ref[...]
    for g, o in enumerate((ofox_ref[...], o_nsa.astype(BF16), o_swa.astype(BF16), o_dil.astype(BF16))):
        y = y + jnp.dot(o, w_ref[g], preferred_element_type=F32)
    o_ref[...] = y


def _outproj(x, heads, zs, lses, sinks, e_gate, e_head, w):
    n, d = x.shape
    tm = 512 if n % 512 == 0 else n
    row = lambda i: (i, 0)
    wide = pl.BlockSpec((tm, 2 * LANES), row)
    narrow = pl.BlockSpec((tm, LANES), row)
    ofox, ocmp, oslc, owin, oswa, od1, od4, od16 = heads
    lswa, l1, l4, l16 = lses
    return pl.pallas_call(
        _outproj_body,
        out_shape=jax.ShapeDtypeStruct((n, d), F32),
        grid=(n // tm,),
        in_specs=[pl.BlockSpec((tm, d), row), wide, wide, wide, wide, narrow, wide, narrow,
                  pl.BlockSpec((1, LANES), lambda i: (0, 0)), wide, wide, wide, narrow, narrow, narrow,
                  pl.BlockSpec((3, LANES, 2 * LANES), lambda i: (0, 0, 0)),
                  pl.BlockSpec((LANES, 2 * LANES), lambda i: (0, 0)),
                  pl.BlockSpec((4, 2 * LANES, d), lambda i: (0, 0, 0))],
        out_specs=pl.BlockSpec((tm, d), row),
        compiler_params=_params(("parallel",), VMEM_LIMIT),
        name="outproj",
    )(x, ofox, ocmp, oslc, owin, zs, oswa, lswa, sinks, od1, od4, od16, l1, l4, l16, e_gate, e_head, w)


def _spread_matrices():
    e_gate = np.zeros((3, LANES, 2 * LANES), np.float32)
    e_head = np.zeros((LANES, 2 * LANES), np.float32)
    for h in range(4):
        e_head[h, 64 * h:64 * (h + 1)] = 1.0
        for br in range(3):
            e_gate[br, GATE_LANE0 + 3 * h + br, 64 * h:64 * (h + 1)] = 1.0
    return jnp.asarray(e_gate), jnp.asarray(e_head)


def _overlap_t(n_cmp, n_sel):
    cs = np.arange(n_cmp) * NSA_CMP_STRIDE
    ss = np.arange(n_sel) * NSA_SEL_LEN
    ov = (cs[None, :] <= ss[:, None] + NSA_SEL_LEN - 1) & (cs[None, :] + NSA_CMP_LEN - 1 >= ss[:, None])
    out = np.zeros((LANES, n_cmp), np.float32)
    out[HEAD_DIM:HEAD_DIM + n_sel] = ov
    return jnp.asarray(out)


def _mixer(x, bsz, s, g_mix, w_in, fox_b_f, cmp_pe, cmp_w1, cmp_w2, swa_sinks, w_out):
    n, d = x.shape
    sl_swa, sl_nsa, sl_dil = _alibi_slopes()
    cols, scale = _w_in_layout()
    w_perm = (w_in[:, cols] * scale).astype(BF16)
    zm, zd, zs = _inproj(x, g_mix, w_perm)
    zm = zm.reshape(bsz, s, N_MAIN * LANES)
    zd = zd.reshape(bsz, s, N_DIL * LANES)

    t_fox = 256
    f_rows = jnp.transpose(zs.reshape(bsz, s, LANES)[:, :, :4], (0, 2, 1)).reshape(bsz, 2, 2, s)
    f_rows = jnp.pad(f_rows, ((0, 0), (0, 0), (0, 6), (0, 0)))
    b_rows = jnp.pad(fox_b_f.reshape(2, 2, 1), ((0, 0), (0, 6), (0, 0)))
    c = _fox_cumsum(f_rows, b_rows, t_fox)
    o_fox = _fox(zm, c, t_fox)

    n_chunk = s // NSA_CMP_STRIDE
    n_sel = s // NSA_SEL_LEN
    assert n_sel <= HEAD_DIM
    kvc = zm[:, :, U_CMP * LANES:(U_CMP + 1) * LANES]
    kch = kvc[:, :, :HEAD_DIM].reshape(bsz, n_chunk, NSA_CMP_STRIDE * HEAD_DIM)
    vch = kvc[:, :, HEAD_DIM:].reshape(bsz, n_chunk, NSA_CMP_STRIDE * HEAD_DIM)
    pe = cmp_pe.reshape(2, 2, 1, NSA_CMP_STRIDE * HEAD_DIM)
    w1 = cmp_w1.reshape(2, 2, NSA_CMP_STRIDE * HEAD_DIM, NSA_CMP_HIDDEN).astype(BF16)
    zpad = jnp.zeros((NSA_CMP_HIDDEN, HEAD_DIM), F32)
    lo = lambda w: jnp.concatenate([w, zpad], axis=1)
    hi = lambda w: jnp.concatenate([zpad, w], axis=1)
    w2 = jnp.stack([jnp.stack([lo(cmp_w2[0]), hi(cmp_w2[1])]),
                    jnp.stack([hi(cmp_w2[0]), lo(cmp_w2[1])])]).astype(BF16)
    kv_c, vk_c = _compress(kch, vch, pe, w1, w2)
    o_cmp, sbias = _cmp_select(zm, kv_c, vk_c, _overlap_t(n_chunk, n_sel), 256, sl_nsa, min(NSA_TOPN, n_sel))
    o_slc = _sel(zm, sbias, 256, sl_nsa)
    (o_win,) = _banded((zm, zm, zm), (U_NSA_Q // 4, U_WIN_KV, U_WIN_VK), sl_nsa, NSA_WINDOW - 1, NSA_WINDOW,
                       256, True, False, "nsa_win_attn")

    o_swa, l_swa = _banded((zm, zm, zm), (U_SWA_Q // 2, U_SWA_K // 2, U_SWA_V // 2), sl_swa, SWA_WINDOW - 1,
                           SWA_WINDOW, 256, False, True, "swa_attn")

    o_dil, l_dil = [], []
    for window, dd in DIL_PAIRS:
        zz = zd if dd == 1 else jnp.transpose(zd.reshape(bsz, s // dd, dd, N_DIL * LANES), (0, 2, 1, 3)).reshape(
            bsz * dd, s // dd, N_DIL * LANES)
        o, l = _banded((zz, zz, zz), (0, 1, 2), sl_dil * dd, window // dd, LANES, 256 if dd < 16 else 128,
                       False, True, "dil%d_attn" % dd)
        if dd > 1:
            o = jnp.transpose(o.reshape(bsz, dd, s // dd, 2 * LANES), (0, 2, 1, 3))
            l = jnp.transpose(l.reshape(bsz, dd, s // dd, LANES), (0, 2, 1, 3))
        o_dil.append(o.reshape(n, 2 * LANES))
        l_dil.append(l.reshape(n, LANES))

    e_gate, e_head = _spread_matrices()
    sinks = jnp.pad(swa_sinks.reshape(1, 4), ((0, 0), (0, LANES - 4)))
    flat = lambda a: a.reshape(n, a.shape[-1])
    heads = (flat(o_fox), flat(o_cmp), flat(o_slc), flat(o_win), flat(o_swa), *o_dil)
    return _outproj(x, heads, zs, (flat(l_swa), *l_dil), sinks, e_gate, e_head,
                    w_out.reshape(4, 2 * LANES, d).astype(BF16))


def kernel(x, norm_ffn1, ffn1_w_gate, ffn1_w_up, ffn1_w_down, norm_mix, w_in, fox_b_f, nsa_cmp_pe, nsa_cmp_w1,
           nsa_cmp_w2, swa_sinks, w_out, norm_ffn2, ffn2_w_gate, ffn2_w_up, ffn2_w_down, norm_final):
    bsz, s, d = x.shape
    depth = norm_ffn1.shape[0]
    h = x.reshape(bsz * s, d)
    for l in range(depth):
        h = _ffn(h, norm_ffn1[l], ffn1_w_gate[l].astype(BF16), ffn1_w_up[l].astype(BF16), ffn1_w_down[l].astype(BF16))
        h = _mixer(h, bsz, s, norm_mix[l], w_in[l], fox_b_f[l], nsa_cmp_pe[l], nsa_cmp_w1[l], nsa_cmp_w2[l],
                   swa_sinks[l], w_out[l])
        h = _ffn(h, norm_ffn2[l], ffn2_w_gate[l].astype(BF16), ffn2_w_up[l].astype(BF16), ffn2_w_down[l].astype(BF16),
                 norm_final if l == depth - 1 else None)
    return h.reshape(bsz, s, d)
```

```python
import functools

import numpy as np
import jax
import jax.numpy as jnp
from jax import lax
from jax.experimental import pallas as pl
from jax.experimental.pallas import tpu as pltpu

F32 = jnp.float32
BF16 = jnp.bfloat16

HEAD_DIM = 64
LANES = 128
NSA_CMP_LEN = 32
NSA_CMP_STRIDE = 16
NSA_CMP_HIDDEN = 128
NSA_SEL_LEN = 64
NSA_TOPN = 16
NSA_WINDOW = 512
SWA_WINDOW = 128
DIL_PAIRS = ((128, 1), (512, 4), (2048, 16))
RMS_EPS = 1e-6
LOG2E = 1.4426950408889634
LN2 = 0.6931471805599453
MASKED = -1e30
VMEM_LIMIT = 56 * 1024 * 1024

IN_SPLITS = (
    ('fox_q', 256), ('fox_k', 256), ('fox_v', 256), ('fox_f', 4),
    ('nsa_q', 256), ('nsa_k_cmp', 64), ('nsa_v_cmp', 64), ('nsa_k_slc', 64), ('nsa_v_slc', 64),
    ('nsa_k_win', 64), ('nsa_v_win', 64), ('nsa_gate', 12),
    ('swa_q', 256), ('swa_k', 128), ('swa_v', 128),
    ('dil_q', 256), ('dil_k', 256), ('dil_v', 256),
)

U_NSA_Q = 0
U_FOX_Q, U_FOX_K, U_FOX_V = 4, 6, 8
U_SWA_Q, U_SWA_K, U_SWA_V = 10, 12, 14
U_SLC = 16
U_WIN = 18
U_CMP = 20
N_MAIN = 21
N_DIL = 6
N_UNITS = N_MAIN + N_DIL + 1
GATE_LANE0 = 4


def _alibi_slopes():
    n = 12
    s = 2.0 ** (-8.0 * np.arange(1, n + 1) / n)
    return s[:4], s[4:8], s[8:]


def _w_in_layout():
    off, o = {}, 0
    for name, w in IN_SPLITS:
        off[name] = o
        o += w
    cols, scale = [], []

    def seg(name, start, width, sc=1.0):
        cols.extend(range(off[name] + start, off[name] + start + width))
        scale.extend([sc] * width)

    def zeros(n):
        cols.extend([0] * n)
        scale.extend([0.0] * n)

    qs = HEAD_DIM ** -0.5 * LOG2E
    for h in range(4):
        seg('nsa_q', 64 * h, 64, qs)
        zeros(64)
    seg('fox_q', 0, 256, qs)
    seg('fox_k', 0, 256)
    seg('fox_v', 0, 256)
    seg('swa_q', 0, 256, qs)
    for name in ('swa_k', 'swa_v'):
        for kv in range(2):
            seg(name, 64 * kv, 64)
            seg(name, 64 * kv, 64)
    for a, b in (('nsa_k_slc', 'nsa_v_slc'), ('nsa_v_slc', 'nsa_k_slc'),
                 ('nsa_k_win', 'nsa_v_win'), ('nsa_v_win', 'nsa_k_win')):
        seg(a, 0, 64)
        seg(b, 0, 64)
    seg('nsa_k_cmp', 0, 64)
    seg('nsa_v_cmp', 0, 64)
    seg('dil_q', 0, 256, qs)
    seg('dil_k', 0, 256)
    seg('dil_v', 0, 256)
    seg('fox_f', 0, 4)
    seg('nsa_gate', 0, 12)
    zeros(LANES - 16)
    assert len(cols) == N_UNITS * LANES
    return np.asarray(cols, np.int32), np.asarray(scale, np.float32)


def _rms(x, g):
    ms = jnp.mean(x * x, axis=-1, keepdims=True)
    return x * lax.rsqrt(ms + RMS_EPS) * g


def _dot_nt(a, b):
    return lax.dot_general(a, b, (((1,), (1,)), ((), ())), preferred_element_type=F32)


def _params(sem, vmem=None):
    return pltpu.CompilerParams(dimension_semantics=sem, vmem_limit_bytes=vmem)


def _stack_packed_q(q):
    slot = lax.shift_right_logical(lax.broadcasted_iota(jnp.int32, q.shape, 1), 6)
    return jnp.concatenate([jnp.where(slot == h, q, jnp.zeros_like(q)) for h in range(4)], axis=0)


def _unstack_packed_o(o, t):
    slot = lax.shift_right_logical(lax.broadcasted_iota(jnp.int32, (t, 2 * LANES), 1), 6)
    out = o[:t]
    for h in range(1, 4):
        out = jnp.where(slot == h, o[h * t:(h + 1) * t], out)
    return out


def _stack_wide_q(q_ref):
    return jnp.concatenate([q_ref[:, h * LANES:(h + 1) * LANES] for h in range(4)], axis=0)


def _unstack_wide_o(o, t):
    lane = lax.broadcasted_iota(jnp.int32, (t, LANES), 1)
    pairs = [jnp.where(lane < HEAD_DIM, o[2 * pr * t:(2 * pr + 1) * t, LANES:], o[(2 * pr + 1) * t:(2 * pr + 2) * t, :LANES])
             for pr in range(2)]
    return jnp.concatenate(pairs, axis=1)


def _flash_chunk(carry, qs, k, v, bias, t, diag):
    m, l, acc = carry
    n = k.shape[0]
    s_all = _dot_nt(qs, k)
    parts = []
    for h in range(4):
        s = bias(h, s_all[h * t:(h + 1) * t], n)
        if diag:
            row = lax.broadcasted_iota(jnp.int32, (t, n), 0)
            col = lax.broadcasted_iota(jnp.int32, (t, n), 1)
            s = jnp.where(col <= row, s, MASKED)
        parts.append(s)
    s = jnp.concatenate(parts, axis=0)
    m_new = jnp.maximum(m, jnp.max(s, axis=-1, keepdims=True))
    alpha = jnp.exp2(m - m_new)
    p = jnp.exp2(s - m_new)
    psum = p[:, :LANES]
    for c in range(1, n // LANES):
        psum = psum + p[:, c * LANES:(c + 1) * LANES]
    l = alpha * l + psum
    acc = alpha * acc + jnp.dot(p.astype(BF16), v, preferred_element_type=F32)
    return m_new, l, acc


def _flash_init(rows):
    return (jnp.full((rows, 1), MASKED, F32), jnp.zeros((rows, LANES), F32), jnp.zeros((rows, 2 * LANES), F32))


def _flash_finish(carry):
    _, l, acc = carry
    return acc * (1.0 / jnp.sum(l, axis=-1, keepdims=True))


def _ffn_body(x_ref, g_ref, wg_ref, wu_ref, wd_ref, *rest, final):
    o_ref = rest[-1]
    x = x_ref[...]
    xn = _rms(x, g_ref[...]).astype(BF16)
    a = jnp.dot(xn, wg_ref[...], preferred_element_type=F32)
    b = jnp.dot(xn, wu_ref[...], preferred_element_type=F32)
    h = (a * jax.nn.sigmoid(a) * b).astype(BF16)
    y = x + 0.5 * jnp.dot(h, wd_ref[...], preferred_element_type=F32)
    if final:
        y = _rms(y, rest[0][...])
    o_ref[...] = y


def _ffn(x, g, wg, wu, wd, g_final=None):
    n, d = x.shape
    f = wg.shape[1]
    tm = 512 if n % 512 == 0 else n
    const = lambda i: (0, 0)
    once = pl.Buffered(1)
    in_specs = [
        pl.BlockSpec((tm, d), lambda i: (i, 0)),
        pl.BlockSpec((1, d), const),
        pl.BlockSpec((d, f), const, pipeline_mode=once),
        pl.BlockSpec((d, f), const, pipeline_mode=once),
        pl.BlockSpec((f, d), const, pipeline_mode=once),
    ]
    args = [x, g.reshape(1, d), wg, wu, wd]
    if g_final is not None:
        in_specs.append(pl.BlockSpec((1, d), const))
        args.append(g_final.reshape(1, d))
    return pl.pallas_call(
        functools.partial(_ffn_body, final=g_final is not None),
        out_shape=jax.ShapeDtypeStruct((n, d), F32),
        grid=(n // tm,),
        in_specs=in_specs,
        out_specs=pl.BlockSpec((tm, d), lambda i: (i, 0)),
        compiler_params=_params(("parallel",), VMEM_LIMIT),
        name="ffn",
    )(*args)


def _inproj_body(x_ref, g_ref, w_ref, zm_ref, zd_ref, zs_ref):
    xn = _rms(x_ref[...], g_ref[...]).astype(BF16)
    z = jnp.dot(xn, w_ref[...], preferred_element_type=F32)
    zm_ref[...] = z[:, :N_MAIN * LANES].astype(BF16)
    zd_ref[...] = z[:, N_MAIN * LANES:(N_MAIN + N_DIL) * LANES].astype(BF16)
    zs_ref[...] = z[:, (N_MAIN + N_DIL) * LANES:]


def _inproj(x, g, w):
    n, d = x.shape
    tm = 512 if n % 512 == 0 else n
    const = lambda i: (0, 0)
    row = lambda i: (i, 0)
    return pl.pallas_call(
        _inproj_body,
        out_shape=(jax.ShapeDtypeStruct((n, N_MAIN * LANES), BF16),
                   jax.ShapeDtypeStruct((n, N_DIL * LANES), BF16),
                   jax.ShapeDtypeStruct((n, LANES), F32)),
        grid=(n // tm,),
        in_specs=[pl.BlockSpec((tm, d), row), pl.BlockSpec((1, d), const),
                  pl.BlockSpec((d, N_UNITS * LANES), const, pipeline_mode=pl.Buffered(1))],
        out_specs=(pl.BlockSpec((tm, N_MAIN * LANES), row), pl.BlockSpec((tm, N_DIL * LANES), row),
                   pl.BlockSpec((tm, LANES), row)),
        compiler_params=_params(("parallel",), VMEM_LIMIT),
        name="inproj",
    )(x, g.reshape(1, d), w)


def _cumsum_body(x_ref, b_ref, u_ref, c_ref, *, tk):
    x = x_ref[...] + b_ref[...]
    lf = (jnp.minimum(x, 0.0) - jnp.log(1.0 + jnp.exp(-jnp.abs(x)))) * LOG2E
    u = u_ref[...]
    carry = jnp.zeros((8, 1), F32)
    per = tk // LANES
    for j in range(x.shape[1] // LANES):
        seg = lf[:, j * LANES:(j + 1) * LANES]
        cs = jnp.dot(seg, u, precision=lax.Precision.HIGHEST, preferred_element_type=F32) + carry
        c_ref[j // per, :, (j % per) * LANES:(j % per + 1) * LANES] = cs
        carry = cs[:, LANES - 1:LANES]


def _fox_cumsum(f_rows, b_rows, tk):
    bsz, _, _, s = f_rows.shape
    u = jnp.asarray(np.triu(np.ones((LANES, LANES), np.float32)))
    return pl.pallas_call(
        functools.partial(_cumsum_body, tk=tk),
        out_shape=jax.ShapeDtypeStruct((bsz, 2, s // tk, 8, tk), F32),
        grid=(bsz, 2),
        in_specs=[pl.BlockSpec((None, None, 8, s), lambda b, p: (b, p, 0, 0)),
                  pl.BlockSpec((None, 8, 1), lambda b, p: (p, 0, 0)),
                  pl.BlockSpec((LANES, LANES), lambda b, p: (0, 0))],
        out_specs=pl.BlockSpec((None, None, s // tk, 8, tk), lambda b, p: (b, p, 0, 0, 0)),
        compiler_params=_params(("parallel", "parallel")),
        name="fox_cumsum",
    )(f_rows, b_rows, u)


def _fox_body(q_ref, k_ref, v_ref, c_ref, o_ref, *, t):
    i = pl.program_id(1)
    qs = _stack_packed_q(q_ref[...])

    def chunk(j, carry, diag):
        k0 = pl.multiple_of(j * t, t)
        bias = lambda h, s, n: s - c_ref[h // 2, j, (h % 2):(h % 2) + 1, :n]
        return _flash_chunk(carry, qs, k_ref[pl.ds(k0, t), :], v_ref[pl.ds(k0, t), :], bias, t, diag)

    carry = lax.fori_loop(0, i, lambda j, c: chunk(j, c, False), _flash_init(4 * t))
    o_ref[...] = _unstack_packed_o(_flash_finish(chunk(i, carry, True)), t).astype(o_ref.dtype)


def _fox(zm, c, t):
    bsz, s, _ = zm.shape
    return pl.pallas_call(
        functools.partial(_fox_body, t=t),
        out_shape=jax.ShapeDtypeStruct((bsz, s, 2 * LANES), BF16),
        grid=(bsz, s // t),
        in_specs=[pl.BlockSpec((None, t, 2 * LANES), lambda b, i: (b, i, U_FOX_Q // 2)),
                  pl.BlockSpec((None, s, 2 * LANES), lambda b, i: (b, 0, U_FOX_K // 2)),
                  pl.BlockSpec((None, s, 2 * LANES), lambda b, i: (b, 0, U_FOX_V // 2)),
                  pl.BlockSpec((None, 2, s // t, 8, t), lambda b, i: (b, 0, 0, 0, 0))],
        out_specs=pl.BlockSpec((None, t, 2 * LANES), lambda b, i: (b, i, 0)),
        compiler_params=_params(("parallel", "arbitrary"), VMEM_LIMIT),
        name="fox_attn",
    )(zm, zm, zm, c)


def _banded_body(*refs, tq, span, s_loc, wpad, max_dist, slopes, wide, with_lse):
    i = pl.program_id(1)
    t0 = i * tq
    start = pl.multiple_of(jnp.clip(t0 - wpad, 0, s_loc - span), LANES)
    kpos = start + lax.broadcasted_iota(jnp.int32, (1, span), 1)
    tpos = t0 + lax.broadcasted_iota(jnp.int32, (tq, 1), 0)
    dist = tpos - kpos
    mask = (dist >= 0) & (dist <= max_dist)
    krel = (kpos - t0).astype(F32)
    trel = (tpos - t0).astype(F32)
    if wide:
        q_ref, kv_ref, o_ref = refs[:3]
        qs = _stack_wide_q(q_ref)
        k = kv_ref[pl.ds(start, span), :LANES]
        v = kv_ref[pl.ds(start, span), :]
    else:
        q_ref, k_ref, v_ref, o_ref = refs[:4]
        qs = _stack_packed_q(q_ref[...])
        k = k_ref[pl.ds(start, span), :]
        v = v_ref[pl.ds(start, span), :]
    s = _dot_nt(qs, k)
    s = jnp.concatenate([jnp.where(mask, s[h * tq:(h + 1) * tq] + (float(slopes[h]) * LOG2E) * krel, MASKED)
                         for h in range(4)], axis=0)
    m = jnp.max(s, axis=-1, keepdims=True)
    e = jnp.exp2(s - m)
    l = jnp.sum(e, axis=-1, keepdims=True)
    o = jnp.dot(e.astype(BF16), v, preferred_element_type=F32) * (1.0 / l)
    o_ref[...] = (_unstack_wide_o(o, tq) if wide else _unstack_packed_o(o, tq)).astype(o_ref.dtype)
    if with_lse:
        lse = (m + jnp.log2(l)) * LN2
        lane = lax.broadcasted_iota(jnp.int32, (tq, LANES), 1)
        acc = jnp.zeros((tq, LANES), F32)
        for h in range(4):
            acc = jnp.where(lane == h, lse[h * tq:(h + 1) * tq] - float(slopes[h]) * trel, acc)
        refs[-1][...] = acc


def _banded(arrs, units, slopes, max_dist, wpad, tq, wide, with_lse, name):
    g, s_loc, _ = arrs[0].shape
    tq = min(tq, s_loc)
    span = min(tq + wpad, s_loc)
    if wide:
        in_specs = [pl.BlockSpec((None, tq, 4 * LANES), lambda b, i: (b, i, units[0])),
                    pl.BlockSpec((None, s_loc, 2 * LANES), lambda b, i: (b, 0, units[1]))]
    else:
        in_specs = [pl.BlockSpec((None, tq, 2 * LANES), lambda b, i: (b, i, units[0])),
                    pl.BlockSpec((None, s_loc, 2 * LANES), lambda b, i: (b, 0, units[1])),
                    pl.BlockSpec((None, s_loc, 2 * LANES), lambda b, i: (b, 0, units[2]))]
    out_shape = [jax.ShapeDtypeStruct((g, s_loc, 2 * LANES), BF16)]
    out_specs = [pl.BlockSpec((None, tq, 2 * LANES), lambda b, i: (b, i, 0))]
    if with_lse:
        out_shape.append(jax.ShapeDtypeStruct((g, s_loc, LANES), F32))
        out_specs.append(pl.BlockSpec((None, tq, LANES), lambda b, i: (b, i, 0)))
    body = functools.partial(_banded_body, tq=tq, span=span, s_loc=s_loc, wpad=wpad, max_dist=max_dist,
                             slopes=tuple(float(x) for x in slopes), wide=wide, with_lse=with_lse)
    return pl.pallas_call(
        body, out_shape=tuple(out_shape), grid=(g, s_loc // tq), in_specs=in_specs, out_specs=tuple(out_specs),
        compiler_params=_params(("parallel", "arbitrary")), name=name,
    )(*arrs)


def _compress_body(k_ref, v_ref, pe_ref, w1_ref, w2_ref, o_ref):
    n = k_ref.shape[0]
    hid = []
    for idx, ref in enumerate((k_ref, v_ref)):
        ch = ref[...].astype(F32)
        a = jnp.dot((ch + pe_ref[idx, 0]).astype(BF16), w1_ref[idx, 0], preferred_element_type=F32)
        b = jnp.dot((ch + pe_ref[idx, 1]).astype(BF16), w1_ref[idx, 1], preferred_element_type=F32)
        pre = a + pltpu.roll(b, n - 1, 0)
        hid.append((pre * jax.nn.sigmoid(pre)).astype(BF16))
    o_ref[...] = (jnp.dot(hid[0], w2_ref[0], preferred_element_type=F32)
                  + jnp.dot(hid[1], w2_ref[1], preferred_element_type=F32)).astype(o_ref.dtype)


def _compress(kch, vch, pe, w1, w2):
    bsz, n, w = kch.shape
    full = lambda *shape: pl.BlockSpec(shape, lambda b: (0,) * len(shape))
    blk = pl.BlockSpec((None, n, w), lambda b: (b, 0, 0))
    return pl.pallas_call(
        _compress_body,
        out_shape=jax.ShapeDtypeStruct((bsz, n, 2 * LANES), BF16),
        grid=(bsz,),
        in_specs=[blk, blk, full(2, 2, 1, w), full(2, 2, w, NSA_CMP_HIDDEN), full(2, NSA_CMP_HIDDEN, 2 * LANES)],
        out_specs=pl.BlockSpec((None, n, 2 * LANES), lambda b: (b, 0, 0)),
        compiler_params=_params(("parallel",)),
        name="nsa_compress",
    )(kch, vch, pe, w1, w2)


def _cmp_select_body(q_ref, kv_ref, ov_ref, o_ref, sb_ref, *, tq, slopes, n_top):
    i = pl.program_id(1)
    t0 = i * tq
    n_cmp = kv_ref.shape[0]
    tpos = t0 + lax.broadcasted_iota(jnp.int32, (tq, 1), 0)
    cend = NSA_CMP_STRIDE * lax.broadcasted_iota(jnp.int32, (1, n_cmp), 1) + (NSA_CMP_LEN - 1)
    dist = tpos - cend
    mask = dist >= 0
    distf = dist.astype(F32)
    kv = kv_ref[...]
    s_all = _dot_nt(_stack_wide_q(q_ref), kv[:, :LANES])
    psum = jnp.zeros((tq, n_cmp), F32)
    ps = []
    for h in range(4):
        s = jnp.where(mask, s_all[h * tq:(h + 1) * tq] - (float(slopes[h]) * LOG2E) * distf, -jnp.inf)
        m = jnp.max(s, axis=-1, keepdims=True)
        m = jnp.where(m == -jnp.inf, 0.0, m)
        e = jnp.where(mask, jnp.exp2(s - m), 0.0)
        p = e / jnp.maximum(jnp.sum(e, axis=-1, keepdims=True), 1e-30)
        ps.append(p.astype(BF16))
        psum = psum + p
    o = jnp.dot(jnp.concatenate(ps, axis=0), kv, preferred_element_type=F32)
    o_ref[...] = _unstack_wide_o(o, tq).astype(o_ref.dtype)

    imp = lax.dot_general(ov_ref[...], psum, (((1,), (1,)), ((), ())), precision=lax.Precision.HIGHEST,
                          preferred_element_type=F32)
    jrow = lax.broadcasted_iota(jnp.int32, (LANES, tq), 0)
    j = jrow - HEAD_DIM
    tl = t0 + lax.broadcasted_iota(jnp.int32, (LANES, tq), 1)
    cur = lax.shift_right_logical(tl, 6)
    causal = (j >= 0) & (j * NSA_SEL_LEN <= tl)
    forced = (j == 0) | (j == cur) | (j == cur - 1)
    score = jnp.where(causal, jnp.where(forced, jnp.inf, imp), -jnp.inf)
    rank = jnp.zeros((LANES, tq), F32)
    for jp in range(HEAD_DIM, LANES):
        other = score[jp:jp + 1, :]
        rank = rank + jnp.where(other > score, 1.0, jnp.where((other == score) & (jrow > jp), 1.0, 0.0))
    keep = causal & (rank < float(n_top))
    bias_t = jnp.where(keep, 0.0, jnp.where(j >= 0, MASKED, 0.0))
    sb_ref[...] = bias_t.T.astype(sb_ref.dtype)


def _cmp_select(zm, kv_c, overlap_t, tq, slopes, n_top):
    bsz, s, _ = zm.shape
    n_cmp = kv_c.shape[1]
    return pl.pallas_call(
        functools.partial(_cmp_select_body, tq=tq, slopes=tuple(float(x) for x in slopes), n_top=n_top),
        out_shape=(jax.ShapeDtypeStruct((bsz, s, 2 * LANES), BF16), jax.ShapeDtypeStruct((bsz, s, LANES), BF16)),
        grid=(bsz, s // tq),
        in_specs=[pl.BlockSpec((None, tq, 4 * LANES), lambda b, i: (b, i, U_NSA_Q // 4)),
                  pl.BlockSpec((None, n_cmp, 2 * LANES), lambda b, i: (b, 0, 0)),
                  pl.BlockSpec((LANES, n_cmp), lambda b, i: (0, 0))],
        out_specs=(pl.BlockSpec((None, tq, 2 * LANES), lambda b, i: (b, i, 0)),
                   pl.BlockSpec((None, tq, LANES), lambda b, i: (b, i, 0))),
        compiler_params=_params(("parallel", "parallel")),
        name="nsa_cmp_select",
    )(zm, kv_c, overlap_t)


def _sel_body(q_ref, sb_ref, kv_ref, o_ref, kp_ref, *, t, slopes):
    i = pl.program_id(1)
    s_len = kv_ref.shape[0]

    @pl.when(i == 0)
    def _():
        pos = lax.broadcasted_iota(jnp.int32, (s_len, LANES), 0)
        ln = lax.broadcasted_iota(jnp.int32, (s_len, LANES), 1)
        onehot = jnp.where(lax.shift_right_logical(pos, 6) == ln - HEAD_DIM, 1.0, 0.0).astype(BF16)
        kp_ref[...] = jnp.where(ln < HEAD_DIM, kv_ref[:, :LANES], onehot)

    lane = lax.broadcasted_iota(jnp.int32, (t, LANES), 1)
    sb = sb_ref[...]
    qs = jnp.concatenate([jnp.where(lane < HEAD_DIM, q_ref[:, h * LANES:(h + 1) * LANES], sb) for h in range(4)], axis=0)
    krel0 = lax.broadcasted_iota(jnp.int32, (1, t), 1)

    def chunk(j, carry, diag):
        k0 = pl.multiple_of(j * t, t)
        krel = (krel0 + (j - i) * t).astype(F32)
        bias = lambda h, s, n: s + (float(slopes[h]) * LOG2E) * krel[:, :n]
        return _flash_chunk(carry, qs, kp_ref[pl.ds(k0, t), :], kv_ref[pl.ds(k0, t), :], bias, t, diag)

    carry = lax.fori_loop(0, i, lambda j, c: chunk(j, c, False), _flash_init(4 * t))
    o_ref[...] = _unstack_wide_o(_flash_finish(chunk(i, carry, True)), t).astype(o_ref.dtype)


def _sel(zm, sbias, t, slopes):
    bsz, s, _ = zm.shape
    return pl.pallas_call(
        functools.partial(_sel_body, t=t, slopes=tuple(float(x) for x in slopes)),
        out_shape=jax.ShapeDtypeStruct((bsz, s, 2 * LANES), BF16),
        grid=(bsz, s // t),
        in_specs=[pl.BlockSpec((None, t, 4 * LANES), lambda b, i: (b, i, U_NSA_Q // 4)),
                  pl.BlockSpec((None, t, LANES), lambda b, i: (b, i, 0)),
                  pl.BlockSpec((None, s, 2 * LANES), lambda b, i: (b, 0, U_SLC // 2))],
        out_specs=pl.BlockSpec((None, t, 2 * LANES), lambda b, i: (b, i, 0)),
        scratch_shapes=[pltpu.VMEM((s, LANES), BF16)],
        compiler_params=_params(("parallel", "arbitrary"), VMEM_LIMIT),
        name="nsa_sel_attn",
    )(zm, sbias, zm)


def _outproj_body(x_ref, ofox_ref, ocmp_ref, oslc_ref, owin_ref, zs_ref, oswa_ref, lswa_ref, sink_ref,
                  od1_ref, od4_ref, od16_ref, l1_ref, l4_ref, l16_ref, eg_ref, eh_ref, w_ref, o_ref):
    hi = lax.Precision.HIGHEST

    def spread(vals, e):
        return jnp.dot(vals, e, precision=hi, preferred_element_type=F32)

    gate = jax.nn.sigmoid(zs_ref[...])
    o_nsa = (spread(gate, eg_ref[0]) * ocmp_ref[...].astype(F32)
             + spread(gate, eg_ref[1]) * oslc_ref[...].astype(F32)
             + spread(gate, eg_ref[2]) * owin_ref[...].astype(F32))
    eh = eh_ref[...]
    keep = jax.nn.sigmoid(lswa_ref[...] - sink_ref[...])
    o_swa = spread(keep, eh) * oswa_ref[...].astype(F32)
    l1, l4, l16 = l1_ref[...], l4_ref[...], l16_ref[...]
    m = jnp.maximum(jnp.maximum(l1, l4), l16)
    e1, e4, e16 = jnp.exp(l1 - m), jnp.exp(l4 - m), jnp.exp(l16 - m)
    inv = 1.0 / (e1 + e4 + e16)
    o_dil = (spread(e1 * inv, eh) * od1_ref[...].astype(F32) + spread(e4 * inv, eh) * od4_ref[...].astype(F32)
             + spread(e16 * inv, eh) * od16_ref[...].astype(F32))
    y = x_ref[...]
    for g, o in enumerate((ofox_ref[...], o_nsa.astype(BF16), o_swa.astype(BF16), o_dil.astype(BF16))):
        y = y + jnp.dot(o, w_ref[g], preferred_element_type=F32)
    o_ref[...] = y


def _outproj(x, heads, zs, lses, sinks, e_gate, e_head, w):
    n, d = x.shape
    tm = 512 if n % 512 == 0 else n
    row = lambda i: (i, 0)
    wide = pl.BlockSpec((tm, 2 * LANES), row)
    narrow = pl.BlockSpec((tm, LANES), row)
    ofox, ocmp, oslc, owin, oswa, od1, od4, od16 = heads
    lswa, l1, l4, l16 = lses
    return pl.pallas_call(
        _outproj_body,
        out_shape=jax.ShapeDtypeStruct((n, d), F32),
        grid=(n // tm,),
        in_specs=[pl.BlockSpec((tm, d), row), wide, wide, wide, wide, narrow, wide, narrow,
                  pl.BlockSpec((1, LANES), lambda i: (0, 0)), wide, wide, wide, narrow, narrow, narrow,
                  pl.BlockSpec((3, LANES, 2 * LANES), lambda i: (0, 0, 0)),
                  pl.BlockSpec((LANES, 2 * LANES), lambda i: (0, 0)),
                  pl.BlockSpec((4, 2 * LANES, d), lambda i: (0, 0, 0))],
        out_specs=pl.BlockSpec((tm, d), row),
        compiler_params=_params(("parallel",), VMEM_LIMIT),
        name="outproj",
    )(x, ofox, ocmp, oslc, owin, zs, oswa, lswa, sinks, od1, od4, od16, l1, l4, l16, e_gate, e_head, w)


def _spread_matrices():
    e_gate = np.zeros((3, LANES, 2 * LANES), np.float32)
    e_head = np.zeros((LANES, 2 * LANES), np.float32)
    for h in range(4):
        e_head[h, 64 * h:64 * (h + 1)] = 1.0
        for br in range(3):
            e_gate[br, GATE_LANE0 + 3 * h + br, 64 * h:64 * (h + 1)] = 1.0
    return jnp.asarray(e_gate), jnp.asarray(e_head)


def _overlap_t(n_cmp, n_sel):
    cs = np.arange(n_cmp) * NSA_CMP_STRIDE
    ss = np.arange(n_sel) * NSA_SEL_LEN
    ov = (cs[None, :] <= ss[:, None] + NSA_SEL_LEN - 1) & (cs[None, :] + NSA_CMP_LEN - 1 >= ss[:, None])
    out = np.zeros((LANES, n_cmp), np.float32)
    out[HEAD_DIM:HEAD_DIM + n_sel] = ov
    return jnp.asarray(out)


def _mixer(x, bsz, s, g_mix, w_in, fox_b_f, cmp_pe, cmp_w1, cmp_w2, swa_sinks, w_out):
    n, d = x.shape
    sl_swa, sl_nsa, sl_dil = _alibi_slopes()
    cols, scale = _w_in_layout()
    w_perm = (w_in[:, cols] * scale).astype(BF16)
    zm, zd, zs = _inproj(x, g_mix, w_perm)
    zm = zm.reshape(bsz, s, N_MAIN * LANES)
    zd = zd.reshape(bsz, s, N_DIL * LANES)

    t_fox = min(512, s)
    f_rows = jnp.transpose(zs.reshape(bsz, s, LANES)[:, :, :4], (0, 2, 1)).reshape(bsz, 2, 2, s)
    f_rows = jnp.pad(f_rows, ((0, 0), (0, 0), (0, 6), (0, 0)))
    b_rows = jnp.pad(fox_b_f.reshape(2, 2, 1), ((0, 0), (0, 6), (0, 0)))
    c = _fox_cumsum(f_rows, b_rows, t_fox)
    o_fox = _fox(zm, c, t_fox)

    n_chunk = s // NSA_CMP_STRIDE
    n_sel = s // NSA_SEL_LEN
    assert n_sel <= HEAD_DIM
    kvc = zm[:, :, U_CMP * LANES:(U_CMP + 1) * LANES]
    kch = kvc[:, :, :HEAD_DIM].reshape(bsz, n_chunk, NSA_CMP_STRIDE * HEAD_DIM)
    vch = kvc[:, :, HEAD_DIM:].reshape(bsz, n_chunk, NSA_CMP_STRIDE * HEAD_DIM)
    pe = cmp_pe.reshape(2, 2, 1, NSA_CMP_STRIDE * HEAD_DIM)
    w1 = cmp_w1.reshape(2, 2, NSA_CMP_STRIDE * HEAD_DIM, NSA_CMP_HIDDEN).astype(BF16)
    zpad = jnp.zeros((NSA_CMP_HIDDEN, HEAD_DIM), F32)
    w2 = jnp.stack([jnp.concatenate([cmp_w2[0], zpad, zpad, cmp_w2[0]], axis=1),
                    jnp.concatenate([zpad, cmp_w2[1], cmp_w2[1], zpad], axis=1)]).astype(BF16)
    kv_c = _compress(kch, vch, pe, w1, w2)
    o_cmp, sbias = _cmp_select(zm, kv_c, _overlap_t(n_chunk, n_sel), 256, sl_nsa, min(NSA_TOPN, n_sel))
    o_slc = _sel(zm, sbias, min(512, s), sl_nsa)
    (o_win,) = _banded((zm, zm), (U_NSA_Q // 4, U_WIN // 2), sl_nsa, NSA_WINDOW - 1, NSA_WINDOW,
                       256, True, False, "nsa_win_attn")

    o_swa, l_swa = _banded((zm, zm, zm), (U_SWA_Q // 2, U_SWA_K // 2, U_SWA_V // 2), sl_swa, SWA_WINDOW - 1,
                           SWA_WINDOW, 256, False, True, "swa_attn")

    o_dil, l_dil = [], []
    for window, dd in DIL_PAIRS:
        zz = zd if dd == 1 else jnp.transpose(zd.reshape(bsz, s // dd, dd, N_DIL * LANES), (0, 2, 1, 3)).reshape(
            bsz * dd, s // dd, N_DIL * LANES)
        o, l = _banded((zz, zz, zz), (0, 1, 2), sl_dil * dd, window // dd, LANES, 256 if dd < 16 else 128,
                       False, True, "dil%d_attn" % dd)
        if dd > 1:
            o = jnp.transpose(o.reshape(bsz, dd, s // dd, 2 * LANES), (0, 2, 1, 3))
            l = jnp.transpose(l.reshape(bsz, dd, s // dd, LANES), (0, 2, 1, 3))
        o_dil.append(o.reshape(n, 2 * LANES))
        l_dil.append(l.reshape(n, LANES))

    e_gate, e_head = _spread_matrices()
    sinks = jnp.pad(swa_sinks.reshape(1, 4), ((0, 0), (0, LANES - 4)))
    flat = lambda a: a.reshape(n, a.shape[-1])
    heads = (flat(o_fox), flat(o_cmp), flat(o_slc), flat(o_win), flat(o_swa), *o_dil)
    return _outproj(x, heads, zs, (flat(l_swa), *l_dil), sinks, e_gate, e_head,
                    w_out.reshape(4, 2 * LANES, d).astype(BF16))


def kernel(x, norm_ffn1, ffn1_w_gate, ffn1_w_up, ffn1_w_down, norm_mix, w_in, fox_b_f, nsa_cmp_pe, nsa_cmp_w1,
           nsa_cmp_w2, swa_sinks, w_out, norm_ffn2, ffn2_w_gate, ffn2_w_up, ffn2_w_down, norm_final):
    bsz, s, d = x.shape
    depth = norm_ffn1.shape[0]
    h = x.reshape(bsz * s, d)
    for l in range(depth):
        h = _ffn(h, norm_ffn1[l], ffn1_w_gate[l].astype(BF16), ffn1_w_up[l].astype(BF16), ffn1_w_down[l].astype(BF16))
        h = _mixer(h, bsz, s, norm_mix[l], w_in[l], fox_b_f[l], nsa_cmp_pe[l], nsa_cmp_w1[l], nsa_cmp_w2[l],
                   swa_sinks[l], w_out[l])
        h = _ffn(h, norm_ffn2[l], ffn2_w_gate[l].astype(BF16), ffn2_w_up[l].astype(BF16), ffn2_w_down[l].astype(BF16),
                 norm_final if l == depth - 1 else None)
    return h.reshape(bsz, s, d)
```

```python
import functools

import numpy as np
import jax
import jax.numpy as jnp
from jax import lax
from jax.experimental import pallas as pl
from jax.experimental.pallas import tpu as pltpu

F32 = jnp.float32
BF16 = jnp.bfloat16

HEAD_DIM = 64
LANES = 128
NSA_CMP_LEN = 32
NSA_CMP_STRIDE = 16
NSA_CMP_HIDDEN = 128
NSA_SEL_LEN = 64
NSA_TOPN = 16
NSA_WINDOW = 512
SWA_WINDOW = 128
DIL_PAIRS = ((128, 1), (512, 4), (2048, 16))
RMS_EPS = 1e-6
LOG2E = 1.4426950408889634
LN2 = 0.6931471805599453
MASKED = -1e30
VMEM_LIMIT = 56 * 1024 * 1024

IN_SPLITS = (
    ('fox_q', 256), ('fox_k', 256), ('fox_v', 256), ('fox_f', 4),
    ('nsa_q', 256), ('nsa_k_cmp', 64), ('nsa_v_cmp', 64), ('nsa_k_slc', 64), ('nsa_v_slc', 64),
    ('nsa_k_win', 64), ('nsa_v_win', 64), ('nsa_gate', 12),
    ('swa_q', 256), ('swa_k', 128), ('swa_v', 128),
    ('dil_q', 256), ('dil_k', 256), ('dil_v', 256),
)

U_NSA_Q = 0
U_FOX_Q, U_FOX_K, U_FOX_V = 4, 6, 8
U_SWA_Q, U_SWA_K, U_SWA_V = 10, 12, 14
U_SLC = 16
U_WIN = 18
U_CMP = 20
N_MAIN = 21
N_DIL = 6
N_UNITS = N_MAIN + N_DIL + 1
GATE_LANE0 = 4


def _alibi_slopes():
    n = 12
    s = 2.0 ** (-8.0 * np.arange(1, n + 1) / n)
    return s[:4], s[4:8], s[8:]


def _w_in_layout():
    off, o = {}, 0
    for name, w in IN_SPLITS:
        off[name] = o
        o += w
    cols, scale = [], []

    def seg(name, start, width, sc=1.0):
        cols.extend(range(off[name] + start, off[name] + start + width))
        scale.extend([sc] * width)

    def zeros(n):
        cols.extend([0] * n)
        scale.extend([0.0] * n)

    qs = HEAD_DIM ** -0.5 * LOG2E
    for h in range(4):
        seg('nsa_q', 64 * h, 64, qs)
        zeros(64)
    seg('fox_q', 0, 256, qs)
    seg('fox_k', 0, 256)
    seg('fox_v', 0, 256)
    seg('swa_q', 0, 256, qs)
    for name in ('swa_k', 'swa_v'):
        for kv in range(2):
            seg(name, 64 * kv, 64)
            seg(name, 64 * kv, 64)
    for a, b in (('nsa_k_slc', 'nsa_v_slc'), ('nsa_v_slc', 'nsa_k_slc'),
                 ('nsa_k_win', 'nsa_v_win'), ('nsa_v_win', 'nsa_k_win')):
        seg(a, 0, 64)
        seg(b, 0, 64)
    seg('nsa_k_cmp', 0, 64)
    seg('nsa_v_cmp', 0, 64)
    seg('dil_q', 0, 256, qs)
    seg('dil_k', 0, 256)
    seg('dil_v', 0, 256)
    seg('fox_f', 0, 4)
    seg('nsa_gate', 0, 12)
    zeros(LANES - 16)
    assert len(cols) == N_UNITS * LANES
    return np.asarray(cols, np.int32), np.asarray(scale, np.float32)


def _rms(x, g):
    ms = jnp.mean(x * x, axis=-1, keepdims=True)
    return x * lax.rsqrt(ms + RMS_EPS) * g


def _dot_nt(a, b):
    return lax.dot_general(a, b, (((1,), (1,)), ((), ())), preferred_element_type=F32)


def _params(sem, vmem=None):
    return pltpu.CompilerParams(dimension_semantics=sem, vmem_limit_bytes=vmem)


def _stack_packed_q(q):
    slot = lax.shift_right_logical(lax.broadcasted_iota(jnp.int32, q.shape, 1), 6)
    return jnp.concatenate([jnp.where(slot == h, q, jnp.zeros_like(q)) for h in range(4)], axis=0)


def _unstack_packed_o(o, t):
    slot = lax.shift_right_logical(lax.broadcasted_iota(jnp.int32, (t, 2 * LANES), 1), 6)
    out = o[:t]
    for h in range(1, 4):
        out = jnp.where(slot == h, o[h * t:(h + 1) * t], out)
    return out


def _stack_wide_q(q_ref):
    return jnp.concatenate([q_ref[:, h * LANES:(h + 1) * LANES] for h in range(4)], axis=0)


def _unstack_wide_o(o, t):
    lane = lax.broadcasted_iota(jnp.int32, (t, LANES), 1)
    pairs = [jnp.where(lane < HEAD_DIM, o[2 * pr * t:(2 * pr + 1) * t, LANES:], o[(2 * pr + 1) * t:(2 * pr + 2) * t, :LANES])
             for pr in range(2)]
    return jnp.concatenate(pairs, axis=1)


FLASH_ROW_BLOCK = 64


def _flash_consume(s_ref, p_ref, stats, v, t, diag):
    m_ref, alpha_ref, l_ref, acc_ref = stats
    n = s_ref.shape[1]
    rb = FLASH_ROW_BLOCK
    lanes = [slice(c * LANES, (c + 1) * LANES) for c in range(n // LANES)]

    def scores(r0):
        s = s_ref[r0:r0 + rb, :]
        if diag:
            row = lax.broadcasted_iota(jnp.int32, (rb, n), 0) + r0 % t
            col = lax.broadcasted_iota(jnp.int32, (rb, n), 1)
            s = jnp.where(col <= row, s, MASKED)
        return s

    for r0 in range(0, 4 * t, rb):
        s = scores(r0)
        mx = s[:, lanes[0]]
        for c in lanes[1:]:
            mx = jnp.maximum(mx, s[:, c])
        m_old = m_ref[r0:r0 + rb, :]
        m_new = jnp.maximum(m_old, jnp.max(mx, axis=-1, keepdims=True))
        alpha_ref[r0:r0 + rb, :] = jnp.exp2(m_old - m_new)
        m_ref[r0:r0 + rb, :] = m_new
    for r0 in range(0, 4 * t, rb):
        s = scores(r0)
        m_new = m_ref[r0:r0 + rb, :]
        psum = jnp.zeros((rb, LANES), F32)
        for c in lanes:
            p = jnp.exp2(s[:, c] - m_new)
            psum = psum + p
            p_ref[r0:r0 + rb, c] = p.astype(BF16)
        l_ref[r0:r0 + rb, :] = alpha_ref[r0:r0 + rb, :] * l_ref[r0:r0 + rb, :] + psum
    alpha = alpha_ref[...]
    acc_ref[...] = (jnp.concatenate([alpha, alpha], axis=1) * acc_ref[...]
                    + jnp.dot(p_ref[...], v, preferred_element_type=F32))


def _flash_causal(i, produce, consume, s_a, s_b, stats):
    m_ref, alpha_ref, l_ref, acc_ref = stats
    m_ref[...] = jnp.full(m_ref.shape, MASKED, F32)
    l_ref[...] = jnp.zeros(l_ref.shape, F32)
    acc_ref[...] = jnp.zeros(acc_ref.shape, F32)
    produce(s_a, 0)

    def pair(jj, carry):
        j = 2 * jj
        produce(s_b, j + 1)
        consume(s_a, j, False)
        produce(s_a, j + 2)
        consume(s_b, j + 1, False)
        return carry

    lax.fori_loop(0, i // 2, pair, 0)

    @pl.when(i % 2 == 1)
    def _():
        consume(s_a, i - 1, False)
        produce(s_b, i)
        consume(s_b, i, True)

    @pl.when(i % 2 == 0)
    def _():
        consume(s_a, i, True)

    return acc_ref[...] * (1.0 / jnp.sum(l_ref[...], axis=-1, keepdims=True))


def _flash_scratch(t):
    return [pltpu.VMEM((4 * t, t), F32), pltpu.VMEM((4 * t, t), F32), pltpu.VMEM((4 * t, t), BF16),
            pltpu.VMEM((4 * t, LANES), F32), pltpu.VMEM((4 * t, LANES), F32), pltpu.VMEM((4 * t, LANES), F32),
            pltpu.VMEM((4 * t, 2 * LANES), F32)]


def _ffn_body(x_ref, g_ref, wg_ref, wu_ref, wd_ref, *rest, final):
    o_ref = rest[-1]
    x = x_ref[...]
    xn = _rms(x, g_ref[...]).astype(BF16)
    a = jnp.dot(xn, wg_ref[...], preferred_element_type=F32)
    b = jnp.dot(xn, wu_ref[...], preferred_element_type=F32)
    h = (a * jax.nn.sigmoid(a) * b).astype(BF16)
    y = x + 0.5 * jnp.dot(h, wd_ref[...], preferred_element_type=F32)
    if final:
        y = _rms(y, rest[0][...])
    o_ref[...] = y


def _ffn(x, g, wg, wu, wd, g_final=None):
    n, d = x.shape
    f = wg.shape[1]
    tm = 512 if n % 512 == 0 else n
    const = lambda i: (0, 0)
    once = pl.Buffered(1)
    in_specs = [
        pl.BlockSpec((tm, d), lambda i: (i, 0)),
        pl.BlockSpec((1, d), const),
        pl.BlockSpec((d, f), const, pipeline_mode=once),
        pl.BlockSpec((d, f), const, pipeline_mode=once),
        pl.BlockSpec((f, d), const, pipeline_mode=once),
    ]
    args = [x, g.reshape(1, d), wg, wu, wd]
    if g_final is not None:
        in_specs.append(pl.BlockSpec((1, d), const))
        args.append(g_final.reshape(1, d))
    return pl.pallas_call(
        functools.partial(_ffn_body, final=g_final is not None),
        out_shape=jax.ShapeDtypeStruct((n, d), F32),
        grid=(n // tm,),
        in_specs=in_specs,
        out_specs=pl.BlockSpec((tm, d), lambda i: (i, 0)),
        compiler_params=_params(("parallel",), VMEM_LIMIT),
        name="ffn",
    )(*args)


def _inproj_body(x_ref, g_ref, w_ref, zm_ref, zd_ref, zs_ref):
    xn = _rms(x_ref[...], g_ref[...]).astype(BF16)
    z = jnp.dot(xn, w_ref[...], preferred_element_type=F32)
    zm_ref[...] = z[:, :N_MAIN * LANES].astype(BF16)
    zd_ref[...] = z[:, N_MAIN * LANES:(N_MAIN + N_DIL) * LANES].astype(BF16)
    zs_ref[...] = z[:, (N_MAIN + N_DIL) * LANES:]


def _inproj(x, g, w):
    n, d = x.shape
    tm = 512 if n % 512 == 0 else n
    const = lambda i: (0, 0)
    row = lambda i: (i, 0)
    return pl.pallas_call(
        _inproj_body,
        out_shape=(jax.ShapeDtypeStruct((n, N_MAIN * LANES), BF16),
                   jax.ShapeDtypeStruct((n, N_DIL * LANES), BF16),
                   jax.ShapeDtypeStruct((n, LANES), F32)),
        grid=(n // tm,),
        in_specs=[pl.BlockSpec((tm, d), row), pl.BlockSpec((1, d), const),
                  pl.BlockSpec((d, N_UNITS * LANES), const, pipeline_mode=pl.Buffered(1))],
        out_specs=(pl.BlockSpec((tm, N_MAIN * LANES), row), pl.BlockSpec((tm, N_DIL * LANES), row),
                   pl.BlockSpec((tm, LANES), row)),
        compiler_params=_params(("parallel",), VMEM_LIMIT),
        name="inproj",
    )(x, g.reshape(1, d), w)


def _cumsum_body(x_ref, b_ref, u_ref, c_ref, *, tk):
    x = x_ref[...] + b_ref[...]
    lf = (jnp.minimum(x, 0.0) - jnp.log(1.0 + jnp.exp(-jnp.abs(x)))) * LOG2E
    u = u_ref[...]
    carry = jnp.zeros((8, 1), F32)
    per = tk // LANES
    for j in range(x.shape[1] // LANES):
        seg = lf[:, j * LANES:(j + 1) * LANES]
        cs = jnp.dot(seg, u, precision=lax.Precision.HIGHEST, preferred_element_type=F32) + carry
        c_ref[j // per, :, (j % per) * LANES:(j % per + 1) * LANES] = cs
        carry = cs[:, LANES - 1:LANES]


def _fox_cumsum(f_rows, b_rows, tk):
    bsz, _, _, s = f_rows.shape
    u = jnp.asarray(np.triu(np.ones((LANES, LANES), np.float32)))
    return pl.pallas_call(
        functools.partial(_cumsum_body, tk=tk),
        out_shape=jax.ShapeDtypeStruct((bsz, 2, s // tk, 8, tk), F32),
        grid=(bsz, 2),
        in_specs=[pl.BlockSpec((None, None, 8, s), lambda b, p: (b, p, 0, 0)),
                  pl.BlockSpec((None, 8, 1), lambda b, p: (p, 0, 0)),
                  pl.BlockSpec((LANES, LANES), lambda b, p: (0, 0))],
        out_specs=pl.BlockSpec((None, None, s // tk, 8, tk), lambda b, p: (b, p, 0, 0, 0)),
        compiler_params=_params(("parallel", "parallel")),
        name="fox_cumsum",
    )(f_rows, b_rows, u)


def _fox_body(q_ref, k_ref, v_ref, c_ref, o_ref, qs_ref, s_a, s_b, p_ref, *stats, t):
    i = pl.program_id(1)
    qs_ref[...] = _stack_packed_q(q_ref[...])

    def rows(j):
        return pl.ds(pl.multiple_of(j * t, t), t)

    def produce(s_ref, j):
        s = _dot_nt(qs_ref[...], k_ref[rows(j), :])
        for h in range(4):
            s_ref[h * t:(h + 1) * t, :] = s[h * t:(h + 1) * t] - c_ref[h // 2, j, (h % 2):(h % 2) + 1, :]

    def consume(s_ref, j, diag):
        _flash_consume(s_ref, p_ref, stats, v_ref[rows(j), :], t, diag)

    o = _flash_causal(i, produce, consume, s_a, s_b, stats)
    o_ref[...] = _unstack_packed_o(o, t).astype(o_ref.dtype)


def _fox(zm, c, t):
    bsz, s, _ = zm.shape
    return pl.pallas_call(
        functools.partial(_fox_body, t=t),
        out_shape=jax.ShapeDtypeStruct((bsz, s, 2 * LANES), BF16),
        grid=(bsz, s // t),
        in_specs=[pl.BlockSpec((None, t, 2 * LANES), lambda b, i: (b, i, U_FOX_Q // 2)),
                  pl.BlockSpec((None, s, 2 * LANES), lambda b, i: (b, 0, U_FOX_K // 2)),
                  pl.BlockSpec((None, s, 2 * LANES), lambda b, i: (b, 0, U_FOX_V // 2)),
                  pl.BlockSpec((None, 2, s // t, 8, t), lambda b, i: (b, 0, 0, 0, 0))],
        out_specs=pl.BlockSpec((None, t, 2 * LANES), lambda b, i: (b, i, 0)),
        scratch_shapes=[pltpu.VMEM((4 * t, 2 * LANES), BF16)] + _flash_scratch(t),
        compiler_params=_params(("parallel", "arbitrary"), VMEM_LIMIT),
        name="fox_attn",
    )(zm, zm, zm, c)


def _banded_body(*refs, tq, sq, span, s_loc, wpad, max_dist, slopes, wide, with_lse):
    i = pl.program_id(1)
    o_ref = refs[2] if wide else refs[3]
    for r in range(tq // sq):
        t0 = i * tq + r * sq
        rows = slice(r * sq, (r + 1) * sq)
        start = pl.multiple_of(jnp.clip(t0 - wpad, 0, s_loc - span), LANES)
        kpos = start + lax.broadcasted_iota(jnp.int32, (1, span), 1)
        tpos = t0 + lax.broadcasted_iota(jnp.int32, (sq, 1), 0)
        dist = tpos - kpos
        mask = (dist >= 0) & (dist <= max_dist)
        krel = (kpos - t0).astype(F32)
        trel = (tpos - t0).astype(F32)
        if wide:
            q_ref, kv_ref = refs[:2]
            qs = jnp.concatenate([q_ref[rows, h * LANES:(h + 1) * LANES] for h in range(4)], axis=0)
            k = kv_ref[pl.ds(start, span), :LANES]
            v = kv_ref[pl.ds(start, span), :]
        else:
            q_ref, k_ref, v_ref = refs[:3]
            qs = _stack_packed_q(q_ref[rows, :])
            k = k_ref[pl.ds(start, span), :]
            v = v_ref[pl.ds(start, span), :]
        s = _dot_nt(qs, k)
        s = jnp.concatenate([jnp.where(mask, s[h * sq:(h + 1) * sq] + (float(slopes[h]) * LOG2E) * krel, MASKED)
                             for h in range(4)], axis=0)
        m = jnp.max(s, axis=-1, keepdims=True)
        e = jnp.exp2(s - m)
        l = jnp.sum(e, axis=-1, keepdims=True)
        o = jnp.dot(e.astype(BF16), v, preferred_element_type=F32) * (1.0 / l)
        o_ref[rows, :] = (_unstack_wide_o(o, sq) if wide else _unstack_packed_o(o, sq)).astype(o_ref.dtype)
        if with_lse:
            lse = (m + jnp.log2(l)) * LN2
            lane = lax.broadcasted_iota(jnp.int32, (sq, LANES), 1)
            acc = jnp.zeros((sq, LANES), F32)
            for h in range(4):
                acc = jnp.where(lane == h, lse[h * sq:(h + 1) * sq] - float(slopes[h]) * trel, acc)
            refs[-1][rows, :] = acc


def _banded(arrs, units, slopes, max_dist, wpad, wide, with_lse, name):
    g, s_loc, _ = arrs[0].shape
    tq = min(512, s_loc)
    sq = min(tq, wpad, 256)
    span = min(sq + wpad, s_loc)
    if wide:
        in_specs = [pl.BlockSpec((None, tq, 4 * LANES), lambda b, i: (b, i, units[0])),
                    pl.BlockSpec((None, s_loc, 2 * LANES), lambda b, i: (b, 0, units[1]))]
    else:
        in_specs = [pl.BlockSpec((None, tq, 2 * LANES), lambda b, i: (b, i, units[0])),
                    pl.BlockSpec((None, s_loc, 2 * LANES), lambda b, i: (b, 0, units[1])),
                    pl.BlockSpec((None, s_loc, 2 * LANES), lambda b, i: (b, 0, units[2]))]
    out_shape = [jax.ShapeDtypeStruct((g, s_loc, 2 * LANES), BF16)]
    out_specs = [pl.BlockSpec((None, tq, 2 * LANES), lambda b, i: (b, i, 0))]
    if with_lse:
        out_shape.append(jax.ShapeDtypeStruct((g, s_loc, LANES), F32))
        out_specs.append(pl.BlockSpec((None, tq, LANES), lambda b, i: (b, i, 0)))
    body = functools.partial(_banded_body, tq=tq, sq=sq, span=span, s_loc=s_loc, wpad=wpad, max_dist=max_dist,
                             slopes=tuple(float(x) for x in slopes), wide=wide, with_lse=with_lse)
    return pl.pallas_call(
        body, out_shape=tuple(out_shape), grid=(g, s_loc // tq), in_specs=in_specs, out_specs=tuple(out_specs),
        compiler_params=_params(("parallel", "arbitrary")), name=name,
    )(*arrs)


def _compress_body(k_ref, v_ref, pe_ref, w1_ref, w2_ref, o_ref):
    n = k_ref.shape[0]
    hid = []
    for idx, ref in enumerate((k_ref, v_ref)):
        ch = ref[...].astype(F32)
        a = jnp.dot((ch + pe_ref[idx, 0]).astype(BF16), w1_ref[idx, 0], preferred_element_type=F32)
        b = jnp.dot((ch + pe_ref[idx, 1]).astype(BF16), w1_ref[idx, 1], preferred_element_type=F32)
        pre = a + pltpu.roll(b, n - 1, 0)
        hid.append((pre * jax.nn.sigmoid(pre)).astype(BF16))
    o_ref[...] = (jnp.dot(hid[0], w2_ref[0], preferred_element_type=F32)
                  + jnp.dot(hid[1], w2_ref[1], preferred_element_type=F32)).astype(o_ref.dtype)


def _compress(kch, vch, pe, w1, w2):
    bsz, n, w = kch.shape
    full = lambda *shape: pl.BlockSpec(shape, lambda b: (0,) * len(shape))
    blk = pl.BlockSpec((None, n, w), lambda b: (b, 0, 0))
    return pl.pallas_call(
        _compress_body,
        out_shape=jax.ShapeDtypeStruct((bsz, n, 2 * LANES), BF16),
        grid=(bsz,),
        in_specs=[blk, blk, full(2, 2, 1, w), full(2, 2, w, NSA_CMP_HIDDEN), full(2, NSA_CMP_HIDDEN, 2 * LANES)],
        out_specs=pl.BlockSpec((None, n, 2 * LANES), lambda b: (b, 0, 0)),
        compiler_params=_params(("parallel",)),
        name="nsa_compress",
    )(kch, vch, pe, w1, w2)


def _cmp_select_body(q_ref, kv_ref, ov_ref, o_ref, sb_ref, *, tq, slopes, n_top):
    i = pl.program_id(1)
    t0 = i * tq
    n_cmp = kv_ref.shape[0]
    tpos = t0 + lax.broadcasted_iota(jnp.int32, (tq, 1), 0)
    cend = NSA_CMP_STRIDE * lax.broadcasted_iota(jnp.int32, (1, n_cmp), 1) + (NSA_CMP_LEN - 1)
    dist = tpos - cend
    mask = dist >= 0
    distf = dist.astype(F32)
    kv = kv_ref[...]
    s_all = _dot_nt(_stack_wide_q(q_ref), kv[:, :LANES])
    psum = jnp.zeros((tq, n_cmp), F32)
    ps = []
    for h in range(4):
        s = jnp.where(mask, s_all[h * tq:(h + 1) * tq] - (float(slopes[h]) * LOG2E) * distf, -jnp.inf)
        m = jnp.max(s, axis=-1, keepdims=True)
        m = jnp.where(m == -jnp.inf, 0.0, m)
        e = jnp.where(mask, jnp.exp2(s - m), 0.0)
        p = e / jnp.maximum(jnp.sum(e, axis=-1, keepdims=True), 1e-30)
        ps.append(p.astype(BF16))
        psum = psum + p
    o = jnp.dot(jnp.concatenate(ps, axis=0), kv, preferred_element_type=F32)
    o_ref[...] = _unstack_wide_o(o, tq).astype(o_ref.dtype)

    imp = lax.dot_general(ov_ref[...], psum, (((1,), (1,)), ((), ())), precision=lax.Precision.HIGHEST,
                          preferred_element_type=F32)
    jrow = lax.broadcasted_iota(jnp.int32, (LANES, tq), 0)
    j = jrow - HEAD_DIM
    tl = t0 + lax.broadcasted_iota(jnp.int32, (LANES, tq), 1)
    cur = lax.shift_right_logical(tl, 6)
    causal = (j >= 0) & (j * NSA_SEL_LEN <= tl)
    forced = (j == 0) | (j == cur) | (j == cur - 1)
    score = jnp.where(causal, jnp.where(forced, jnp.inf, imp), -jnp.inf)
    rank = jnp.zeros((LANES, tq), F32)
    for jp in range(HEAD_DIM, LANES):
        other = score[jp:jp + 1, :]
        rank = rank + jnp.where(other > score, 1.0, jnp.where((other == score) & (jrow > jp), 1.0, 0.0))
    keep = causal & (rank < float(n_top))
    bias_t = jnp.where(keep, 0.0, jnp.where(j >= 0, MASKED, 0.0))
    sb_ref[...] = bias_t.T.astype(sb_ref.dtype)


def _cmp_select(zm, kv_c, overlap_t, tq, slopes, n_top):
    bsz, s, _ = zm.shape
    n_cmp = kv_c.shape[1]
    return pl.pallas_call(
        functools.partial(_cmp_select_body, tq=tq, slopes=tuple(float(x) for x in slopes), n_top=n_top),
        out_shape=(jax.ShapeDtypeStruct((bsz, s, 2 * LANES), BF16), jax.ShapeDtypeStruct((bsz, s, LANES), BF16)),
        grid=(bsz, s // tq),
        in_specs=[pl.BlockSpec((None, tq, 4 * LANES), lambda b, i: (b, i, U_NSA_Q // 4)),
                  pl.BlockSpec((None, n_cmp, 2 * LANES), lambda b, i: (b, 0, 0)),
                  pl.BlockSpec((LANES, n_cmp), lambda b, i: (0, 0))],
        out_specs=(pl.BlockSpec((None, tq, 2 * LANES), lambda b, i: (b, i, 0)),
                   pl.BlockSpec((None, tq, LANES), lambda b, i: (b, i, 0))),
        compiler_params=_params(("parallel", "parallel")),
        name="nsa_cmp_select",
    )(zm, kv_c, overlap_t)


def _sel_body(q_ref, sb_ref, kv_ref, o_ref, kp_ref, qs_ref, s_a, s_b, p_ref, *stats, t, slopes):
    i = pl.program_id(1)
    s_len = kv_ref.shape[0]

    @pl.when(i == 0)
    def _():
        pos = lax.broadcasted_iota(jnp.int32, (s_len, LANES), 0)
        ln = lax.broadcasted_iota(jnp.int32, (s_len, LANES), 1)
        onehot = jnp.where(lax.shift_right_logical(pos, 6) == ln - HEAD_DIM, 1.0, 0.0).astype(BF16)
        kp_ref[...] = jnp.where(ln < HEAD_DIM, kv_ref[:, :LANES], onehot)

    lane = lax.broadcasted_iota(jnp.int32, (t, LANES), 1)
    sb = sb_ref[...]
    for h in range(4):
        qs_ref[h * t:(h + 1) * t, :] = jnp.where(lane < HEAD_DIM, q_ref[:, h * LANES:(h + 1) * LANES], sb)
    krel0 = lax.broadcasted_iota(jnp.int32, (1, t), 1)

    def rows(j):
        return pl.ds(pl.multiple_of(j * t, t), t)

    def produce(s_ref, j):
        s = _dot_nt(qs_ref[...], kp_ref[rows(j), :])
        krel = (krel0 + (j - i) * t).astype(F32)
        for h in range(4):
            s_ref[h * t:(h + 1) * t, :] = s[h * t:(h + 1) * t] + (float(slopes[h]) * LOG2E) * krel

    def consume(s_ref, j, diag):
        _flash_consume(s_ref, p_ref, stats, kv_ref[rows(j), :], t, diag)

    o = _flash_causal(i, produce, consume, s_a, s_b, stats)
    o_ref[...] = _unstack_wide_o(o, t).astype(o_ref.dtype)


def _sel(zm, sbias, t, slopes):
    bsz, s, _ = zm.shape
    return pl.pallas_call(
        functools.partial(_sel_body, t=t, slopes=tuple(float(x) for x in slopes)),
        out_shape=jax.ShapeDtypeStruct((bsz, s, 2 * LANES), BF16),
        grid=(bsz, s // t),
        in_specs=[pl.BlockSpec((None, t, 4 * LANES), lambda b, i: (b, i, U_NSA_Q // 4)),
                  pl.BlockSpec((None, t, LANES), lambda b, i: (b, i, 0)),
                  pl.BlockSpec((None, s, 2 * LANES), lambda b, i: (b, 0, U_SLC // 2))],
        out_specs=pl.BlockSpec((None, t, 2 * LANES), lambda b, i: (b, i, 0)),
        scratch_shapes=[pltpu.VMEM((s, LANES), BF16), pltpu.VMEM((4 * t, LANES), BF16)] + _flash_scratch(t),
        compiler_params=_params(("parallel", "arbitrary"), VMEM_LIMIT),
        name="nsa_sel_attn",
    )(zm, sbias, zm)


def _outproj_body(x_ref, ofox_ref, ocmp_ref, oslc_ref, owin_ref, zs_ref, oswa_ref, lswa_ref, sink_ref,
                  od1_ref, od4_ref, od16_ref, l1_ref, l4_ref, l16_ref, eg_ref, eh_ref, w_ref, o_ref):
    hi = lax.Precision.HIGHEST

    def spread(vals, e):
        return jnp.dot(vals, e, precision=hi, preferred_element_type=F32)

    gate = jax.nn.sigmoid(zs_ref[...])
    o_nsa = (spread(gate, eg_ref[0]) * ocmp_ref[...].astype(F32)
             + spread(gate, eg_ref[1]) * oslc_ref[...].astype(F32)
             + spread(gate, eg_ref[2]) * owin_ref[...].astype(F32))
    eh = eh_ref[...]
    keep = jax.nn.sigmoid(lswa_ref[...] - sink_ref[...])
    o_swa = spread(keep, eh) * oswa_ref[...].astype(F32)
    l1, l4, l16 = l1_ref[...], l4_ref[...], l16_ref[...]
    m = jnp.maximum(jnp.maximum(l1, l4), l16)
    e1, e4, e16 = jnp.exp(l1 - m), jnp.exp(l4 - m), jnp.exp(l16 - m)
    inv = 1.0 / (e1 + e4 + e16)
    o_dil = (spread(e1 * inv, eh) * od1_ref[...].astype(F32) + spread(e4 * inv, eh) * od4_ref[...].astype(F32)
             + spread(e16 * inv, eh) * od16_ref[...].astype(F32))
    y = x_ref[...]
    for g, o in enumerate((ofox_ref[...], o_nsa.astype(BF16), o_swa.astype(BF16), o_dil.astype(BF16))):
        y = y + jnp.dot(o, w_ref[g], preferred_element_type=F32)
    o_ref[...] = y


def _outproj(x, heads, zs, lses, sinks, e_gate, e_head, w):
    n, d = x.shape
    tm = 512 if n % 512 == 0 else n
    row = lambda i: (i, 0)
    wide = pl.BlockSpec((tm, 2 * LANES), row)
    narrow = pl.BlockSpec((tm, LANES), row)
    ofox, ocmp, oslc, owin, oswa, od1, od4, od16 = heads
    lswa, l1, l4, l16 = lses
    return pl.pallas_call(
        _outproj_body,
        out_shape=jax.ShapeDtypeStruct((n, d), F32),
        grid=(n // tm,),
        in_specs=[pl.BlockSpec((tm, d), row), wide, wide, wide, wide, narrow, wide, narrow,
                  pl.BlockSpec((1, LANES), lambda i: (0, 0)), wide, wide, wide, narrow, narrow, narrow,
                  pl.BlockSpec((3, LANES, 2 * LANES), lambda i: (0, 0, 0)),
                  pl.BlockSpec((LANES, 2 * LANES), lambda i: (0, 0)),
                  pl.BlockSpec((4, 2 * LANES, d), lambda i: (0, 0, 0))],
        out_specs=pl.BlockSpec((tm, d), row),
        compiler_params=_params(("parallel",), VMEM_LIMIT),
        name="outproj",
    )(x, ofox, ocmp, oslc, owin, zs, oswa, lswa, sinks, od1, od4, od16, l1, l4, l16, e_gate, e_head, w)


def _spread_matrices():
    e_gate = np.zeros((3, LANES, 2 * LANES), np.float32)
    e_head = np.zeros((LANES, 2 * LANES), np.float32)
    for h in range(4):
        e_head[h, 64 * h:64 * (h + 1)] = 1.0
        for br in range(3):
            e_gate[br, GATE_LANE0 + 3 * h + br, 64 * h:64 * (h + 1)] = 1.0
    return jnp.asarray(e_gate), jnp.asarray(e_head)


def _overlap_t(n_cmp, n_sel):
    cs = np.arange(n_cmp) * NSA_CMP_STRIDE
    ss = np.arange(n_sel) * NSA_SEL_LEN
    ov = (cs[None, :] <= ss[:, None] + NSA_SEL_LEN - 1) & (cs[None, :] + NSA_CMP_LEN - 1 >= ss[:, None])
    out = np.zeros((LANES, n_cmp), np.float32)
    out[HEAD_DIM:HEAD_DIM + n_sel] = ov
    return jnp.asarray(out)


def _mixer(x, bsz, s, g_mix, w_in, fox_b_f, cmp_pe, cmp_w1, cmp_w2, swa_sinks, w_out):
    n, d = x.shape
    sl_swa, sl_nsa, sl_dil = _alibi_slopes()
    cols, scale = _w_in_layout()
    w_perm = (w_in[:, cols] * scale).astype(BF16)
    zm, zd, zs = _inproj(x, g_mix, w_perm)
    zm = zm.reshape(bsz, s, N_MAIN * LANES)
    zd = zd.reshape(bsz, s, N_DIL * LANES)

    t_fox = min(512, s)
    f_rows = jnp.transpose(zs.reshape(bsz, s, LANES)[:, :, :4], (0, 2, 1)).reshape(bsz, 2, 2, s)
    f_rows = jnp.pad(f_rows, ((0, 0), (0, 0), (0, 6), (0, 0)))
    b_rows = jnp.pad(fox_b_f.reshape(2, 2, 1), ((0, 0), (0, 6), (0, 0)))
    c = _fox_cumsum(f_rows, b_rows, t_fox)
    o_fox = _fox(zm, c, t_fox)

    n_chunk = s // NSA_CMP_STRIDE
    n_sel = s // NSA_SEL_LEN
    assert n_sel <= HEAD_DIM
    kvc = zm[:, :, U_CMP * LANES:(U_CMP + 1) * LANES]
    kch = kvc[:, :, :HEAD_DIM].reshape(bsz, n_chunk, NSA_CMP_STRIDE * HEAD_DIM)
    vch = kvc[:, :, HEAD_DIM:].reshape(bsz, n_chunk, NSA_CMP_STRIDE * HEAD_DIM)
    pe = cmp_pe.reshape(2, 2, 1, NSA_CMP_STRIDE * HEAD_DIM)
    w1 = cmp_w1.reshape(2, 2, NSA_CMP_STRIDE * HEAD_DIM, NSA_CMP_HIDDEN).astype(BF16)
    zpad = jnp.zeros((NSA_CMP_HIDDEN, HEAD_DIM), F32)
    w2 = jnp.stack([jnp.concatenate([cmp_w2[0], zpad, zpad, cmp_w2[0]], axis=1),
                    jnp.concatenate([zpad, cmp_w2[1], cmp_w2[1], zpad], axis=1)]).astype(BF16)
    kv_c = _compress(kch, vch, pe, w1, w2)
    o_cmp, sbias = _cmp_select(zm, kv_c, _overlap_t(n_chunk, n_sel), 256, sl_nsa, min(NSA_TOPN, n_sel))
    o_slc = _sel(zm, sbias, min(512, s), sl_nsa)
    (o_win,) = _banded((zm, zm), (U_NSA_Q // 4, U_WIN // 2), sl_nsa, NSA_WINDOW - 1, NSA_WINDOW,
                       True, False, "nsa_win_attn")

    o_swa, l_swa = _banded((zm, zm, zm), (U_SWA_Q // 2, U_SWA_K // 2, U_SWA_V // 2), sl_swa, SWA_WINDOW - 1,
                           SWA_WINDOW, False, True, "swa_attn")

    o_dil, l_dil = [], []
    for window, dd in DIL_PAIRS:
        zz = zd if dd == 1 else jnp.transpose(zd.reshape(bsz, s // dd, dd, N_DIL * LANES), (0, 2, 1, 3)).reshape(
            bsz * dd, s // dd, N_DIL * LANES)
        o, l = _banded((zz, zz, zz), (0, 1, 2), sl_dil * dd, window // dd, LANES, False, True, "dil%d_attn" % dd)
        if dd > 1:
            o = jnp.transpose(o.reshape(bsz, dd, s // dd, 2 * LANES), (0, 2, 1, 3))
            l = jnp.transpose(l.reshape(bsz, dd, s // dd, LANES), (0, 2, 1, 3))
        o_dil.append(o.reshape(n, 2 * LANES))
        l_dil.append(l.reshape(n, LANES))

    e_gate, e_head = _spread_matrices()
    sinks = jnp.pad(swa_sinks.reshape(1, 4), ((0, 0), (0, LANES - 4)))
    flat = lambda a: a.reshape(n, a.shape[-1])
    heads = (flat(o_fox), flat(o_cmp), flat(o_slc), flat(o_win), flat(o_swa), *o_dil)
    return _outproj(x, heads, zs, (flat(l_swa), *l_dil), sinks, e_gate, e_head,
                    w_out.reshape(4, 2 * LANES, d).astype(BF16))


def kernel(x, norm_ffn1, ffn1_w_gate, ffn1_w_up, ffn1_w_down, norm_mix, w_in, fox_b_f, nsa_cmp_pe, nsa_cmp_w1,
           nsa_cmp_w2, swa_sinks, w_out, norm_ffn2, ffn2_w_gate, ffn2_w_up, ffn2_w_down, norm_final):
    bsz, s, d = x.shape
    depth = norm_ffn1.shape[0]
    h = x.reshape(bsz * s, d)
    for l in range(depth):
        h = _ffn(h, norm_ffn1[l], ffn1_w_gate[l].astype(BF16), ffn1_w_up[l].astype(BF16), ffn1_w_down[l].astype(BF16))
        h = _mixer(h, bsz, s, norm_mix[l], w_in[l], fox_b_f[l], nsa_cmp_pe[l], nsa_cmp_w1[l], nsa_cmp_w2[l],
                   swa_sinks[l], w_out[l])
        h = _ffn(h, norm_ffn2[l], ffn2_w_gate[l].astype(BF16), ffn2_w_up[l].astype(BF16), ffn2_w_down[l].astype(BF16),
                 norm_final if l == depth - 1 else None)
    return h.reshape(bsz, s, d)
```

```python
import functools

import numpy as np
import jax
import jax.numpy as jnp
from jax import lax
from jax.experimental import pallas as pl
from jax.experimental.pallas import tpu as pltpu

F32 = jnp.float32
BF16 = jnp.bfloat16

HEAD_DIM = 64
LANES = 128
NSA_CMP_LEN = 32
NSA_CMP_STRIDE = 16
NSA_CMP_HIDDEN = 128
NSA_SEL_LEN = 64
NSA_TOPN = 16
NSA_WINDOW = 512
SWA_WINDOW = 128
DIL_PAIRS = ((128, 1), (512, 4), (2048, 16))
RMS_EPS = 1e-6
LOG2E = 1.4426950408889634
LN2 = 0.6931471805599453
MASKED = -1e30
VMEM_LIMIT = 56 * 1024 * 1024

IN_SPLITS = (
    ('fox_q', 256), ('fox_k', 256), ('fox_v', 256), ('fox_f', 4),
    ('nsa_q', 256), ('nsa_k_cmp', 64), ('nsa_v_cmp', 64), ('nsa_k_slc', 64), ('nsa_v_slc', 64),
    ('nsa_k_win', 64), ('nsa_v_win', 64), ('nsa_gate', 12),
    ('swa_q', 256), ('swa_k', 128), ('swa_v', 128),
    ('dil_q', 256), ('dil_k', 256), ('dil_v', 256),
)

U_NSA_Q = 0
U_FOX_Q, U_FOX_K, U_FOX_V = 4, 6, 8
U_SWA_Q, U_SWA_K, U_SWA_V = 10, 12, 14
U_SLC = 16
U_WIN = 18
U_CMP = 20
N_MAIN = 21
N_DIL = 6
N_UNITS = N_MAIN + N_DIL + 1
GATE_LANE0 = 4


def _alibi_slopes():
    n = 12
    s = 2.0 ** (-8.0 * np.arange(1, n + 1) / n)
    return s[:4], s[4:8], s[8:]


def _w_in_layout():
    off, o = {}, 0
    for name, w in IN_SPLITS:
        off[name] = o
        o += w
    cols, scale = [], []

    def seg(name, start, width, sc=1.0):
        cols.extend(range(off[name] + start, off[name] + start + width))
        scale.extend([sc] * width)

    def zeros(n):
        cols.extend([0] * n)
        scale.extend([0.0] * n)

    qs = HEAD_DIM ** -0.5 * LOG2E
    for h in range(4):
        seg('nsa_q', 64 * h, 64, qs)
        zeros(64)
    seg('fox_q', 0, 256, qs)
    seg('fox_k', 0, 256)
    seg('fox_v', 0, 256)
    seg('swa_q', 0, 256, qs)
    for name in ('swa_k', 'swa_v'):
        for kv in range(2):
            seg(name, 64 * kv, 64)
            seg(name, 64 * kv, 64)
    for a, b in (('nsa_k_slc', 'nsa_v_slc'), ('nsa_v_slc', 'nsa_k_slc'),
                 ('nsa_k_win', 'nsa_v_win'), ('nsa_v_win', 'nsa_k_win')):
        seg(a, 0, 64)
        seg(b, 0, 64)
    seg('nsa_k_cmp', 0, 64)
    seg('nsa_v_cmp', 0, 64)
    seg('dil_q', 0, 256, qs)
    seg('dil_k', 0, 256)
    seg('dil_v', 0, 256)
    seg('fox_f', 0, 4)
    seg('nsa_gate', 0, 12)
    zeros(LANES - 16)
    assert len(cols) == N_UNITS * LANES
    return np.asarray(cols, np.int32), np.asarray(scale, np.float32)


def _rms(x, g):
    ms = jnp.mean(x * x, axis=-1, keepdims=True)
    return x * lax.rsqrt(ms + RMS_EPS) * g


def _dot_nt(a, b):
    return lax.dot_general(a, b, (((1,), (1,)), ((), ())), preferred_element_type=F32)


def _params(sem, vmem=None):
    return pltpu.CompilerParams(dimension_semantics=sem, vmem_limit_bytes=vmem)


def _stack_packed_q(q):
    slot = lax.shift_right_logical(lax.broadcasted_iota(jnp.int32, q.shape, 1), 6)
    return jnp.concatenate([jnp.where(slot == h, q, jnp.zeros_like(q)) for h in range(4)], axis=0)


def _unstack_packed_o(o, t):
    slot = lax.shift_right_logical(lax.broadcasted_iota(jnp.int32, (t, 2 * LANES), 1), 6)
    out = o[:t]
    for h in range(1, 4):
        out = jnp.where(slot == h, o[h * t:(h + 1) * t], out)
    return out


def _stack_wide_q(q_ref):
    return jnp.concatenate([q_ref[:, h * LANES:(h + 1) * LANES] for h in range(4)], axis=0)


def _unstack_wide_o(o, t):
    lane = lax.broadcasted_iota(jnp.int32, (t, LANES), 1)
    pairs = [jnp.where(lane < HEAD_DIM, o[2 * pr * t:(2 * pr + 1) * t, LANES:], o[(2 * pr + 1) * t:(2 * pr + 2) * t, :LANES])
             for pr in range(2)]
    return jnp.concatenate(pairs, axis=1)


FLASH_ROW_BLOCK = 64


def _flash_consume(s_ref, p_ref, stats, v, t, diag):
    m_ref, alpha_ref, l_ref, acc_ref = stats
    n = s_ref.shape[1]
    rb = FLASH_ROW_BLOCK
    lanes = [slice(c * LANES, (c + 1) * LANES) for c in range(n // LANES)]

    def scores(r0):
        s = s_ref[r0:r0 + rb, :]
        if diag:
            row = lax.broadcasted_iota(jnp.int32, (rb, n), 0) + r0 % t
            col = lax.broadcasted_iota(jnp.int32, (rb, n), 1)
            s = jnp.where(col <= row, s, MASKED)
        return s

    for r0 in range(0, 4 * t, rb):
        s = scores(r0)
        mx = s[:, lanes[0]]
        for c in lanes[1:]:
            mx = jnp.maximum(mx, s[:, c])
        m_old = m_ref[r0:r0 + rb, :]
        m_new = jnp.maximum(m_old, jnp.max(mx, axis=-1, keepdims=True))
        alpha_ref[r0:r0 + rb, :] = jnp.exp2(m_old - m_new)
        m_ref[r0:r0 + rb, :] = m_new
    for r0 in range(0, 4 * t, rb):
        s = scores(r0)
        m_new = m_ref[r0:r0 + rb, :]
        psum = jnp.zeros((rb, LANES), F32)
        for c in lanes:
            p = jnp.exp2(s[:, c] - m_new)
            psum = psum + p
            p_ref[r0:r0 + rb, c] = p.astype(BF16)
        l_ref[r0:r0 + rb, :] = alpha_ref[r0:r0 + rb, :] * l_ref[r0:r0 + rb, :] + psum
    alpha = alpha_ref[...]
    acc_ref[...] = (jnp.concatenate([alpha, alpha], axis=1) * acc_ref[...]
                    + jnp.dot(p_ref[...], v, preferred_element_type=F32))


def _flash_causal(i, produce, consume, s_a, s_b, stats):
    m_ref, alpha_ref, l_ref, acc_ref = stats
    m_ref[...] = jnp.full(m_ref.shape, MASKED, F32)
    l_ref[...] = jnp.zeros(l_ref.shape, F32)
    acc_ref[...] = jnp.zeros(acc_ref.shape, F32)
    produce(s_a, 0)

    def pair(jj, carry):
        j = 2 * jj
        produce(s_b, j + 1)
        consume(s_a, j, False)
        produce(s_a, j + 2)
        consume(s_b, j + 1, False)
        return carry

    lax.fori_loop(0, i // 2, pair, 0)

    @pl.when(i % 2 == 1)
    def _():
        consume(s_a, i - 1, False)
        produce(s_b, i)
        consume(s_b, i, True)

    @pl.when(i % 2 == 0)
    def _():
        consume(s_a, i, True)

    return acc_ref[...] * (1.0 / jnp.sum(l_ref[...], axis=-1, keepdims=True))


def _flash_scratch(t):
    return [pltpu.VMEM((4 * t, t), F32), pltpu.VMEM((4 * t, t), F32), pltpu.VMEM((4 * t, t), BF16),
            pltpu.VMEM((4 * t, LANES), F32), pltpu.VMEM((4 * t, LANES), F32), pltpu.VMEM((4 * t, LANES), F32),
            pltpu.VMEM((4 * t, 2 * LANES), F32)]


def _ffn_body(x_ref, g_ref, wg_ref, wu_ref, wd_ref, *rest, final):
    o_ref = rest[-1]
    x = x_ref[...]
    xn = _rms(x, g_ref[...]).astype(BF16)
    a = jnp.dot(xn, wg_ref[...], preferred_element_type=F32)
    b = jnp.dot(xn, wu_ref[...], preferred_element_type=F32)
    h = (a * jax.nn.sigmoid(a) * b).astype(BF16)
    y = x + 0.5 * jnp.dot(h, wd_ref[...], preferred_element_type=F32)
    if final:
        y = _rms(y, rest[0][...])
    o_ref[...] = y


def _ffn(x, g, wg, wu, wd, g_final=None):
    n, d = x.shape
    f = wg.shape[1]
    tm = 512 if n % 512 == 0 else n
    const = lambda i: (0, 0)
    once = pl.Buffered(1)
    in_specs = [
        pl.BlockSpec((tm, d), lambda i: (i, 0)),
        pl.BlockSpec((1, d), const),
        pl.BlockSpec((d, f), const, pipeline_mode=once),
        pl.BlockSpec((d, f), const, pipeline_mode=once),
        pl.BlockSpec((f, d), const, pipeline_mode=once),
    ]
    args = [x, g.reshape(1, d), wg, wu, wd]
    if g_final is not None:
        in_specs.append(pl.BlockSpec((1, d), const))
        args.append(g_final.reshape(1, d))
    return pl.pallas_call(
        functools.partial(_ffn_body, final=g_final is not None),
        out_shape=jax.ShapeDtypeStruct((n, d), F32),
        grid=(n // tm,),
        in_specs=in_specs,
        out_specs=pl.BlockSpec((tm, d), lambda i: (i, 0)),
        compiler_params=_params(("parallel",), VMEM_LIMIT),
        name="ffn",
    )(*args)


def _inproj_body(x_ref, g_ref, w_ref, zm_ref, zs_ref, *rest):
    zd_refs, zscr = rest[:-1], rest[-1]
    tm = x_ref.shape[0]
    xn = _rms(x_ref[...], g_ref[...]).astype(BF16)
    z = jnp.dot(xn, w_ref[...], preferred_element_type=F32)
    zm_ref[...] = z[:, :N_MAIN * LANES].astype(BF16)
    zs_ref[...] = z[:, (N_MAIN + N_DIL) * LANES:]
    for u in range(N_DIL):
        zscr[u] = z[:, (N_MAIN + u) * LANES:(N_MAIN + u + 1) * LANES]
    for (_, dd), ref in zip(DIL_PAIRS, zd_refs):
        for u in range(N_DIL):
            if dd == 1:
                ref[0, :, u * LANES:(u + 1) * LANES] = zscr[u].astype(BF16)
            else:
                for r in range(dd):
                    ref[r, :, u * LANES:(u + 1) * LANES] = zscr[u, pl.ds(r, tm // dd, stride=dd), :].astype(BF16)


def _inproj(x, g, w):
    bsz, s, d = x.shape
    tm = min(512, s)
    const = lambda b, i: (0, 0)
    row = lambda b, i: (b, i, 0)
    wd = N_DIL * LANES
    zd_shapes = tuple(jax.ShapeDtypeStruct((bsz, dd, s // dd, wd), BF16) for _, dd in DIL_PAIRS)
    zd_specs = tuple(pl.BlockSpec((None, dd, tm // dd, wd), lambda b, i: (b, 0, i, 0)) for _, dd in DIL_PAIRS)
    return pl.pallas_call(
        _inproj_body,
        out_shape=(jax.ShapeDtypeStruct((bsz, s, N_MAIN * LANES), BF16), jax.ShapeDtypeStruct((bsz, s, LANES), F32))
        + zd_shapes,
        grid=(bsz, s // tm),
        in_specs=[pl.BlockSpec((None, tm, d), row), pl.BlockSpec((1, d), const),
                  pl.BlockSpec((d, N_UNITS * LANES), const, pipeline_mode=pl.Buffered(1))],
        out_specs=(pl.BlockSpec((None, tm, N_MAIN * LANES), row), pl.BlockSpec((None, tm, LANES), row)) + zd_specs,
        scratch_shapes=[pltpu.VMEM((N_DIL, tm, LANES), F32)],
        compiler_params=_params(("parallel", "parallel"), VMEM_LIMIT),
        name="inproj",
    )(x, g.reshape(1, d), w)


def _cumsum_body(x_ref, b_ref, u_ref, c_ref, *, tk):
    x = x_ref[...] + b_ref[...]
    lf = (jnp.minimum(x, 0.0) - jnp.log(1.0 + jnp.exp(-jnp.abs(x)))) * LOG2E
    u = u_ref[...]
    carry = jnp.zeros((8, 1), F32)
    per = tk // LANES
    for j in range(x.shape[1] // LANES):
        seg = lf[:, j * LANES:(j + 1) * LANES]
        cs = jnp.dot(seg, u, precision=lax.Precision.HIGHEST, preferred_element_type=F32) + carry
        c_ref[j // per, :, (j % per) * LANES:(j % per + 1) * LANES] = cs
        carry = cs[:, LANES - 1:LANES]


def _fox_cumsum(f_rows, b_rows, tk):
    bsz, _, _, s = f_rows.shape
    u = jnp.asarray(np.triu(np.ones((LANES, LANES), np.float32)))
    return pl.pallas_call(
        functools.partial(_cumsum_body, tk=tk),
        out_shape=jax.ShapeDtypeStruct((bsz, 2, s // tk, 8, tk), F32),
        grid=(bsz, 2),
        in_specs=[pl.BlockSpec((None, None, 8, s), lambda b, p: (b, p, 0, 0)),
                  pl.BlockSpec((None, 8, 1), lambda b, p: (p, 0, 0)),
                  pl.BlockSpec((LANES, LANES), lambda b, p: (0, 0))],
        out_specs=pl.BlockSpec((None, None, s // tk, 8, tk), lambda b, p: (b, p, 0, 0, 0)),
        compiler_params=_params(("parallel", "parallel")),
        name="fox_cumsum",
    )(f_rows, b_rows, u)


def _fox_body(q_ref, k_ref, v_ref, c_ref, o_ref, qs_ref, s_a, s_b, p_ref, *stats, t):
    i = pl.program_id(1)
    qs_ref[...] = _stack_packed_q(q_ref[...])

    def rows(j):
        return pl.ds(pl.multiple_of(j * t, t), t)

    def produce(s_ref, j):
        s = _dot_nt(qs_ref[...], k_ref[rows(j), :])
        for h in range(4):
            s_ref[h * t:(h + 1) * t, :] = s[h * t:(h + 1) * t] - c_ref[h // 2, j, (h % 2):(h % 2) + 1, :]

    def consume(s_ref, j, diag):
        _flash_consume(s_ref, p_ref, stats, v_ref[rows(j), :], t, diag)

    o = _flash_causal(i, produce, consume, s_a, s_b, stats)
    o_ref[...] = _unstack_packed_o(o, t).astype(o_ref.dtype)


def _fox(zm, c, t):
    bsz, s, _ = zm.shape
    return pl.pallas_call(
        functools.partial(_fox_body, t=t),
        out_shape=jax.ShapeDtypeStruct((bsz, s, 2 * LANES), BF16),
        grid=(bsz, s // t),
        in_specs=[pl.BlockSpec((None, t, 2 * LANES), lambda b, i: (b, i, U_FOX_Q // 2)),
                  pl.BlockSpec((None, s, 2 * LANES), lambda b, i: (b, 0, U_FOX_K // 2)),
                  pl.BlockSpec((None, s, 2 * LANES), lambda b, i: (b, 0, U_FOX_V // 2)),
                  pl.BlockSpec((None, 2, s // t, 8, t), lambda b, i: (b, 0, 0, 0, 0))],
        out_specs=pl.BlockSpec((None, t, 2 * LANES), lambda b, i: (b, i, 0)),
        scratch_shapes=[pltpu.VMEM((4 * t, 2 * LANES), BF16)] + _flash_scratch(t),
        compiler_params=_params(("parallel", "arbitrary"), VMEM_LIMIT),
        name="fox_attn",
    )(zm, zm, zm, c)


def _banded_body(*refs, tq, sq, span, s_loc, wpad, max_dist, slopes, wide, with_lse):
    i = pl.program_id(1)
    o_ref = refs[2] if wide else refs[3]
    for r in range(tq // sq):
        t0 = i * tq + r * sq
        rows = slice(r * sq, (r + 1) * sq)
        start = pl.multiple_of(jnp.clip(t0 - wpad, 0, s_loc - span), LANES)
        kpos = start + lax.broadcasted_iota(jnp.int32, (1, span), 1)
        tpos = t0 + lax.broadcasted_iota(jnp.int32, (sq, 1), 0)
        dist = tpos - kpos
        mask = (dist >= 0) & (dist <= max_dist)
        krel = (kpos - t0).astype(F32)
        trel = (tpos - t0).astype(F32)
        if wide:
            q_ref, kv_ref = refs[:2]
            qs = jnp.concatenate([q_ref[rows, h * LANES:(h + 1) * LANES] for h in range(4)], axis=0)
            k = kv_ref[pl.ds(start, span), :LANES]
            v = kv_ref[pl.ds(start, span), :]
        else:
            q_ref, k_ref, v_ref = refs[:3]
            qs = _stack_packed_q(q_ref[rows, :])
            k = k_ref[pl.ds(start, span), :]
            v = v_ref[pl.ds(start, span), :]
        s = _dot_nt(qs, k)
        s = jnp.concatenate([jnp.where(mask, s[h * sq:(h + 1) * sq] + (float(slopes[h]) * LOG2E) * krel, MASKED)
                             for h in range(4)], axis=0)
        m = jnp.max(s, axis=-1, keepdims=True)
        e = jnp.exp2(s - m)
        l = jnp.sum(e, axis=-1, keepdims=True)
        o = jnp.dot(e.astype(BF16), v, preferred_element_type=F32) * (1.0 / l)
        o_ref[rows, :] = (_unstack_wide_o(o, sq) if wide else _unstack_packed_o(o, sq)).astype(o_ref.dtype)
        if with_lse:
            lse = (m + jnp.log2(l)) * LN2
            lane = lax.broadcasted_iota(jnp.int32, (sq, LANES), 1)
            acc = jnp.zeros((sq, LANES), F32)
            for h in range(4):
                acc = jnp.where(lane == h, lse[h * sq:(h + 1) * sq] - float(slopes[h]) * trel, acc)
            refs[-1][rows, :] = acc


def _banded(arrs, units, slopes, max_dist, wpad, wide, with_lse, name):
    g, s_loc, _ = arrs[0].shape
    tq = min(512, s_loc)
    sq = min(tq, wpad, 256)
    span = min(sq + wpad, s_loc)
    if wide:
        in_specs = [pl.BlockSpec((None, tq, 4 * LANES), lambda b, i: (b, i, units[0])),
                    pl.BlockSpec((None, s_loc, 2 * LANES), lambda b, i: (b, 0, units[1]))]
    else:
        in_specs = [pl.BlockSpec((None, tq, 2 * LANES), lambda b, i: (b, i, units[0])),
                    pl.BlockSpec((None, s_loc, 2 * LANES), lambda b, i: (b, 0, units[1])),
                    pl.BlockSpec((None, s_loc, 2 * LANES), lambda b, i: (b, 0, units[2]))]
    out_shape = [jax.ShapeDtypeStruct((g, s_loc, 2 * LANES), BF16)]
    out_specs = [pl.BlockSpec((None, tq, 2 * LANES), lambda b, i: (b, i, 0))]
    if with_lse:
        out_shape.append(jax.ShapeDtypeStruct((g, s_loc, LANES), F32))
        out_specs.append(pl.BlockSpec((None, tq, LANES), lambda b, i: (b, i, 0)))
    body = functools.partial(_banded_body, tq=tq, sq=sq, span=span, s_loc=s_loc, wpad=wpad, max_dist=max_dist,
                             slopes=tuple(float(x) for x in slopes), wide=wide, with_lse=with_lse)
    return pl.pallas_call(
        body, out_shape=tuple(out_shape), grid=(g, s_loc // tq), in_specs=in_specs, out_specs=tuple(out_specs),
        compiler_params=_params(("parallel", "arbitrary")), name=name,
    )(*arrs)


def _compress_body(k_ref, v_ref, pe_ref, w1_ref, w2_ref, o_ref):
    n = k_ref.shape[0]
    hid = []
    for idx, ref in enumerate((k_ref, v_ref)):
        ch = ref[...].astype(F32)
        a = jnp.dot((ch + pe_ref[idx, 0]).astype(BF16), w1_ref[idx, 0], preferred_element_type=F32)
        b = jnp.dot((ch + pe_ref[idx, 1]).astype(BF16), w1_ref[idx, 1], preferred_element_type=F32)
        pre = a + pltpu.roll(b, n - 1, 0)
        hid.append((pre * jax.nn.sigmoid(pre)).astype(BF16))
    o_ref[...] = (jnp.dot(hid[0], w2_ref[0], preferred_element_type=F32)
                  + jnp.dot(hid[1], w2_ref[1], preferred_element_type=F32)).astype(o_ref.dtype)


def _compress(kch, vch, pe, w1, w2):
    bsz, n, w = kch.shape
    full = lambda *shape: pl.BlockSpec(shape, lambda b: (0,) * len(shape))
    blk = pl.BlockSpec((None, n, w), lambda b: (b, 0, 0))
    return pl.pallas_call(
        _compress_body,
        out_shape=jax.ShapeDtypeStruct((bsz, n, 2 * LANES), BF16),
        grid=(bsz,),
        in_specs=[blk, blk, full(2, 2, 1, w), full(2, 2, w, NSA_CMP_HIDDEN), full(2, NSA_CMP_HIDDEN, 2 * LANES)],
        out_specs=pl.BlockSpec((None, n, 2 * LANES), lambda b: (b, 0, 0)),
        compiler_params=_params(("parallel",)),
        name="nsa_compress",
    )(kch, vch, pe, w1, w2)


def _cmp_select_body(q_ref, kv_ref, ov_ref, o_ref, sb_ref, *, tq, slopes, n_top):
    i = pl.program_id(1)
    t0 = i * tq
    n_cmp = kv_ref.shape[0]
    tpos = t0 + lax.broadcasted_iota(jnp.int32, (tq, 1), 0)
    cend = NSA_CMP_STRIDE * lax.broadcasted_iota(jnp.int32, (1, n_cmp), 1) + (NSA_CMP_LEN - 1)
    dist = tpos - cend
    mask = dist >= 0
    distf = dist.astype(F32)
    kv = kv_ref[...]
    s_all = _dot_nt(_stack_wide_q(q_ref), kv[:, :LANES])
    psum = jnp.zeros((tq, n_cmp), F32)
    ps = []
    for h in range(4):
        s = jnp.where(mask, s_all[h * tq:(h + 1) * tq] - (float(slopes[h]) * LOG2E) * distf, -jnp.inf)
        m = jnp.max(s, axis=-1, keepdims=True)
        m = jnp.where(m == -jnp.inf, 0.0, m)
        e = jnp.where(mask, jnp.exp2(s - m), 0.0)
        p = e / jnp.maximum(jnp.sum(e, axis=-1, keepdims=True), 1e-30)
        ps.append(p.astype(BF16))
        psum = psum + p
    o = jnp.dot(jnp.concatenate(ps, axis=0), kv, preferred_element_type=F32)
    o_ref[...] = _unstack_wide_o(o, tq).astype(o_ref.dtype)

    imp = lax.dot_general(ov_ref[...], psum, (((1,), (1,)), ((), ())), precision=lax.Precision.HIGHEST,
                          preferred_element_type=F32)
    j = lax.broadcasted_iota(jnp.int32, (HEAD_DIM, tq), 0)
    tl = t0 + lax.broadcasted_iota(jnp.int32, (HEAD_DIM, tq), 1)
    cur = lax.shift_right_logical(tl, 6)
    causal = j * NSA_SEL_LEN <= tl
    forced = (j == 0) | (j == cur) | (j == cur - 1)
    score = jnp.where(causal, jnp.where(forced, jnp.inf, imp), -jnp.inf)
    rank = jnp.zeros((HEAD_DIM, tq), F32)
    for jp in range(HEAD_DIM):
        other = score[jp:jp + 1, :]
        rank = rank + jnp.where(other > score, 1.0, jnp.where((other == score) & (j > jp), 1.0, 0.0))
    bias_t = jnp.where(causal & (rank < float(n_top)), 0.0, MASKED)
    sb_ref[...] = jnp.concatenate([jnp.zeros((HEAD_DIM, tq), F32), bias_t], axis=0).T.astype(sb_ref.dtype)


def _cmp_select(zm, kv_c, overlap_t, tq, slopes, n_top):
    bsz, s, _ = zm.shape
    n_cmp = kv_c.shape[1]
    return pl.pallas_call(
        functools.partial(_cmp_select_body, tq=tq, slopes=tuple(float(x) for x in slopes), n_top=n_top),
        out_shape=(jax.ShapeDtypeStruct((bsz, s, 2 * LANES), BF16), jax.ShapeDtypeStruct((bsz, s, LANES), BF16)),
        grid=(bsz, s // tq),
        in_specs=[pl.BlockSpec((None, tq, 4 * LANES), lambda b, i: (b, i, U_NSA_Q // 4)),
                  pl.BlockSpec((None, n_cmp, 2 * LANES), lambda b, i: (b, 0, 0)),
                  pl.BlockSpec((HEAD_DIM, n_cmp), lambda b, i: (0, 0))],
        out_specs=(pl.BlockSpec((None, tq, 2 * LANES), lambda b, i: (b, i, 0)),
                   pl.BlockSpec((None, tq, LANES), lambda b, i: (b, i, 0))),
        compiler_params=_params(("parallel", "parallel")),
        name="nsa_cmp_select",
    )(zm, kv_c, overlap_t)


def _sel_body(q_ref, sb_ref, kv_ref, o_ref, kp_ref, qs_ref, s_a, s_b, p_ref, *stats, t, slopes):
    i = pl.program_id(1)
    s_len = kv_ref.shape[0]

    @pl.when(i == 0)
    def _():
        pos = lax.broadcasted_iota(jnp.int32, (s_len, LANES), 0)
        ln = lax.broadcasted_iota(jnp.int32, (s_len, LANES), 1)
        onehot = jnp.where(lax.shift_right_logical(pos, 6) == ln - HEAD_DIM, 1.0, 0.0).astype(BF16)
        kp_ref[...] = jnp.where(ln < HEAD_DIM, kv_ref[:, :LANES], onehot)

    lane = lax.broadcasted_iota(jnp.int32, (t, LANES), 1)
    sb = sb_ref[...]
    for h in range(4):
        qs_ref[h * t:(h + 1) * t, :] = jnp.where(lane < HEAD_DIM, q_ref[:, h * LANES:(h + 1) * LANES], sb)
    krel0 = lax.broadcasted_iota(jnp.int32, (1, t), 1)

    def rows(j):
        return pl.ds(pl.multiple_of(j * t, t), t)

    def produce(s_ref, j):
        s = _dot_nt(qs_ref[...], kp_ref[rows(j), :])
        krel = (krel0 + (j - i) * t).astype(F32)
        for h in range(4):
            s_ref[h * t:(h + 1) * t, :] = s[h * t:(h + 1) * t] + (float(slopes[h]) * LOG2E) * krel

    def consume(s_ref, j, diag):
        _flash_consume(s_ref, p_ref, stats, kv_ref[rows(j), :], t, diag)

    o = _flash_causal(i, produce, consume, s_a, s_b, stats)
    o_ref[...] = _unstack_wide_o(o, t).astype(o_ref.dtype)


def _sel(zm, sbias, t, slopes):
    bsz, s, _ = zm.shape
    return pl.pallas_call(
        functools.partial(_sel_body, t=t, slopes=tuple(float(x) for x in slopes)),
        out_shape=jax.ShapeDtypeStruct((bsz, s, 2 * LANES), BF16),
        grid=(bsz, s // t),
        in_specs=[pl.BlockSpec((None, t, 4 * LANES), lambda b, i: (b, i, U_NSA_Q // 4)),
                  pl.BlockSpec((None, t, LANES), lambda b, i: (b, i, 0)),
                  pl.BlockSpec((None, s, 2 * LANES), lambda b, i: (b, 0, U_SLC // 2))],
        out_specs=pl.BlockSpec((None, t, 2 * LANES), lambda b, i: (b, i, 0)),
        scratch_shapes=[pltpu.VMEM((s, LANES), BF16), pltpu.VMEM((4 * t, LANES), BF16)] + _flash_scratch(t),
        compiler_params=_params(("parallel", "arbitrary"), VMEM_LIMIT),
        name="nsa_sel_attn",
    )(zm, sbias, zm)


def _spread_heads(vals, lane0, stride, tm):
    slot = lax.shift_right_logical(lax.broadcasted_iota(jnp.int32, (tm, 2 * LANES), 1), 6)
    out = jnp.broadcast_to(vals[:, lane0:lane0 + 1], (tm, 2 * LANES))
    for h in range(1, 4):
        c = lane0 + stride * h
        out = jnp.where(slot == h, vals[:, c:c + 1], out)
    return out


def _outproj_body(x_ref, ofox_ref, ocmp_ref, oslc_ref, owin_ref, zs_ref, oswa_ref, lswa_ref, sink_ref,
                  od1_ref, od4_ref, od16_ref, l1_ref, l4_ref, l16_ref, w_ref, o_ref, o_scr, l_scr):
    tm = x_ref.shape[0]

    def natural(ref, scr):
        dd, _, w = ref.shape
        if dd == 1:
            return ref[0].astype(F32)
        for u in range(w // LANES):
            for r in range(dd):
                scr[u, pl.ds(r, tm // dd, stride=dd), :] = ref[r, :, u * LANES:(u + 1) * LANES].astype(F32)
        return jnp.concatenate([scr[u] for u in range(w // LANES)], axis=1)

    gate = jax.nn.sigmoid(zs_ref[...])
    o_nsa = (_spread_heads(gate, GATE_LANE0, 3, tm) * ocmp_ref[...].astype(F32)
             + _spread_heads(gate, GATE_LANE0 + 1, 3, tm) * oslc_ref[...].astype(F32)
             + _spread_heads(gate, GATE_LANE0 + 2, 3, tm) * owin_ref[...].astype(F32))
    keep = jax.nn.sigmoid(lswa_ref[...] - sink_ref[...])
    o_swa = _spread_heads(keep, 0, 1, tm) * oswa_ref[...].astype(F32)
    l1, l4, l16 = natural(l1_ref, l_scr), natural(l4_ref, l_scr), natural(l16_ref, l_scr)
    m = jnp.maximum(jnp.maximum(l1, l4), l16)
    e1, e4, e16 = jnp.exp(l1 - m), jnp.exp(l4 - m), jnp.exp(l16 - m)
    inv = 1.0 / (e1 + e4 + e16)
    o_dil = _spread_heads(e1 * inv, 0, 1, tm) * natural(od1_ref, o_scr)
    o_dil = o_dil + _spread_heads(e4 * inv, 0, 1, tm) * natural(od4_ref, o_scr)
    o_dil = o_dil + _spread_heads(e16 * inv, 0, 1, tm) * natural(od16_ref, o_scr)
    y = x_ref[...]
    for g, o in enumerate((ofox_ref[...], o_nsa.astype(BF16), o_swa.astype(BF16), o_dil.astype(BF16))):
        y = y + jnp.dot(o, w_ref[g], preferred_element_type=F32)
    o_ref[...] = y


def _outproj(x, heads, zs, lses, sinks, w):
    bsz, s, d = x.shape
    tm = min(512, s)
    row = lambda b, i: (b, i, 0)
    wide = pl.BlockSpec((None, tm, 2 * LANES), row)
    narrow = pl.BlockSpec((None, tm, LANES), row)

    def strided(a):
        dd = a.shape[1]
        return pl.BlockSpec((None, dd, tm // dd, a.shape[-1]), lambda b, i: (b, 0, i, 0))

    ofox, ocmp, oslc, owin, oswa, od1, od4, od16 = heads
    lswa, l1, l4, l16 = lses
    return pl.pallas_call(
        _outproj_body,
        out_shape=jax.ShapeDtypeStruct((bsz, s, d), F32),
        grid=(bsz, s // tm),
        in_specs=[pl.BlockSpec((None, tm, d), row), wide, wide, wide, wide, narrow, wide, narrow,
                  pl.BlockSpec((1, LANES), lambda b, i: (0, 0)), strided(od1), strided(od4), strided(od16),
                  strided(l1), strided(l4), strided(l16),
                  pl.BlockSpec((4, 2 * LANES, d), lambda b, i: (0, 0, 0))],
        out_specs=pl.BlockSpec((None, tm, d), row),
        scratch_shapes=[pltpu.VMEM((2, tm, LANES), F32), pltpu.VMEM((1, tm, LANES), F32)],
        compiler_params=_params(("parallel", "parallel"), VMEM_LIMIT),
        name="outproj",
    )(x, ofox, ocmp, oslc, owin, zs, oswa, lswa, sinks, od1, od4, od16, l1, l4, l16, w)


def _overlap_t(n_cmp, n_sel):
    cs = np.arange(n_cmp) * NSA_CMP_STRIDE
    ss = np.arange(n_sel) * NSA_SEL_LEN
    ov = (cs[None, :] <= ss[:, None] + NSA_SEL_LEN - 1) & (cs[None, :] + NSA_CMP_LEN - 1 >= ss[:, None])
    out = np.zeros((HEAD_DIM, n_cmp), np.float32)
    out[:n_sel] = ov
    return jnp.asarray(out)


def _mixer(x, g_mix, w_in, fox_b_f, cmp_pe, cmp_w1, cmp_w2, swa_sinks, w_out):
    bsz, s, d = x.shape
    sl_swa, sl_nsa, sl_dil = _alibi_slopes()
    cols, scale = _w_in_layout()
    w_perm = (w_in[:, cols] * scale).astype(BF16)
    zm, zs, *zds = _inproj(x, g_mix, w_perm)

    t_fox = min(512, s)
    f_rows = jnp.transpose(zs[:, :, :4], (0, 2, 1)).reshape(bsz, 2, 2, s)
    f_rows = jnp.pad(f_rows, ((0, 0), (0, 0), (0, 6), (0, 0)))
    b_rows = jnp.pad(fox_b_f.reshape(2, 2, 1), ((0, 0), (0, 6), (0, 0)))
    c = _fox_cumsum(f_rows, b_rows, t_fox)
    o_fox = _fox(zm, c, t_fox)

    n_chunk = s // NSA_CMP_STRIDE
    n_sel = s // NSA_SEL_LEN
    assert n_sel <= HEAD_DIM
    kvc = zm[:, :, U_CMP * LANES:(U_CMP + 1) * LANES]
    kch = kvc[:, :, :HEAD_DIM].reshape(bsz, n_chunk, NSA_CMP_STRIDE * HEAD_DIM)
    vch = kvc[:, :, HEAD_DIM:].reshape(bsz, n_chunk, NSA_CMP_STRIDE * HEAD_DIM)
    pe = cmp_pe.reshape(2, 2, 1, NSA_CMP_STRIDE * HEAD_DIM)
    w1 = cmp_w1.reshape(2, 2, NSA_CMP_STRIDE * HEAD_DIM, NSA_CMP_HIDDEN).astype(BF16)
    zpad = jnp.zeros((NSA_CMP_HIDDEN, HEAD_DIM), F32)
    w2 = jnp.stack([jnp.concatenate([cmp_w2[0], zpad, zpad, cmp_w2[0]], axis=1),
                    jnp.concatenate([zpad, cmp_w2[1], cmp_w2[1], zpad], axis=1)]).astype(BF16)
    kv_c = _compress(kch, vch, pe, w1, w2)
    o_cmp, sbias = _cmp_select(zm, kv_c, _overlap_t(n_chunk, n_sel), 256, sl_nsa, min(NSA_TOPN, n_sel))
    o_slc = _sel(zm, sbias, min(512, s), sl_nsa)
    (o_win,) = _banded((zm, zm), (U_NSA_Q // 4, U_WIN // 2), sl_nsa, NSA_WINDOW - 1, NSA_WINDOW,
                       True, False, "nsa_win_attn")

    o_swa, l_swa = _banded((zm, zm, zm), (U_SWA_Q // 2, U_SWA_K // 2, U_SWA_V // 2), sl_swa, SWA_WINDOW - 1,
                           SWA_WINDOW, False, True, "swa_attn")

    o_dil, l_dil = [], []
    for (window, dd), zd in zip(DIL_PAIRS, zds):
        zz = zd.reshape(bsz * dd, s // dd, N_DIL * LANES)
        o, l = _banded((zz, zz, zz), (0, 1, 2), sl_dil * dd, window // dd, LANES, False, True, "dil%d_attn" % dd)
        o_dil.append(o.reshape(bsz, dd, s // dd, 2 * LANES))
        l_dil.append(l.reshape(bsz, dd, s // dd, LANES))

    sinks = jnp.pad(swa_sinks.reshape(1, 4), ((0, 0), (0, LANES - 4)))
    heads = (o_fox, o_cmp, o_slc, o_win, o_swa, *o_dil)
    return _outproj(x, heads, zs, (l_swa, *l_dil), sinks, w_out.reshape(4, 2 * LANES, d).astype(BF16))


def kernel(x, norm_ffn1, ffn1_w_gate, ffn1_w_up, ffn1_w_down, norm_mix, w_in, fox_b_f, nsa_cmp_pe, nsa_cmp_w1,
           nsa_cmp_w2, swa_sinks, w_out, norm_ffn2, ffn2_w_gate, ffn2_w_up, ffn2_w_down, norm_final):
    bsz, s, d = x.shape
    depth = norm_ffn1.shape[0]
    h = x.reshape(bsz * s, d)
    for l in range(depth):
        h = _ffn(h, norm_ffn1[l], ffn1_w_gate[l].astype(BF16), ffn1_w_up[l].astype(BF16), ffn1_w_down[l].astype(BF16))
        h = _mixer(h.reshape(bsz, s, d), norm_mix[l], w_in[l], fox_b_f[l], nsa_cmp_pe[l], nsa_cmp_w1[l],
                   nsa_cmp_w2[l], swa_sinks[l], w_out[l]).reshape(bsz * s, d)
        h = _ffn(h, norm_ffn2[l], ffn2_w_gate[l].astype(BF16), ffn2_w_up[l].astype(BF16), ffn2_w_down[l].astype(BF16),
                 norm_final if l == depth - 1 else None)
    return h.reshape(bsz, s, d)
```

```python
import functools

import numpy as np
import jax
import jax.numpy as jnp
from jax import lax
from jax.experimental import pallas as pl
from jax.experimental.pallas import tpu as pltpu

F32 = jnp.float32
BF16 = jnp.bfloat16

HEAD_DIM = 64
LANES = 128
NSA_CMP_LEN = 32
NSA_CMP_STRIDE = 16
NSA_CMP_HIDDEN = 128
NSA_SEL_LEN = 64
NSA_TOPN = 16
NSA_WINDOW = 512
SWA_WINDOW = 128
DIL_PAIRS = ((128, 1), (512, 4), (2048, 16))
RMS_EPS = 1e-6
LOG2E = 1.4426950408889634
LN2 = 0.6931471805599453
MASKED = -1e30
VMEM_LIMIT = 56 * 1024 * 1024

IN_SPLITS = (
    ('fox_q', 256), ('fox_k', 256), ('fox_v', 256), ('fox_f', 4),
    ('nsa_q', 256), ('nsa_k_cmp', 64), ('nsa_v_cmp', 64), ('nsa_k_slc', 64), ('nsa_v_slc', 64),
    ('nsa_k_win', 64), ('nsa_v_win', 64), ('nsa_gate', 12),
    ('swa_q', 256), ('swa_k', 128), ('swa_v', 128),
    ('dil_q', 256), ('dil_k', 256), ('dil_v', 256),
)

U_NSA_Q = 0
U_FOX_Q, U_FOX_K, U_FOX_V = 4, 6, 8
U_SWA_Q, U_SWA_K, U_SWA_V = 10, 12, 14
U_SLC = 16
U_WIN = 18
U_CMP = 20
N_MAIN = 21
N_DIL = 6
N_UNITS = N_MAIN + N_DIL + 1
GATE_LANE0 = 4


def _alibi_slopes():
    n = 12
    s = 2.0 ** (-8.0 * np.arange(1, n + 1) / n)
    return s[:4], s[4:8], s[8:]


def _w_in_layout():
    off, o = {}, 0
    for name, w in IN_SPLITS:
        off[name] = o
        o += w
    cols, scale = [], []

    def seg(name, start, width, sc=1.0):
        cols.extend(range(off[name] + start, off[name] + start + width))
        scale.extend([sc] * width)

    def zeros(n):
        cols.extend([0] * n)
        scale.extend([0.0] * n)

    qs = HEAD_DIM ** -0.5 * LOG2E
    for h in range(4):
        seg('nsa_q', 64 * h, 64, qs)
        zeros(64)
    seg('fox_q', 0, 256, qs)
    seg('fox_k', 0, 256)
    seg('fox_v', 0, 256)
    seg('swa_q', 0, 256, qs)
    for name in ('swa_k', 'swa_v'):
        for kv in range(2):
            seg(name, 64 * kv, 64)
            seg(name, 64 * kv, 64)
    for a, b in (('nsa_k_slc', 'nsa_v_slc'), ('nsa_v_slc', 'nsa_k_slc'),
                 ('nsa_k_win', 'nsa_v_win'), ('nsa_v_win', 'nsa_k_win')):
        seg(a, 0, 64)
        seg(b, 0, 64)
    seg('nsa_k_cmp', 0, 64)
    seg('nsa_v_cmp', 0, 64)
    seg('dil_q', 0, 256, qs)
    seg('dil_k', 0, 256)
    seg('dil_v', 0, 256)
    seg('fox_f', 0, 4)
    seg('nsa_gate', 0, 12)
    zeros(LANES - 16)
    assert len(cols) == N_UNITS * LANES
    return np.asarray(cols, np.int32), np.asarray(scale, np.float32)


def _relayout_w_in(w_in, cols, scale):
    runs, start = [], 0
    for c in range(1, len(cols) + 1):
        same = c < len(cols) and scale[c] == scale[c - 1] and (scale[c] == 0.0 or cols[c] == cols[c - 1] + 1)
        if not same:
            runs.append((start, c))
            start = c
    parts = []
    for a, b in runs:
        if scale[a] == 0.0:
            parts.append(jnp.zeros((w_in.shape[0], b - a), BF16))
        else:
            parts.append((w_in[:, int(cols[a]):int(cols[a]) + (b - a)] * float(scale[a])).astype(BF16))
    return jnp.concatenate(parts, axis=1)


def _rms(x, g):
    ms = jnp.mean(x * x, axis=-1, keepdims=True)
    return x * lax.rsqrt(ms + RMS_EPS) * g


def _dot_nt(a, b):
    return lax.dot_general(a, b, (((1,), (1,)), ((), ())), preferred_element_type=F32)


def _params(sem, vmem=None):
    return pltpu.CompilerParams(dimension_semantics=sem, vmem_limit_bytes=vmem)


def _stack_packed_q(q):
    slot = lax.shift_right_logical(lax.broadcasted_iota(jnp.int32, q.shape, 1), 6)
    return jnp.concatenate([jnp.where(slot == h, q, jnp.zeros_like(q)) for h in range(4)], axis=0)


def _unstack_packed_o(o, t):
    slot = lax.shift_right_logical(lax.broadcasted_iota(jnp.int32, (t, 2 * LANES), 1), 6)
    out = o[:t]
    for h in range(1, 4):
        out = jnp.where(slot == h, o[h * t:(h + 1) * t], out)
    return out


def _stack_wide_q(q_ref):
    return jnp.concatenate([q_ref[:, h * LANES:(h + 1) * LANES] for h in range(4)], axis=0)


def _unstack_wide_o(o, t):
    lane = lax.broadcasted_iota(jnp.int32, (t, LANES), 1)
    pairs = [jnp.where(lane < HEAD_DIM, o[2 * pr * t:(2 * pr + 1) * t, LANES:], o[(2 * pr + 1) * t:(2 * pr + 2) * t, :LANES])
             for pr in range(2)]
    return jnp.concatenate(pairs, axis=1)


FLASH_ROW_BLOCK = 64


def _flash_consume(s_ref, p_ref, stats, v, t, diag):
    m_ref, alpha_ref, l_ref, acc_ref = stats
    n = s_ref.shape[1]
    rb = FLASH_ROW_BLOCK
    lanes = [slice(c * LANES, (c + 1) * LANES) for c in range(n // LANES)]

    def scores(r0):
        s = s_ref[r0:r0 + rb, :]
        if diag:
            row = lax.broadcasted_iota(jnp.int32, (rb, n), 0) + r0 % t
            col = lax.broadcasted_iota(jnp.int32, (rb, n), 1)
            s = jnp.where(col <= row, s, MASKED)
        return s

    for r0 in range(0, 4 * t, rb):
        s = scores(r0)
        mx = s[:, lanes[0]]
        for c in lanes[1:]:
            mx = jnp.maximum(mx, s[:, c])
        m_old = m_ref[r0:r0 + rb, :]
        m_new = jnp.maximum(m_old, jnp.max(mx, axis=-1, keepdims=True))
        alpha_ref[r0:r0 + rb, :] = jnp.exp2(m_old - m_new)
        m_ref[r0:r0 + rb, :] = m_new
    for r0 in range(0, 4 * t, rb):
        s = scores(r0)
        m_new = m_ref[r0:r0 + rb, :]
        psum = jnp.zeros((rb, LANES), F32)
        for c in lanes:
            p = jnp.exp2(s[:, c] - m_new)
            psum = psum + p
            p_ref[r0:r0 + rb, c] = p.astype(BF16)
        l_ref[r0:r0 + rb, :] = alpha_ref[r0:r0 + rb, :] * l_ref[r0:r0 + rb, :] + psum
    alpha = alpha_ref[...]
    acc_ref[...] = (jnp.concatenate([alpha, alpha], axis=1) * acc_ref[...]
                    + jnp.dot(p_ref[...], v, preferred_element_type=F32))


def _flash_causal(i, produce, consume, s_a, s_b, stats):
    m_ref, alpha_ref, l_ref, acc_ref = stats
    m_ref[...] = jnp.full(m_ref.shape, MASKED, F32)
    l_ref[...] = jnp.zeros(l_ref.shape, F32)
    acc_ref[...] = jnp.zeros(acc_ref.shape, F32)
    produce(s_a, 0)

    def pair(jj, carry):
        j = 2 * jj
        produce(s_b, j + 1)
        consume(s_a, j, False)
        produce(s_a, j + 2)
        consume(s_b, j + 1, False)
        return carry

    lax.fori_loop(0, i // 2, pair, 0)

    @pl.when(i % 2 == 1)
    def _():
        consume(s_a, i - 1, False)
        produce(s_b, i)
        consume(s_b, i, True)

    @pl.when(i % 2 == 0)
    def _():
        consume(s_a, i, True)

    return acc_ref[...] * (1.0 / jnp.sum(l_ref[...], axis=-1, keepdims=True))


def _flash_scratch(t):
    return [pltpu.VMEM((4 * t, t), F32), pltpu.VMEM((4 * t, t), F32), pltpu.VMEM((4 * t, t), BF16),
            pltpu.VMEM((4 * t, LANES), F32), pltpu.VMEM((4 * t, LANES), F32), pltpu.VMEM((4 * t, LANES), F32),
            pltpu.VMEM((4 * t, 2 * LANES), F32)]


def _ffn_body(x_ref, g_ref, wg_ref, wu_ref, wd_ref, *rest, final):
    o_ref = rest[-1]
    x = x_ref[...]
    xn = _rms(x, g_ref[...]).astype(BF16)
    a = jnp.dot(xn, wg_ref[...], preferred_element_type=F32)
    b = jnp.dot(xn, wu_ref[...], preferred_element_type=F32)
    h = (a * jax.nn.sigmoid(a) * b).astype(BF16)
    y = x + 0.5 * jnp.dot(h, wd_ref[...], preferred_element_type=F32)
    if final:
        y = _rms(y, rest[0][...])
    o_ref[...] = y


def _ffn(x, g, wg, wu, wd, g_final=None):
    n, d = x.shape
    f = wg.shape[1]
    tm = 512 if n % 512 == 0 else n
    const = lambda i: (0, 0)
    once = pl.Buffered(1)
    in_specs = [
        pl.BlockSpec((tm, d), lambda i: (i, 0)),
        pl.BlockSpec((1, d), const),
        pl.BlockSpec((d, f), const, pipeline_mode=once),
        pl.BlockSpec((d, f), const, pipeline_mode=once),
        pl.BlockSpec((f, d), const, pipeline_mode=once),
    ]
    args = [x, g.reshape(1, d), wg, wu, wd]
    if g_final is not None:
        in_specs.append(pl.BlockSpec((1, d), const))
        args.append(g_final.reshape(1, d))
    return pl.pallas_call(
        functools.partial(_ffn_body, final=g_final is not None),
        out_shape=jax.ShapeDtypeStruct((n, d), F32),
        grid=(n // tm,),
        in_specs=in_specs,
        out_specs=pl.BlockSpec((tm, d), lambda i: (i, 0)),
        compiler_params=_params(("parallel",), VMEM_LIMIT),
        name="ffn",
    )(*args)


def _inproj_body(x_ref, g_ref, w_ref, zm_ref, zs_ref, *rest):
    zd_refs, zscr = rest[:-1], rest[-1]
    tm = x_ref.shape[0]
    xn = _rms(x_ref[...], g_ref[...]).astype(BF16)
    z = jnp.dot(xn, w_ref[...], preferred_element_type=F32)
    zm_ref[...] = z[:, :N_MAIN * LANES].astype(BF16)
    zs_ref[...] = z[:, (N_MAIN + N_DIL) * LANES:]
    for u in range(N_DIL):
        zscr[u] = z[:, (N_MAIN + u) * LANES:(N_MAIN + u + 1) * LANES]
    for (_, dd), ref in zip(DIL_PAIRS, zd_refs):
        for u in range(N_DIL):
            if dd == 1:
                ref[0, :, u * LANES:(u + 1) * LANES] = zscr[u].astype(BF16)
            else:
                for r in range(dd):
                    ref[r, :, u * LANES:(u + 1) * LANES] = zscr[u, pl.ds(r, tm // dd, stride=dd), :].astype(BF16)


def _inproj(x, g, w):
    bsz, s, d = x.shape
    tm = min(512, s)
    const = lambda b, i: (0, 0)
    row = lambda b, i: (b, i, 0)
    wd = N_DIL * LANES
    zd_shapes = tuple(jax.ShapeDtypeStruct((bsz, dd, s // dd, wd), BF16) for _, dd in DIL_PAIRS)
    zd_specs = tuple(pl.BlockSpec((None, dd, tm // dd, wd), lambda b, i: (b, 0, i, 0)) for _, dd in DIL_PAIRS)
    return pl.pallas_call(
        _inproj_body,
        out_shape=(jax.ShapeDtypeStruct((bsz, s, N_MAIN * LANES), BF16), jax.ShapeDtypeStruct((bsz, s, LANES), F32))
        + zd_shapes,
        grid=(bsz, s // tm),
        in_specs=[pl.BlockSpec((None, tm, d), row), pl.BlockSpec((1, d), const),
                  pl.BlockSpec((d, N_UNITS * LANES), const, pipeline_mode=pl.Buffered(1))],
        out_specs=(pl.BlockSpec((None, tm, N_MAIN * LANES), row), pl.BlockSpec((None, tm, LANES), row)) + zd_specs,
        scratch_shapes=[pltpu.VMEM((N_DIL, tm, LANES), F32)],
        compiler_params=_params(("parallel", "parallel"), VMEM_LIMIT),
        name="inproj",
    )(x, g.reshape(1, d), w)


def _cumsum_body(x_ref, b_ref, u_ref, c_ref, *, tk):
    x = x_ref[...] + b_ref[...]
    lf = (jnp.minimum(x, 0.0) - jnp.log(1.0 + jnp.exp(-jnp.abs(x)))) * LOG2E
    u = u_ref[...]
    carry = jnp.zeros((8, 1), F32)
    per = tk // LANES
    for j in range(x.shape[1] // LANES):
        seg = lf[:, j * LANES:(j + 1) * LANES]
        cs = jnp.dot(seg, u, precision=lax.Precision.HIGHEST, preferred_element_type=F32) + carry
        c_ref[j // per, :, (j % per) * LANES:(j % per + 1) * LANES] = cs
        carry = cs[:, LANES - 1:LANES]


def _fox_cumsum(f_rows, b_rows, tk):
    bsz, _, _, s = f_rows.shape
    u = jnp.asarray(np.triu(np.ones((LANES, LANES), np.float32)))
    return pl.pallas_call(
        functools.partial(_cumsum_body, tk=tk),
        out_shape=jax.ShapeDtypeStruct((bsz, 2, s // tk, 8, tk), F32),
        grid=(bsz, 2),
        in_specs=[pl.BlockSpec((None, None, 8, s), lambda b, p: (b, p, 0, 0)),
                  pl.BlockSpec((None, 8, 1), lambda b, p: (p, 0, 0)),
                  pl.BlockSpec((LANES, LANES), lambda b, p: (0, 0))],
        out_specs=pl.BlockSpec((None, None, s // tk, 8, tk), lambda b, p: (b, p, 0, 0, 0)),
        compiler_params=_params(("parallel", "parallel")),
        name="fox_cumsum",
    )(f_rows, b_rows, u)


def _fox_body(q_ref, k_ref, v_ref, c_ref, o_ref, qs_ref, s_a, s_b, p_ref, *stats, t):
    i = pl.program_id(1)
    qs_ref[...] = _stack_packed_q(q_ref[...])

    def rows(j):
        return pl.ds(pl.multiple_of(j * t, t), t)

    def produce(s_ref, j):
        s = _dot_nt(qs_ref[...], k_ref[rows(j), :])
        for h in range(4):
            s_ref[h * t:(h + 1) * t, :] = s[h * t:(h + 1) * t] - c_ref[h // 2, j, (h % 2):(h % 2) + 1, :]

    def consume(s_ref, j, diag):
        _flash_consume(s_ref, p_ref, stats, v_ref[rows(j), :], t, diag)

    o = _flash_causal(i, produce, consume, s_a, s_b, stats)
    o_ref[...] = _unstack_packed_o(o, t).astype(o_ref.dtype)


def _fox(zm, c, t):
    bsz, s, _ = zm.shape
    return pl.pallas_call(
        functools.partial(_fox_body, t=t),
        out_shape=jax.ShapeDtypeStruct((bsz, s, 2 * LANES), BF16),
        grid=(bsz, s // t),
        in_specs=[pl.BlockSpec((None, t, 2 * LANES), lambda b, i: (b, i, U_FOX_Q // 2)),
                  pl.BlockSpec((None, s, 2 * LANES), lambda b, i: (b, 0, U_FOX_K // 2)),
                  pl.BlockSpec((None, s, 2 * LANES), lambda b, i: (b, 0, U_FOX_V // 2)),
                  pl.BlockSpec((None, 2, s // t, 8, t), lambda b, i: (b, 0, 0, 0, 0))],
        out_specs=pl.BlockSpec((None, t, 2 * LANES), lambda b, i: (b, i, 0)),
        scratch_shapes=[pltpu.VMEM((4 * t, 2 * LANES), BF16)] + _flash_scratch(t),
        compiler_params=_params(("parallel", "arbitrary"), VMEM_LIMIT),
        name="fox_attn",
    )(zm, zm, zm, c)


def _banded_body(*refs, gb, tq, sq, span, s_loc, wpad, max_dist, slopes, wide, with_lse):
    i = pl.program_id(1)
    o_ref = refs[2] if wide else refs[3]
    for g in range(gb):
        for r in range(tq // sq):
            t0 = i * tq + r * sq
            rows = slice(r * sq, (r + 1) * sq)
            start = pl.multiple_of(jnp.clip(t0 - wpad, 0, s_loc - span), LANES)
            kpos = start + lax.broadcasted_iota(jnp.int32, (1, span), 1)
            tpos = t0 + lax.broadcasted_iota(jnp.int32, (sq, 1), 0)
            dist = tpos - kpos
            mask = lax.bitcast_convert_type(dist, jnp.uint32) <= jnp.uint32(max_dist)
            krel = (kpos - t0).astype(F32)
            trel = (tpos - t0).astype(F32)
            if wide:
                q_ref, kv_ref = refs[:2]
                qs = jnp.concatenate([q_ref[g, rows, h * LANES:(h + 1) * LANES] for h in range(4)], axis=0)
                k = kv_ref[g, pl.ds(start, span), :LANES]
                v = kv_ref[g, pl.ds(start, span), :]
            else:
                q_ref, k_ref, v_ref = refs[:3]
                qs = _stack_packed_q(q_ref[g, rows, :])
                k = k_ref[g, pl.ds(start, span), :]
                v = v_ref[g, pl.ds(start, span), :]
            s = _dot_nt(qs, k)
            s = jnp.concatenate([jnp.where(mask, s[h * sq:(h + 1) * sq] + (float(slopes[h]) * LOG2E) * krel, MASKED)
                                 for h in range(4)], axis=0)
            m = jnp.max(s, axis=-1, keepdims=True)
            e = jnp.exp2(s - m)
            l = jnp.sum(e, axis=-1, keepdims=True)
            o = jnp.dot(e.astype(BF16), v, preferred_element_type=F32) * (1.0 / l)
            o_ref[g, rows, :] = (_unstack_wide_o(o, sq) if wide else _unstack_packed_o(o, sq)).astype(o_ref.dtype)
            if with_lse:
                lse = (m + jnp.log2(l)) * LN2
                lane = lax.broadcasted_iota(jnp.int32, (sq, LANES), 1)
                acc = jnp.zeros((sq, LANES), F32)
                for h in range(4):
                    acc = jnp.where(lane == h, lse[h * sq:(h + 1) * sq] - float(slopes[h]) * trel, acc)
                refs[-1][g, rows, :] = acc


BANDED_ROWS = 1024


def _banded(arrs, units, slopes, max_dist, wpad, wide, with_lse, name):
    g, s_loc, _ = arrs[0].shape
    tq = min(BANDED_ROWS, s_loc)
    gb = min(g, BANDED_ROWS // tq)
    sq = min(tq, wpad, 256)
    span = min(sq + wpad, s_loc)
    if wide:
        in_specs = [pl.BlockSpec((gb, tq, 4 * LANES), lambda b, i: (b, i, units[0])),
                    pl.BlockSpec((gb, s_loc, 2 * LANES), lambda b, i: (b, 0, units[1]))]
    else:
        in_specs = [pl.BlockSpec((gb, tq, 2 * LANES), lambda b, i: (b, i, units[0])),
                    pl.BlockSpec((gb, s_loc, 2 * LANES), lambda b, i: (b, 0, units[1])),
                    pl.BlockSpec((gb, s_loc, 2 * LANES), lambda b, i: (b, 0, units[2]))]
    out_shape = [jax.ShapeDtypeStruct((g, s_loc, 2 * LANES), BF16)]
    out_specs = [pl.BlockSpec((gb, tq, 2 * LANES), lambda b, i: (b, i, 0))]
    if with_lse:
        out_shape.append(jax.ShapeDtypeStruct((g, s_loc, LANES), F32))
        out_specs.append(pl.BlockSpec((gb, tq, LANES), lambda b, i: (b, i, 0)))
    body = functools.partial(_banded_body, gb=gb, tq=tq, sq=sq, span=span, s_loc=s_loc, wpad=wpad,
                             max_dist=max_dist, slopes=tuple(float(x) for x in slopes), wide=wide, with_lse=with_lse)
    return pl.pallas_call(
        body, out_shape=tuple(out_shape), grid=(g // gb, s_loc // tq), in_specs=in_specs, out_specs=tuple(out_specs),
        compiler_params=_params(("parallel", "arbitrary")), name=name,
    )(*arrs)


def _compress_body(k_ref, v_ref, pe_ref, w1_ref, w2_ref, o_ref):
    n = k_ref.shape[0]
    hid = []
    for idx, ref in enumerate((k_ref, v_ref)):
        ch = ref[...].astype(F32)
        a = jnp.dot((ch + pe_ref[idx, 0]).astype(BF16), w1_ref[idx, 0], preferred_element_type=F32)
        b = jnp.dot((ch + pe_ref[idx, 1]).astype(BF16), w1_ref[idx, 1], preferred_element_type=F32)
        pre = a + pltpu.roll(b, n - 1, 0)
        hid.append((pre * jax.nn.sigmoid(pre)).astype(BF16))
    o_ref[...] = (jnp.dot(hid[0], w2_ref[0], preferred_element_type=F32)
                  + jnp.dot(hid[1], w2_ref[1], preferred_element_type=F32)).astype(o_ref.dtype)


def _compress(kch, vch, pe, w1, w2):
    bsz, n, w = kch.shape
    full = lambda *shape: pl.BlockSpec(shape, lambda b: (0,) * len(shape))
    blk = pl.BlockSpec((None, n, w), lambda b: (b, 0, 0))
    return pl.pallas_call(
        _compress_body,
        out_shape=jax.ShapeDtypeStruct((bsz, n, 2 * LANES), BF16),
        grid=(bsz,),
        in_specs=[blk, blk, full(2, 2, 1, w), full(2, 2, w, NSA_CMP_HIDDEN), full(2, NSA_CMP_HIDDEN, 2 * LANES)],
        out_specs=pl.BlockSpec((None, n, 2 * LANES), lambda b: (b, 0, 0)),
        compiler_params=_params(("parallel",)),
        name="nsa_compress",
    )(kch, vch, pe, w1, w2)


def _cmp_select_body(q_ref, kv_ref, ov_ref, o_ref, sb_ref, *, tq, slopes, n_top):
    i = pl.program_id(1)
    t0 = i * tq
    n_cmp = kv_ref.shape[0]
    tpos = t0 + lax.broadcasted_iota(jnp.int32, (tq, 1), 0)
    cend = NSA_CMP_STRIDE * lax.broadcasted_iota(jnp.int32, (1, n_cmp), 1) + (NSA_CMP_LEN - 1)
    dist = tpos - cend
    mask = dist >= 0
    distf = dist.astype(F32)
    kv = kv_ref[...]
    s_all = _dot_nt(_stack_wide_q(q_ref), kv[:, :LANES])
    psum = jnp.zeros((tq, n_cmp), F32)
    ps = []
    for h in range(4):
        s = jnp.where(mask, s_all[h * tq:(h + 1) * tq] - (float(slopes[h]) * LOG2E) * distf, -jnp.inf)
        m = jnp.max(s, axis=-1, keepdims=True)
        m = jnp.where(m == -jnp.inf, 0.0, m)
        e = jnp.where(mask, jnp.exp2(s - m), 0.0)
        p = e / jnp.maximum(jnp.sum(e, axis=-1, keepdims=True), 1e-30)
        ps.append(p.astype(BF16))
        psum = psum + p
    o = jnp.dot(jnp.concatenate(ps, axis=0), kv, preferred_element_type=F32)
    o_ref[...] = _unstack_wide_o(o, tq).astype(o_ref.dtype)

    imp = lax.dot_general(ov_ref[...], psum, (((1,), (1,)), ((), ())), precision=lax.Precision.HIGHEST,
                          preferred_element_type=F32)
    j = lax.broadcasted_iota(jnp.int32, (HEAD_DIM, tq), 0)
    tl = t0 + lax.broadcasted_iota(jnp.int32, (HEAD_DIM, tq), 1)
    cur = lax.shift_right_logical(tl, 6)
    causal = j * NSA_SEL_LEN <= tl
    forced = (j == 0) | (j == cur) | (j == cur - 1)
    score = jnp.where(causal, jnp.where(forced, jnp.inf, imp), -jnp.inf)
    rank = jnp.zeros((HEAD_DIM, tq), F32)
    for jp in range(HEAD_DIM):
        other = score[jp:jp + 1, :]
        rank = rank + jnp.where(other > score, 1.0, jnp.where((other == score) & (j > jp), 1.0, 0.0))
    bias_t = jnp.where(causal & (rank < float(n_top)), 0.0, MASKED)
    sb_ref[...] = jnp.concatenate([jnp.zeros((HEAD_DIM, tq), F32), bias_t], axis=0).T.astype(sb_ref.dtype)


def _cmp_select(zm, kv_c, overlap_t, tq, slopes, n_top):
    bsz, s, _ = zm.shape
    n_cmp = kv_c.shape[1]
    return pl.pallas_call(
        functools.partial(_cmp_select_body, tq=tq, slopes=tuple(float(x) for x in slopes), n_top=n_top),
        out_shape=(jax.ShapeDtypeStruct((bsz, s, 2 * LANES), BF16), jax.ShapeDtypeStruct((bsz, s, LANES), BF16)),
        grid=(bsz, s // tq),
        in_specs=[pl.BlockSpec((None, tq, 4 * LANES), lambda b, i: (b, i, U_NSA_Q // 4)),
                  pl.BlockSpec((None, n_cmp, 2 * LANES), lambda b, i: (b, 0, 0)),
                  pl.BlockSpec((HEAD_DIM, n_cmp), lambda b, i: (0, 0))],
        out_specs=(pl.BlockSpec((None, tq, 2 * LANES), lambda b, i: (b, i, 0)),
                   pl.BlockSpec((None, tq, LANES), lambda b, i: (b, i, 0))),
        compiler_params=_params(("parallel", "parallel")),
        name="nsa_cmp_select",
    )(zm, kv_c, overlap_t)


def _sel_body(q_ref, sb_ref, kv_ref, o_ref, kp_ref, qs_ref, s_a, s_b, p_ref, *stats, t, slopes):
    i = pl.program_id(1)
    s_len = kv_ref.shape[0]

    @pl.when(i == 0)
    def _():
        pos = lax.broadcasted_iota(jnp.int32, (s_len, LANES), 0)
        ln = lax.broadcasted_iota(jnp.int32, (s_len, LANES), 1)
        onehot = jnp.where(lax.shift_right_logical(pos, 6) == ln - HEAD_DIM, 1.0, 0.0).astype(BF16)
        kp_ref[...] = jnp.where(ln < HEAD_DIM, kv_ref[:, :LANES], onehot)

    lane = lax.broadcasted_iota(jnp.int32, (t, LANES), 1)
    sb = sb_ref[...]
    for h in range(4):
        qs_ref[h * t:(h + 1) * t, :] = jnp.where(lane < HEAD_DIM, q_ref[:, h * LANES:(h + 1) * LANES], sb)
    krel0 = lax.broadcasted_iota(jnp.int32, (1, t), 1)

    def rows(j):
        return pl.ds(pl.multiple_of(j * t, t), t)

    def produce(s_ref, j):
        s = _dot_nt(qs_ref[...], kp_ref[rows(j), :])
        krel = (krel0 + (j - i) * t).astype(F32)
        for h in range(4):
            s_ref[h * t:(h + 1) * t, :] = s[h * t:(h + 1) * t] + (float(slopes[h]) * LOG2E) * krel

    def consume(s_ref, j, diag):
        _flash_consume(s_ref, p_ref, stats, kv_ref[rows(j), :], t, diag)

    o = _flash_causal(i, produce, consume, s_a, s_b, stats)
    o_ref[...] = _unstack_wide_o(o, t).astype(o_ref.dtype)


def _sel(zm, sbias, t, slopes):
    bsz, s, _ = zm.shape
    return pl.pallas_call(
        functools.partial(_sel_body, t=t, slopes=tuple(float(x) for x in slopes)),
        out_shape=jax.ShapeDtypeStruct((bsz, s, 2 * LANES), BF16),
        grid=(bsz, s // t),
        in_specs=[pl.BlockSpec((None, t, 4 * LANES), lambda b, i: (b, i, U_NSA_Q // 4)),
                  pl.BlockSpec((None, t, LANES), lambda b, i: (b, i, 0)),
                  pl.BlockSpec((None, s, 2 * LANES), lambda b, i: (b, 0, U_SLC // 2))],
        out_specs=pl.BlockSpec((None, t, 2 * LANES), lambda b, i: (b, i, 0)),
        scratch_shapes=[pltpu.VMEM((s, LANES), BF16), pltpu.VMEM((4 * t, LANES), BF16)] + _flash_scratch(t),
        compiler_params=_params(("parallel", "arbitrary"), VMEM_LIMIT),
        name="nsa_sel_attn",
    )(zm, sbias, zm)


def _spread_heads(vals, lane0, stride, tm):
    slot = lax.shift_right_logical(lax.broadcasted_iota(jnp.int32, (tm, 2 * LANES), 1), 6)
    out = jnp.broadcast_to(vals[:, lane0:lane0 + 1], (tm, 2 * LANES))
    for h in range(1, 4):
        c = lane0 + stride * h
        out = jnp.where(slot == h, vals[:, c:c + 1], out)
    return out


def _outproj_body(x_ref, ofox_ref, ocmp_ref, oslc_ref, owin_ref, zs_ref, oswa_ref, lswa_ref, sink_ref,
                  od1_ref, od4_ref, od16_ref, l1_ref, l4_ref, l16_ref, w_ref, o_ref, o_scr, l_scr):
    tm = x_ref.shape[0]

    def natural(ref, scr):
        dd, _, w = ref.shape
        if dd == 1:
            return ref[0].astype(F32)
        for u in range(w // LANES):
            for r in range(dd):
                scr[u, pl.ds(r, tm // dd, stride=dd), :] = ref[r, :, u * LANES:(u + 1) * LANES].astype(F32)
        return jnp.concatenate([scr[u] for u in range(w // LANES)], axis=1)

    gate = jax.nn.sigmoid(zs_ref[...])
    o_nsa = (_spread_heads(gate, GATE_LANE0, 3, tm) * ocmp_ref[...].astype(F32)
             + _spread_heads(gate, GATE_LANE0 + 1, 3, tm) * oslc_ref[...].astype(F32)
             + _spread_heads(gate, GATE_LANE0 + 2, 3, tm) * owin_ref[...].astype(F32))
    keep = jax.nn.sigmoid(lswa_ref[...] - sink_ref[...])
    o_swa = _spread_heads(keep, 0, 1, tm) * oswa_ref[...].astype(F32)
    l1, l4, l16 = natural(l1_ref, l_scr), natural(l4_ref, l_scr), natural(l16_ref, l_scr)
    m = jnp.maximum(jnp.maximum(l1, l4), l16)
    e1, e4, e16 = jnp.exp(l1 - m), jnp.exp(l4 - m), jnp.exp(l16 - m)
    inv = 1.0 / (e1 + e4 + e16)
    o_dil = _spread_heads(e1 * inv, 0, 1, tm) * natural(od1_ref, o_scr)
    o_dil = o_dil + _spread_heads(e4 * inv, 0, 1, tm) * natural(od4_ref, o_scr)
    o_dil = o_dil + _spread_heads(e16 * inv, 0, 1, tm) * natural(od16_ref, o_scr)
    y = x_ref[...]
    for g, o in enumerate((ofox_ref[...], o_nsa.astype(BF16), o_swa.astype(BF16), o_dil.astype(BF16))):
        y = y + jnp.dot(o, w_ref[g], preferred_element_type=F32)
    o_ref[...] = y


def _outproj(x, heads, zs, lses, sinks, w):
    bsz, s, d = x.shape
    tm = min(512, s)
    row = lambda b, i: (b, i, 0)
    wide = pl.BlockSpec((None, tm, 2 * LANES), row)
    narrow = pl.BlockSpec((None, tm, LANES), row)

    def strided(a):
        dd = a.shape[1]
        return pl.BlockSpec((None, dd, tm // dd, a.shape[-1]), lambda b, i: (b, 0, i, 0))

    ofox, ocmp, oslc, owin, oswa, od1, od4, od16 = heads
    lswa, l1, l4, l16 = lses
    return pl.pallas_call(
        _outproj_body,
        out_shape=jax.ShapeDtypeStruct((bsz, s, d), F32),
        grid=(bsz, s // tm),
        in_specs=[pl.BlockSpec((None, tm, d), row), wide, wide, wide, wide, narrow, wide, narrow,
                  pl.BlockSpec((1, LANES), lambda b, i: (0, 0)), strided(od1), strided(od4), strided(od16),
                  strided(l1), strided(l4), strided(l16),
                  pl.BlockSpec((4, 2 * LANES, d), lambda b, i: (0, 0, 0))],
        out_specs=pl.BlockSpec((None, tm, d), row),
        scratch_shapes=[pltpu.VMEM((2, tm, LANES), F32), pltpu.VMEM((1, tm, LANES), F32)],
        compiler_params=_params(("parallel", "parallel"), VMEM_LIMIT),
        name="outproj",
    )(x, ofox, ocmp, oslc, owin, zs, oswa, lswa, sinks, od1, od4, od16, l1, l4, l16, w)


def _overlap_t(n_cmp, n_sel):
    cs = np.arange(n_cmp) * NSA_CMP_STRIDE
    ss = np.arange(n_sel) * NSA_SEL_LEN
    ov = (cs[None, :] <= ss[:, None] + NSA_SEL_LEN - 1) & (cs[None, :] + NSA_CMP_LEN - 1 >= ss[:, None])
    out = np.zeros((HEAD_DIM, n_cmp), np.float32)
    out[:n_sel] = ov
    return jnp.asarray(out)


def _mixer(x, g_mix, w_in, fox_b_f, cmp_pe, cmp_w1, cmp_w2, swa_sinks, w_out):
    bsz, s, d = x.shape
    sl_swa, sl_nsa, sl_dil = _alibi_slopes()
    cols, scale = _w_in_layout()
    w_perm = _relayout_w_in(w_in, cols, scale)
    zm, zs, *zds = _inproj(x, g_mix, w_perm)

    t_fox = min(512, s)
    f_rows = jnp.transpose(zs[:, :, :4], (0, 2, 1)).reshape(bsz, 2, 2, s)
    f_rows = jnp.pad(f_rows, ((0, 0), (0, 0), (0, 6), (0, 0)))
    b_rows = jnp.pad(fox_b_f.reshape(2, 2, 1), ((0, 0), (0, 6), (0, 0)))
    c = _fox_cumsum(f_rows, b_rows, t_fox)
    o_fox = _fox(zm, c, t_fox)

    n_chunk = s // NSA_CMP_STRIDE
    n_sel = s // NSA_SEL_LEN
    assert n_sel <= HEAD_DIM
    kvc = zm[:, :, U_CMP * LANES:(U_CMP + 1) * LANES]
    kch = kvc[:, :, :HEAD_DIM].reshape(bsz, n_chunk, NSA_CMP_STRIDE * HEAD_DIM)
    vch = kvc[:, :, HEAD_DIM:].reshape(bsz, n_chunk, NSA_CMP_STRIDE * HEAD_DIM)
    pe = cmp_pe.reshape(2, 2, 1, NSA_CMP_STRIDE * HEAD_DIM)
    w1 = cmp_w1.reshape(2, 2, NSA_CMP_STRIDE * HEAD_DIM, NSA_CMP_HIDDEN).astype(BF16)
    zpad = jnp.zeros((NSA_CMP_HIDDEN, HEAD_DIM), F32)
    w2 = jnp.stack([jnp.concatenate([cmp_w2[0], zpad, zpad, cmp_w2[0]], axis=1),
                    jnp.concatenate([zpad, cmp_w2[1], cmp_w2[1], zpad], axis=1)]).astype(BF16)
    kv_c = _compress(kch, vch, pe, w1, w2)
    o_cmp, sbias = _cmp_select(zm, kv_c, _overlap_t(n_chunk, n_sel), min(512, s), sl_nsa, min(NSA_TOPN, n_sel))
    o_slc = _sel(zm, sbias, min(512, s), sl_nsa)
    (o_win,) = _banded((zm, zm), (U_NSA_Q // 4, U_WIN // 2), sl_nsa, NSA_WINDOW - 1, NSA_WINDOW,
                       True, False, "nsa_win_attn")

    o_swa, l_swa = _banded((zm, zm, zm), (U_SWA_Q // 2, U_SWA_K // 2, U_SWA_V // 2), sl_swa, SWA_WINDOW - 1,
                           SWA_WINDOW, False, True, "swa_attn")

    o_dil, l_dil = [], []
    for (window, dd), zd in zip(DIL_PAIRS, zds):
        zz = zd.reshape(bsz * dd, s // dd, N_DIL * LANES)
        o, l = _banded((zz, zz, zz), (0, 1, 2), sl_dil * dd, window // dd, LANES, False, True, "dil%d_attn" % dd)
        o_dil.append(o.reshape(bsz, dd, s // dd, 2 * LANES))
        l_dil.append(l.reshape(bsz, dd, s // dd, LANES))

    sinks = jnp.pad(swa_sinks.reshape(1, 4), ((0, 0), (0, LANES - 4)))
    heads = (o_fox, o_cmp, o_slc, o_win, o_swa, *o_dil)
    return _outproj(x, heads, zs, (l_swa, *l_dil), sinks, w_out.reshape(4, 2 * LANES, d).astype(BF16))


def kernel(x, norm_ffn1, ffn1_w_gate, ffn1_w_up, ffn1_w_down, norm_mix, w_in, fox_b_f, nsa_cmp_pe, nsa_cmp_w1,
           nsa_cmp_w2, swa_sinks, w_out, norm_ffn2, ffn2_w_gate, ffn2_w_up, ffn2_w_down, norm_final):
    bsz, s, d = x.shape
    depth = norm_ffn1.shape[0]
    h = x.reshape(bsz * s, d)
    for l in range(depth):
        h = _ffn(h, norm_ffn1[l], ffn1_w_gate[l].astype(BF16), ffn1_w_up[l].astype(BF16), ffn1_w_down[l].astype(BF16))
        h = _mixer(h.reshape(bsz, s, d), norm_mix[l], w_in[l], fox_b_f[l], nsa_cmp_pe[l], nsa_cmp_w1[l],
                   nsa_cmp_w2[l], swa_sinks[l], w_out[l]).reshape(bsz * s, d)
        h = _ffn(h, norm_ffn2[l], ffn2_w_gate[l].astype(BF16), ffn2_w_up[l].astype(BF16), ffn2_w_down[l].astype(BF16),
                 norm_final if l == depth - 1 else None)
    return h.reshape(bsz, s, d)
```

```python
import functools

import numpy as np
import jax
import jax.numpy as jnp
from jax import lax
from jax.experimental import pallas as pl
from jax.experimental.pallas import tpu as pltpu

F32 = jnp.float32
BF16 = jnp.bfloat16

HEAD_DIM = 64
LANES = 128
NSA_CMP_LEN = 32
NSA_CMP_STRIDE = 16
NSA_CMP_HIDDEN = 128
NSA_SEL_LEN = 64
NSA_TOPN = 16
NSA_WINDOW = 512
SWA_WINDOW = 128
DIL_PAIRS = ((128, 1), (512, 4), (2048, 16))
RMS_EPS = 1e-6
LOG2E = 1.4426950408889634
LN2 = 0.6931471805599453
MASKED = -1e30
VMEM_LIMIT = 56 * 1024 * 1024

IN_SPLITS = (
    ('fox_q', 256), ('fox_k', 256), ('fox_v', 256), ('fox_f', 4),
    ('nsa_q', 256), ('nsa_k_cmp', 64), ('nsa_v_cmp', 64), ('nsa_k_slc', 64), ('nsa_v_slc', 64),
    ('nsa_k_win', 64), ('nsa_v_win', 64), ('nsa_gate', 12),
    ('swa_q', 256), ('swa_k', 128), ('swa_v', 128),
    ('dil_q', 256), ('dil_k', 256), ('dil_v', 256),
)

U_NSA_Q = 0
U_FOX_Q, U_FOX_K, U_FOX_V = 4, 6, 8
U_SWA_Q, U_SWA_K, U_SWA_V = 10, 12, 14
U_SLC = 16
U_WIN = 18
U_CMP = 20
N_MAIN = 21
N_DIL = 6
N_UNITS = N_MAIN + N_DIL + 1
GATE_LANE0 = 4


def _alibi_slopes():
    n = 12
    s = 2.0 ** (-8.0 * np.arange(1, n + 1) / n)
    return s[:4], s[4:8], s[8:]


def _w_in_layout():
    off, o = {}, 0
    for name, w in IN_SPLITS:
        off[name] = o
        o += w
    cols, scale = [], []

    def seg(name, start, width, sc=1.0):
        cols.extend(range(off[name] + start, off[name] + start + width))
        scale.extend([sc] * width)

    def zeros(n):
        cols.extend([0] * n)
        scale.extend([0.0] * n)

    qs = HEAD_DIM ** -0.5 * LOG2E
    for h in range(4):
        seg('nsa_q', 64 * h, 64, qs)
        zeros(64)
    seg('fox_q', 0, 256, qs)
    seg('fox_k', 0, 256)
    seg('fox_v', 0, 256)
    seg('swa_q', 0, 256, qs)
    for name in ('swa_k', 'swa_v'):
        for kv in range(2):
            seg(name, 64 * kv, 64)
            seg(name, 64 * kv, 64)
    for a, b in (('nsa_k_slc', 'nsa_v_slc'), ('nsa_v_slc', 'nsa_k_slc'),
                 ('nsa_k_win', 'nsa_v_win'), ('nsa_v_win', 'nsa_k_win')):
        seg(a, 0, 64)
        seg(b, 0, 64)
    seg('nsa_k_cmp', 0, 64)
    seg('nsa_v_cmp', 0, 64)
    seg('dil_q', 0, 256, qs)
    seg('dil_k', 0, 256)
    seg('dil_v', 0, 256)
    seg('fox_f', 0, 4)
    seg('nsa_gate', 0, 12)
    zeros(LANES - 16)
    assert len(cols) == N_UNITS * LANES
    return np.asarray(cols, np.int32), np.asarray(scale, np.float32)


def _relayout_w_in(w_in, cols, scale):
    runs, start = [], 0
    for c in range(1, len(cols) + 1):
        same = c < len(cols) and scale[c] == scale[c - 1] and (scale[c] == 0.0 or cols[c] == cols[c - 1] + 1)
        if not same:
            runs.append((start, c))
            start = c
    parts = []
    for a, b in runs:
        if scale[a] == 0.0:
            parts.append(jnp.zeros((w_in.shape[0], b - a), BF16))
        else:
            parts.append((w_in[:, int(cols[a]):int(cols[a]) + (b - a)] * float(scale[a])).astype(BF16))
    return jnp.concatenate(parts, axis=1)


def _rms(x, g):
    ms = jnp.mean(x * x, axis=-1, keepdims=True)
    return x * lax.rsqrt(ms + RMS_EPS) * g


def _dot_nt(a, b):
    return lax.dot_general(a, b, (((1,), (1,)), ((), ())), preferred_element_type=F32)


def _params(sem, vmem=None):
    return pltpu.CompilerParams(dimension_semantics=sem, vmem_limit_bytes=vmem)


def _stack_packed_q(q):
    slot = lax.shift_right_logical(lax.broadcasted_iota(jnp.int32, q.shape, 1), 6)
    return jnp.concatenate([jnp.where(slot == h, q, jnp.zeros_like(q)) for h in range(4)], axis=0)


def _unstack_packed_o(o, t):
    slot = lax.shift_right_logical(lax.broadcasted_iota(jnp.int32, (t, 2 * LANES), 1), 6)
    out = o[:t]
    for h in range(1, 4):
        out = jnp.where(slot == h, o[h * t:(h + 1) * t], out)
    return out


def _stack_wide_q(q_ref):
    return jnp.concatenate([q_ref[:, h * LANES:(h + 1) * LANES] for h in range(4)], axis=0)


def _unstack_wide_o(o, t):
    lane = lax.broadcasted_iota(jnp.int32, (t, LANES), 1)
    pairs = [jnp.where(lane < HEAD_DIM, o[2 * pr * t:(2 * pr + 1) * t, LANES:], o[(2 * pr + 1) * t:(2 * pr + 2) * t, :LANES])
             for pr in range(2)]
    return jnp.concatenate(pairs, axis=1)


FLASH_ROW_BLOCK = 64


def _flash_consume(s_ref, p_ref, stats, v, t, diag):
    m_ref, alpha_ref, l_ref, acc_ref = stats
    n = s_ref.shape[1]
    rb = FLASH_ROW_BLOCK
    lanes = [slice(c * LANES, (c + 1) * LANES) for c in range(n // LANES)]

    def scores(r0):
        s = s_ref[r0:r0 + rb, :]
        if diag:
            row = lax.broadcasted_iota(jnp.int32, (rb, n), 0) + r0 % t
            col = lax.broadcasted_iota(jnp.int32, (rb, n), 1)
            s = jnp.where(col <= row, s, MASKED)
        return s

    for r0 in range(0, 4 * t, rb):
        s = scores(r0)
        mx = s[:, lanes[0]]
        for c in lanes[1:]:
            mx = jnp.maximum(mx, s[:, c])
        m_old = m_ref[r0:r0 + rb, :]
        m_new = jnp.maximum(m_old, jnp.max(mx, axis=-1, keepdims=True))
        alpha_ref[r0:r0 + rb, :] = jnp.exp2(m_old - m_new)
        m_ref[r0:r0 + rb, :] = m_new
    for r0 in range(0, 4 * t, rb):
        s = scores(r0)
        m_new = m_ref[r0:r0 + rb, :]
        psum = jnp.zeros((rb, LANES), F32)
        for c in lanes:
            p = jnp.exp2(s[:, c] - m_new)
            psum = psum + p
            p_ref[r0:r0 + rb, c] = p.astype(BF16)
        l_ref[r0:r0 + rb, :] = alpha_ref[r0:r0 + rb, :] * l_ref[r0:r0 + rb, :] + psum
    alpha = alpha_ref[...]
    acc_ref[...] = (jnp.concatenate([alpha, alpha], axis=1) * acc_ref[...]
                    + jnp.dot(p_ref[...], v, preferred_element_type=F32))


def _flash_causal(i, produce, consume, s_a, s_b, stats):
    m_ref, alpha_ref, l_ref, acc_ref = stats
    m_ref[...] = jnp.full(m_ref.shape, MASKED, F32)
    l_ref[...] = jnp.zeros(l_ref.shape, F32)
    acc_ref[...] = jnp.zeros(acc_ref.shape, F32)
    produce(s_a, 0)

    def pair(jj, carry):
        j = 2 * jj
        produce(s_b, j + 1)
        consume(s_a, j, False)
        produce(s_a, j + 2)
        consume(s_b, j + 1, False)
        return carry

    lax.fori_loop(0, i // 2, pair, 0)

    @pl.when(i % 2 == 1)
    def _():
        consume(s_a, i - 1, False)
        produce(s_b, i)
        consume(s_b, i, True)

    @pl.when(i % 2 == 0)
    def _():
        consume(s_a, i, True)

    return acc_ref[...] * (1.0 / jnp.sum(l_ref[...], axis=-1, keepdims=True))


def _flash_scratch(t):
    return [pltpu.VMEM((4 * t, t), F32), pltpu.VMEM((4 * t, t), F32), pltpu.VMEM((4 * t, t), BF16),
            pltpu.VMEM((4 * t, LANES), F32), pltpu.VMEM((4 * t, LANES), F32), pltpu.VMEM((4 * t, LANES), F32),
            pltpu.VMEM((4 * t, 2 * LANES), F32)]


def _ffn_body(x_ref, g_ref, wg_ref, wu_ref, wd_ref, *rest, final):
    o_ref = rest[-1]
    x = x_ref[...]
    xn = _rms(x, g_ref[...]).astype(BF16)
    a = jnp.dot(xn, wg_ref[...], preferred_element_type=F32)
    b = jnp.dot(xn, wu_ref[...], preferred_element_type=F32)
    h = (a * jax.nn.sigmoid(a) * b).astype(BF16)
    y = x + 0.5 * jnp.dot(h, wd_ref[...], preferred_element_type=F32)
    if final:
        y = _rms(y, rest[0][...])
    o_ref[...] = y


def _ffn(x, g, wg, wu, wd, g_final=None):
    n, d = x.shape
    f = wg.shape[1]
    tm = 512 if n % 512 == 0 else n
    const = lambda i: (0, 0)
    once = pl.Buffered(1)
    in_specs = [
        pl.BlockSpec((tm, d), lambda i: (i, 0)),
        pl.BlockSpec((1, d), const),
        pl.BlockSpec((d, f), const, pipeline_mode=once),
        pl.BlockSpec((d, f), const, pipeline_mode=once),
        pl.BlockSpec((f, d), const, pipeline_mode=once),
    ]
    args = [x, g.reshape(1, d), wg, wu, wd]
    if g_final is not None:
        in_specs.append(pl.BlockSpec((1, d), const))
        args.append(g_final.reshape(1, d))
    return pl.pallas_call(
        functools.partial(_ffn_body, final=g_final is not None),
        out_shape=jax.ShapeDtypeStruct((n, d), F32),
        grid=(n // tm,),
        in_specs=in_specs,
        out_specs=pl.BlockSpec((tm, d), lambda i: (i, 0)),
        compiler_params=_params(("parallel",), VMEM_LIMIT),
        name="ffn",
    )(*args)


def _inproj_body(x_ref, g_ref, w_ref, zm_ref, zs_ref, *rest):
    zd_refs, zscr = rest[:-1], rest[-1]
    tm = x_ref.shape[0]
    xn = _rms(x_ref[...], g_ref[...]).astype(BF16)
    z = jnp.dot(xn, w_ref[...], preferred_element_type=F32)
    zm_ref[...] = z[:, :N_MAIN * LANES].astype(BF16)
    zs_ref[...] = z[:, (N_MAIN + N_DIL) * LANES:]
    for u in range(N_DIL):
        zscr[u] = z[:, (N_MAIN + u) * LANES:(N_MAIN + u + 1) * LANES]
    for (_, dd), ref in zip(DIL_PAIRS, zd_refs):
        for u in range(N_DIL):
            if dd == 1:
                ref[0, :, u * LANES:(u + 1) * LANES] = zscr[u].astype(BF16)
            else:
                for r in range(dd):
                    ref[r, :, u * LANES:(u + 1) * LANES] = zscr[u, pl.ds(r, tm // dd, stride=dd), :].astype(BF16)


def _inproj(x, g, w):
    bsz, s, d = x.shape
    tm = min(512, s)
    const = lambda b, i: (0, 0)
    row = lambda b, i: (b, i, 0)
    wd = N_DIL * LANES
    zd_shapes = tuple(jax.ShapeDtypeStruct((bsz, dd, s // dd, wd), BF16) for _, dd in DIL_PAIRS)
    zd_specs = tuple(pl.BlockSpec((None, dd, tm // dd, wd), lambda b, i: (b, 0, i, 0)) for _, dd in DIL_PAIRS)
    return pl.pallas_call(
        _inproj_body,
        out_shape=(jax.ShapeDtypeStruct((bsz, s, N_MAIN * LANES), BF16), jax.ShapeDtypeStruct((bsz, s, LANES), F32))
        + zd_shapes,
        grid=(bsz, s // tm),
        in_specs=[pl.BlockSpec((None, tm, d), row), pl.BlockSpec((1, d), const),
                  pl.BlockSpec((d, N_UNITS * LANES), const, pipeline_mode=pl.Buffered(1))],
        out_specs=(pl.BlockSpec((None, tm, N_MAIN * LANES), row), pl.BlockSpec((None, tm, LANES), row)) + zd_specs,
        scratch_shapes=[pltpu.VMEM((N_DIL, tm, LANES), F32)],
        compiler_params=_params(("parallel", "parallel"), VMEM_LIMIT),
        name="inproj",
    )(x, g.reshape(1, d), w)


def _cumsum_body(x_ref, b_ref, u_ref, c_ref, *, tk):
    x = x_ref[...] + b_ref[...]
    lf = (jnp.minimum(x, 0.0) - jnp.log(1.0 + jnp.exp(-jnp.abs(x)))) * LOG2E
    u = u_ref[...]
    carry = jnp.zeros((8, 1), F32)
    per = tk // LANES
    for j in range(x.shape[1] // LANES):
        seg = lf[:, j * LANES:(j + 1) * LANES]
        cs = jnp.dot(seg, u, precision=lax.Precision.HIGHEST, preferred_element_type=F32) + carry
        c_ref[j // per, :, (j % per) * LANES:(j % per + 1) * LANES] = cs
        carry = cs[:, LANES - 1:LANES]


def _fox_cumsum(f_rows, b_rows, tk):
    bsz, _, _, s = f_rows.shape
    u = jnp.asarray(np.triu(np.ones((LANES, LANES), np.float32)))
    return pl.pallas_call(
        functools.partial(_cumsum_body, tk=tk),
        out_shape=jax.ShapeDtypeStruct((bsz, 2, s // tk, 8, tk), F32),
        grid=(bsz, 2),
        in_specs=[pl.BlockSpec((None, None, 8, s), lambda b, p: (b, p, 0, 0)),
                  pl.BlockSpec((None, 8, 1), lambda b, p: (p, 0, 0)),
                  pl.BlockSpec((LANES, LANES), lambda b, p: (0, 0))],
        out_specs=pl.BlockSpec((None, None, s // tk, 8, tk), lambda b, p: (b, p, 0, 0, 0)),
        compiler_params=_params(("parallel", "parallel")),
        name="fox_cumsum",
    )(f_rows, b_rows, u)


def _fox_body(q_ref, k_ref, v_ref, c_ref, o_ref, qs_ref, s_a, s_b, p_ref, *stats, t):
    i = pl.program_id(1)
    qs_ref[...] = _stack_packed_q(q_ref[...])

    def rows(j):
        return pl.ds(pl.multiple_of(j * t, t), t)

    def produce(s_ref, j):
        s = _dot_nt(qs_ref[...], k_ref[rows(j), :])
        for h in range(4):
            s_ref[h * t:(h + 1) * t, :] = s[h * t:(h + 1) * t] - c_ref[h // 2, j, (h % 2):(h % 2) + 1, :]

    def consume(s_ref, j, diag):
        _flash_consume(s_ref, p_ref, stats, v_ref[rows(j), :], t, diag)

    o = _flash_causal(i, produce, consume, s_a, s_b, stats)
    o_ref[...] = _unstack_packed_o(o, t).astype(o_ref.dtype)


def _fox(zm, c, t):
    bsz, s, _ = zm.shape
    return pl.pallas_call(
        functools.partial(_fox_body, t=t),
        out_shape=jax.ShapeDtypeStruct((bsz, s, 2 * LANES), BF16),
        grid=(bsz, s // t),
        in_specs=[pl.BlockSpec((None, t, 2 * LANES), lambda b, i: (b, i, U_FOX_Q // 2)),
                  pl.BlockSpec((None, s, 2 * LANES), lambda b, i: (b, 0, U_FOX_K // 2)),
                  pl.BlockSpec((None, s, 2 * LANES), lambda b, i: (b, 0, U_FOX_V // 2)),
                  pl.BlockSpec((None, 2, s // t, 8, t), lambda b, i: (b, 0, 0, 0, 0))],
        out_specs=pl.BlockSpec((None, t, 2 * LANES), lambda b, i: (b, i, 0)),
        scratch_shapes=[pltpu.VMEM((4 * t, 2 * LANES), BF16)] + _flash_scratch(t),
        compiler_params=_params(("parallel", "arbitrary"), VMEM_LIMIT),
        name="fox_attn",
    )(zm, zm, zm, c)


def _banded_body(*refs, gb, tq, sq, span, s_loc, wpad, max_dist, slopes, wide, with_lse):
    i = pl.program_id(1)
    o_ref = refs[2] if wide else refs[3]
    for g in range(gb):
        for r in range(tq // sq):
            t0 = i * tq + r * sq
            rows = slice(r * sq, (r + 1) * sq)
            start = pl.multiple_of(jnp.clip(t0 - wpad, 0, s_loc - span), LANES)
            kpos = start + lax.broadcasted_iota(jnp.int32, (1, span), 1)
            tpos = t0 + lax.broadcasted_iota(jnp.int32, (sq, 1), 0)
            dist = tpos - kpos
            mask = lax.bitcast_convert_type(dist, jnp.uint32) <= jnp.uint32(max_dist)
            krel = (kpos - t0).astype(F32)
            trel = (tpos - t0).astype(F32)
            if wide:
                q_ref, kv_ref = refs[:2]
                qs = jnp.concatenate([q_ref[g, rows, h * LANES:(h + 1) * LANES] for h in range(4)], axis=0)
                k = kv_ref[g, pl.ds(start, span), :LANES]
                v = kv_ref[g, pl.ds(start, span), :]
            else:
                q_ref, k_ref, v_ref = refs[:3]
                qs = _stack_packed_q(q_ref[g, rows, :])
                k = k_ref[g, pl.ds(start, span), :]
                v = v_ref[g, pl.ds(start, span), :]
            s = _dot_nt(qs, k)
            s = jnp.concatenate([jnp.where(mask, s[h * sq:(h + 1) * sq] + (float(slopes[h]) * LOG2E) * krel, MASKED)
                                 for h in range(4)], axis=0)
            m = jnp.max(s, axis=-1, keepdims=True)
            e = jnp.exp2(s - m)
            l = jnp.sum(e, axis=-1, keepdims=True)
            o = jnp.dot(e.astype(BF16), v, preferred_element_type=F32) * (1.0 / l)
            o_ref[g, rows, :] = (_unstack_wide_o(o, sq) if wide else _unstack_packed_o(o, sq)).astype(o_ref.dtype)
            if with_lse:
                lse = (m + jnp.log2(l)) * LN2
                lane = lax.broadcasted_iota(jnp.int32, (sq, LANES), 1)
                acc = jnp.zeros((sq, LANES), F32)
                for h in range(4):
                    acc = jnp.where(lane == h, lse[h * sq:(h + 1) * sq] - float(slopes[h]) * trel, acc)
                refs[-1][g, rows, :] = acc


BANDED_ROWS = 1024


def _banded(arrs, units, slopes, max_dist, wpad, wide, with_lse, name):
    g, s_loc, _ = arrs[0].shape
    tq = min(BANDED_ROWS, s_loc)
    gb = min(g, BANDED_ROWS // tq)
    sq = min(tq, wpad, 256)
    span = min(sq + wpad, s_loc)
    if wide:
        in_specs = [pl.BlockSpec((gb, tq, 4 * LANES), lambda b, i: (b, i, units[0])),
                    pl.BlockSpec((gb, s_loc, 2 * LANES), lambda b, i: (b, 0, units[1]))]
    else:
        in_specs = [pl.BlockSpec((gb, tq, 2 * LANES), lambda b, i: (b, i, units[0])),
                    pl.BlockSpec((gb, s_loc, 2 * LANES), lambda b, i: (b, 0, units[1])),
                    pl.BlockSpec((gb, s_loc, 2 * LANES), lambda b, i: (b, 0, units[2]))]
    out_shape = [jax.ShapeDtypeStruct((g, s_loc, 2 * LANES), BF16)]
    out_specs = [pl.BlockSpec((gb, tq, 2 * LANES), lambda b, i: (b, i, 0))]
    if with_lse:
        out_shape.append(jax.ShapeDtypeStruct((g, s_loc, LANES), F32))
        out_specs.append(pl.BlockSpec((gb, tq, LANES), lambda b, i: (b, i, 0)))
    body = functools.partial(_banded_body, gb=gb, tq=tq, sq=sq, span=span, s_loc=s_loc, wpad=wpad,
                             max_dist=max_dist, slopes=tuple(float(x) for x in slopes), wide=wide, with_lse=with_lse)
    return pl.pallas_call(
        body, out_shape=tuple(out_shape), grid=(g // gb, s_loc // tq), in_specs=in_specs, out_specs=tuple(out_specs),
        compiler_params=_params(("parallel", "arbitrary")), name=name,
    )(*arrs)


def _compress_body(k_ref, v_ref, pe_ref, w1_ref, w2_ref, o_ref):
    n = k_ref.shape[0]
    hid = []
    for idx, ref in enumerate((k_ref, v_ref)):
        ch = ref[...].astype(F32)
        a = jnp.dot((ch + pe_ref[idx, 0]).astype(BF16), w1_ref[idx, 0], preferred_element_type=F32)
        b = jnp.dot((ch + pe_ref[idx, 1]).astype(BF16), w1_ref[idx, 1], preferred_element_type=F32)
        pre = a + pltpu.roll(b, n - 1, 0)
        hid.append((pre * jax.nn.sigmoid(pre)).astype(BF16))
    o_ref[...] = (jnp.dot(hid[0], w2_ref[0], preferred_element_type=F32)
                  + jnp.dot(hid[1], w2_ref[1], preferred_element_type=F32)).astype(o_ref.dtype)


def _compress(kch, vch, pe, w1, w2):
    bsz, n, w = kch.shape
    full = lambda *shape: pl.BlockSpec(shape, lambda b: (0,) * len(shape))
    blk = pl.BlockSpec((None, n, w), lambda b: (b, 0, 0))
    return pl.pallas_call(
        _compress_body,
        out_shape=jax.ShapeDtypeStruct((bsz, n, 2 * LANES), BF16),
        grid=(bsz,),
        in_specs=[blk, blk, full(2, 2, 1, w), full(2, 2, w, NSA_CMP_HIDDEN), full(2, NSA_CMP_HIDDEN, 2 * LANES)],
        out_specs=pl.BlockSpec((None, n, 2 * LANES), lambda b: (b, 0, 0)),
        compiler_params=_params(("parallel",)),
        name="nsa_compress",
    )(kch, vch, pe, w1, w2)


def _cmp_select_body(q_ref, kv_ref, ov_ref, o_ref, sb_ref, used_ref, *, tq, slopes, n_top):
    i = pl.program_id(1)
    t0 = i * tq
    n_cmp = kv_ref.shape[0]
    tpos = t0 + lax.broadcasted_iota(jnp.int32, (tq, 1), 0)
    cend = NSA_CMP_STRIDE * lax.broadcasted_iota(jnp.int32, (1, n_cmp), 1) + (NSA_CMP_LEN - 1)
    dist = tpos - cend
    mask = dist >= 0
    distf = dist.astype(F32)
    kv = kv_ref[...]
    s_all = _dot_nt(_stack_wide_q(q_ref), kv[:, :LANES])
    psum = jnp.zeros((tq, n_cmp), F32)
    ps = []
    for h in range(4):
        s = jnp.where(mask, s_all[h * tq:(h + 1) * tq] - (float(slopes[h]) * LOG2E) * distf, -jnp.inf)
        m = jnp.max(s, axis=-1, keepdims=True)
        m = jnp.where(m == -jnp.inf, 0.0, m)
        e = jnp.where(mask, jnp.exp2(s - m), 0.0)
        p = e / jnp.maximum(jnp.sum(e, axis=-1, keepdims=True), 1e-30)
        ps.append(p.astype(BF16))
        psum = psum + p
    o = jnp.dot(jnp.concatenate(ps, axis=0), kv, preferred_element_type=F32)
    o_ref[...] = _unstack_wide_o(o, tq).astype(o_ref.dtype)

    imp = lax.dot_general(ov_ref[...], psum, (((1,), (1,)), ((), ())), precision=lax.Precision.HIGHEST,
                          preferred_element_type=F32)
    j = lax.broadcasted_iota(jnp.int32, (HEAD_DIM, tq), 0)
    tl = t0 + lax.broadcasted_iota(jnp.int32, (HEAD_DIM, tq), 1)
    cur = lax.shift_right_logical(tl, 6)
    causal = j * NSA_SEL_LEN <= tl
    forced = (j == 0) | (j == cur) | (j == cur - 1)
    score = jnp.where(causal, jnp.where(forced, jnp.inf, imp), -jnp.inf)
    rank = jnp.zeros((HEAD_DIM, tq), F32)
    for jp in range(HEAD_DIM):
        other = score[jp:jp + 1, :]
        rank = rank + jnp.where(other > score, 1.0, jnp.where((other == score) & (j > jp), 1.0, 0.0))
    keep = causal & (rank < float(n_top))
    bias_t = jnp.where(keep, 0.0, MASKED)
    sb_ref[...] = jnp.concatenate([jnp.zeros((HEAD_DIM, tq), F32), bias_t], axis=0).T.astype(sb_ref.dtype)
    used = jnp.max(jnp.where(keep, 1.0, 0.0), axis=1, keepdims=True)
    used_ref[...] = jnp.broadcast_to(used, (HEAD_DIM, LANES)).astype(jnp.int32)


def _cmp_select(zm, kv_c, overlap_t, tq, slopes, n_top):
    bsz, s, _ = zm.shape
    n_cmp = kv_c.shape[1]
    return pl.pallas_call(
        functools.partial(_cmp_select_body, tq=tq, slopes=tuple(float(x) for x in slopes), n_top=n_top),
        out_shape=(jax.ShapeDtypeStruct((bsz, s, 2 * LANES), BF16), jax.ShapeDtypeStruct((bsz, s, LANES), BF16),
                   jax.ShapeDtypeStruct((bsz, s // tq, HEAD_DIM, LANES), jnp.int32)),
        grid=(bsz, s // tq),
        in_specs=[pl.BlockSpec((None, tq, 4 * LANES), lambda b, i: (b, i, U_NSA_Q // 4)),
                  pl.BlockSpec((None, n_cmp, 2 * LANES), lambda b, i: (b, 0, 0)),
                  pl.BlockSpec((HEAD_DIM, n_cmp), lambda b, i: (0, 0))],
        out_specs=(pl.BlockSpec((None, tq, 2 * LANES), lambda b, i: (b, i, 0)),
                   pl.BlockSpec((None, tq, LANES), lambda b, i: (b, i, 0)),
                   pl.BlockSpec((None, None, HEAD_DIM, LANES), lambda b, i: (b, i, 0, 0))),
        compiler_params=_params(("parallel", "parallel")),
        name="nsa_cmp_select",
    )(zm, kv_c, overlap_t)


def _sel_body(used_ref, q_ref, sb_ref, kv_ref, o_ref, kp_ref, qs_ref, lst_ref, s_a, s_b, p_ref, *stats, t, slopes):
    b = pl.program_id(0)
    i = pl.program_id(1)
    s_len = kv_ref.shape[0]

    @pl.when(i == 0)
    def _():
        pos = lax.broadcasted_iota(jnp.int32, (s_len, LANES), 0)
        ln = lax.broadcasted_iota(jnp.int32, (s_len, LANES), 1)
        onehot = jnp.where(lax.shift_right_logical(pos, 6) == ln - HEAD_DIM, 1.0, 0.0).astype(BF16)
        kp_ref[...] = jnp.where(ln < HEAD_DIM, kv_ref[:, :LANES], onehot)

    per = t // NSA_SEL_LEN
    n = jnp.int32(0)
    for c in range(s_len // t - 1):
        any_used = used_ref[b, i, c * per]
        for k in range(1, per):
            any_used = any_used | used_ref[b, i, c * per + k]
        lst_ref[n] = jnp.int32(c)
        n = n + jnp.where((c < i) & (any_used != 0), 1, 0)
    lst_ref[n] = i

    lane = lax.broadcasted_iota(jnp.int32, (t, LANES), 1)
    sb = sb_ref[...]
    for h in range(4):
        qs_ref[h * t:(h + 1) * t, :] = jnp.where(lane < HEAD_DIM, q_ref[:, h * LANES:(h + 1) * LANES], sb)
    krel0 = lax.broadcasted_iota(jnp.int32, (1, t), 1)

    def rows(j):
        return pl.ds(pl.multiple_of(j * t, t), t)

    def produce(s_ref, pos):
        j = lst_ref[pos]
        s = _dot_nt(qs_ref[...], kp_ref[rows(j), :])
        krel = (krel0 + (j - i) * t).astype(F32)
        for h in range(4):
            s_ref[h * t:(h + 1) * t, :] = s[h * t:(h + 1) * t] + (float(slopes[h]) * LOG2E) * krel

    def consume(s_ref, pos, diag):
        _flash_consume(s_ref, p_ref, stats, kv_ref[rows(lst_ref[pos]), :], t, diag)

    o = _flash_causal(n, produce, consume, s_a, s_b, stats)
    o_ref[...] = _unstack_wide_o(o, t).astype(o_ref.dtype)


def _sel(zm, sbias, used, t, slopes):
    bsz, s, _ = zm.shape
    grid_spec = pltpu.PrefetchScalarGridSpec(
        num_scalar_prefetch=1,
        grid=(bsz, s // t),
        in_specs=[pl.BlockSpec((None, t, 4 * LANES), lambda b, i, u: (b, i, U_NSA_Q // 4)),
                  pl.BlockSpec((None, t, LANES), lambda b, i, u: (b, i, 0)),
                  pl.BlockSpec((None, s, 2 * LANES), lambda b, i, u: (b, 0, U_SLC // 2))],
        out_specs=pl.BlockSpec((None, t, 2 * LANES), lambda b, i, u: (b, i, 0)),
        scratch_shapes=[pltpu.VMEM((s, LANES), BF16), pltpu.VMEM((4 * t, LANES), BF16),
                        pltpu.SMEM((s // t + 1,), jnp.int32)] + _flash_scratch(t),
    )
    return pl.pallas_call(
        functools.partial(_sel_body, t=t, slopes=tuple(float(x) for x in slopes)),
        out_shape=jax.ShapeDtypeStruct((bsz, s, 2 * LANES), BF16),
        grid_spec=grid_spec,
        compiler_params=_params(("parallel", "arbitrary"), VMEM_LIMIT),
        name="nsa_sel_attn",
    )(used, zm, sbias, zm)


def _spread_heads(vals, lane0, stride, tm):
    slot = lax.shift_right_logical(lax.broadcasted_iota(jnp.int32, (tm, 2 * LANES), 1), 6)
    out = jnp.broadcast_to(vals[:, lane0:lane0 + 1], (tm, 2 * LANES))
    for h in range(1, 4):
        c = lane0 + stride * h
        out = jnp.where(slot == h, vals[:, c:c + 1], out)
    return out


def _outproj_body(x_ref, ofox_ref, ocmp_ref, oslc_ref, owin_ref, zs_ref, oswa_ref, lswa_ref, sink_ref,
                  od1_ref, od4_ref, od16_ref, l1_ref, l4_ref, l16_ref, w_ref, o_ref, o_scr, l_scr):
    tm = x_ref.shape[0]

    def natural(ref, scr):
        dd, _, w = ref.shape
        if dd == 1:
            return ref[0].astype(F32)
        for u in range(w // LANES):
            for r in range(dd):
                scr[u, pl.ds(r, tm // dd, stride=dd), :] = ref[r, :, u * LANES:(u + 1) * LANES].astype(F32)
        return jnp.concatenate([scr[u] for u in range(w // LANES)], axis=1)

    gate = jax.nn.sigmoid(zs_ref[...])
    o_nsa = (_spread_heads(gate, GATE_LANE0, 3, tm) * ocmp_ref[...].astype(F32)
             + _spread_heads(gate, GATE_LANE0 + 1, 3, tm) * oslc_ref[...].astype(F32)
             + _spread_heads(gate, GATE_LANE0 + 2, 3, tm) * owin_ref[...].astype(F32))
    keep = jax.nn.sigmoid(lswa_ref[...] - sink_ref[...])
    o_swa = _spread_heads(keep, 0, 1, tm) * oswa_ref[...].astype(F32)
    l1, l4, l16 = natural(l1_ref, l_scr), natural(l4_ref, l_scr), natural(l16_ref, l_scr)
    m = jnp.maximum(jnp.maximum(l1, l4), l16)
    e1, e4, e16 = jnp.exp(l1 - m), jnp.exp(l4 - m), jnp.exp(l16 - m)
    inv = 1.0 / (e1 + e4 + e16)
    o_dil = _spread_heads(e1 * inv, 0, 1, tm) * natural(od1_ref, o_scr)
    o_dil = o_dil + _spread_heads(e4 * inv, 0, 1, tm) * natural(od4_ref, o_scr)
    o_dil = o_dil + _spread_heads(e16 * inv, 0, 1, tm) * natural(od16_ref, o_scr)
    y = x_ref[...]
    for g, o in enumerate((ofox_ref[...], o_nsa.astype(BF16), o_swa.astype(BF16), o_dil.astype(BF16))):
        y = y + jnp.dot(o, w_ref[g], preferred_element_type=F32)
    o_ref[...] = y


def _outproj(x, heads, zs, lses, sinks, w):
    bsz, s, d = x.shape
    tm = min(512, s)
    row = lambda b, i: (b, i, 0)
    wide = pl.BlockSpec((None, tm, 2 * LANES), row)
    narrow = pl.BlockSpec((None, tm, LANES), row)

    def strided(a):
        dd = a.shape[1]
        return pl.BlockSpec((None, dd, tm // dd, a.shape[-1]), lambda b, i: (b, 0, i, 0))

    ofox, ocmp, oslc, owin, oswa, od1, od4, od16 = heads
    lswa, l1, l4, l16 = lses
    return pl.pallas_call(
        _outproj_body,
        out_shape=jax.ShapeDtypeStruct((bsz, s, d), F32),
        grid=(bsz, s // tm),
        in_specs=[pl.BlockSpec((None, tm, d), row), wide, wide, wide, wide, narrow, wide, narrow,
                  pl.BlockSpec((1, LANES), lambda b, i: (0, 0)), strided(od1), strided(od4), strided(od16),
                  strided(l1), strided(l4), strided(l16),
                  pl.BlockSpec((4, 2 * LANES, d), lambda b, i: (0, 0, 0))],
        out_specs=pl.BlockSpec((None, tm, d), row),
        scratch_shapes=[pltpu.VMEM((2, tm, LANES), F32), pltpu.VMEM((1, tm, LANES), F32)],
        compiler_params=_params(("parallel", "parallel"), VMEM_LIMIT),
        name="outproj",
    )(x, ofox, ocmp, oslc, owin, zs, oswa, lswa, sinks, od1, od4, od16, l1, l4, l16, w)


def _overlap_t(n_cmp, n_sel):
    cs = np.arange(n_cmp) * NSA_CMP_STRIDE
    ss = np.arange(n_sel) * NSA_SEL_LEN
    ov = (cs[None, :] <= ss[:, None] + NSA_SEL_LEN - 1) & (cs[None, :] + NSA_CMP_LEN - 1 >= ss[:, None])
    out = np.zeros((HEAD_DIM, n_cmp), np.float32)
    out[:n_sel] = ov
    return jnp.asarray(out)


def _mixer(x, g_mix, w_in, fox_b_f, cmp_pe, cmp_w1, cmp_w2, swa_sinks, w_out):
    bsz, s, d = x.shape
    sl_swa, sl_nsa, sl_dil = _alibi_slopes()
    cols, scale = _w_in_layout()
    w_perm = _relayout_w_in(w_in, cols, scale)
    zm, zs, *zds = _inproj(x, g_mix, w_perm)

    t_fox = min(512, s)
    f_rows = jnp.transpose(zs[:, :, :4], (0, 2, 1)).reshape(bsz, 2, 2, s)
    f_rows = jnp.pad(f_rows, ((0, 0), (0, 0), (0, 6), (0, 0)))
    b_rows = jnp.pad(fox_b_f.reshape(2, 2, 1), ((0, 0), (0, 6), (0, 0)))
    c = _fox_cumsum(f_rows, b_rows, t_fox)
    o_fox = _fox(zm, c, t_fox)

    n_chunk = s // NSA_CMP_STRIDE
    n_sel = s // NSA_SEL_LEN
    assert n_sel <= HEAD_DIM
    kvc = zm[:, :, U_CMP * LANES:(U_CMP + 1) * LANES]
    kch = kvc[:, :, :HEAD_DIM].reshape(bsz, n_chunk, NSA_CMP_STRIDE * HEAD_DIM)
    vch = kvc[:, :, HEAD_DIM:].reshape(bsz, n_chunk, NSA_CMP_STRIDE * HEAD_DIM)
    pe = cmp_pe.reshape(2, 2, 1, NSA_CMP_STRIDE * HEAD_DIM)
    w1 = cmp_w1.reshape(2, 2, NSA_CMP_STRIDE * HEAD_DIM, NSA_CMP_HIDDEN).astype(BF16)
    zpad = jnp.zeros((NSA_CMP_HIDDEN, HEAD_DIM), F32)
    w2 = jnp.stack([jnp.concatenate([cmp_w2[0], zpad, zpad, cmp_w2[0]], axis=1),
                    jnp.concatenate([zpad, cmp_w2[1], cmp_w2[1], zpad], axis=1)]).astype(BF16)
    kv_c = _compress(kch, vch, pe, w1, w2)
    t_sel = min(512, s)
    o_cmp, sbias, used = _cmp_select(zm, kv_c, _overlap_t(n_chunk, n_sel), t_sel, sl_nsa, min(NSA_TOPN, n_sel))
    o_slc = _sel(zm, sbias, used[:, :, :, 0], t_sel, sl_nsa)
    (o_win,) = _banded((zm, zm), (U_NSA_Q // 4, U_WIN // 2), sl_nsa, NSA_WINDOW - 1, NSA_WINDOW,
                       True, False, "nsa_win_attn")

    o_swa, l_swa = _banded((zm, zm, zm), (U_SWA_Q // 2, U_SWA_K // 2, U_SWA_V // 2), sl_swa, SWA_WINDOW - 1,
                           SWA_WINDOW, False, True, "swa_attn")

    o_dil, l_dil = [], []
    for (window, dd), zd in zip(DIL_PAIRS, zds):
        zz = zd.reshape(bsz * dd, s // dd, N_DIL * LANES)
        o, l = _banded((zz, zz, zz), (0, 1, 2), sl_dil * dd, window // dd, LANES, False, True, "dil%d_attn" % dd)
        o_dil.append(o.reshape(bsz, dd, s // dd, 2 * LANES))
        l_dil.append(l.reshape(bsz, dd, s // dd, LANES))

    sinks = jnp.pad(swa_sinks.reshape(1, 4), ((0, 0), (0, LANES - 4)))
    heads = (o_fox, o_cmp, o_slc, o_win, o_swa, *o_dil)
    return _outproj(x, heads, zs, (l_swa, *l_dil), sinks, w_out.reshape(4, 2 * LANES, d).astype(BF16))


def kernel(x, norm_ffn1, ffn1_w_gate, ffn1_w_up, ffn1_w_down, norm_mix, w_in, fox_b_f, nsa_cmp_pe, nsa_cmp_w1,
           nsa_cmp_w2, swa_sinks, w_out, norm_ffn2, ffn2_w_gate, ffn2_w_up, ffn2_w_down, norm_final):
    bsz, s, d = x.shape
    depth = norm_ffn1.shape[0]
    h = x.reshape(bsz * s, d)
    for l in range(depth):
        h = _ffn(h, norm_ffn1[l], ffn1_w_gate[l].astype(BF16), ffn1_w_up[l].astype(BF16), ffn1_w_down[l].astype(BF16))
        h = _mixer(h.reshape(bsz, s, d), norm_mix[l], w_in[l], fox_b_f[l], nsa_cmp_pe[l], nsa_cmp_w1[l],
                   nsa_cmp_w2[l], swa_sinks[l], w_out[l]).reshape(bsz * s, d)
        h = _ffn(h, norm_ffn2[l], ffn2_w_gate[l].astype(BF16), ffn2_w_up[l].astype(BF16), ffn2_w_down[l].astype(BF16),
                 norm_final if l == depth - 1 else None)
    return h.reshape(bsz, s, d)
```

```python
import functools

import numpy as np
import jax
import jax.numpy as jnp
from jax import lax
from jax.experimental import pallas as pl
from jax.experimental.pallas import tpu as pltpu

F32 = jnp.float32
BF16 = jnp.bfloat16

HEAD_DIM = 64
LANES = 128
NSA_CMP_LEN = 32
NSA_CMP_STRIDE = 16
NSA_CMP_HIDDEN = 128
NSA_SEL_LEN = 64
NSA_TOPN = 16
NSA_WINDOW = 512
SWA_WINDOW = 128
DIL_PAIRS = ((128, 1), (512, 4), (2048, 16))
RMS_EPS = 1e-6
LOG2E = 1.4426950408889634
LN2 = 0.6931471805599453
MASKED = -1e30
VMEM_LIMIT = 56 * 1024 * 1024

IN_SPLITS = (
    ('fox_q', 256), ('fox_k', 256), ('fox_v', 256), ('fox_f', 4),
    ('nsa_q', 256), ('nsa_k_cmp', 64), ('nsa_v_cmp', 64), ('nsa_k_slc', 64), ('nsa_v_slc', 64),
    ('nsa_k_win', 64), ('nsa_v_win', 64), ('nsa_gate', 12),
    ('swa_q', 256), ('swa_k', 128), ('swa_v', 128),
    ('dil_q', 256), ('dil_k', 256), ('dil_v', 256),
)

U_NSA_Q = 0
U_FOX_Q, U_FOX_K, U_FOX_V = 4, 6, 8
U_SWA_Q, U_SWA_K, U_SWA_V = 10, 12, 14
U_SLC = 16
U_WIN = 18
U_CMP = 20
N_MAIN = 21
N_DIL = 6
N_UNITS = N_MAIN + N_DIL + 1
GATE_LANE0 = 4


def _alibi_slopes():
    n = 12
    s = 2.0 ** (-8.0 * np.arange(1, n + 1) / n)
    return s[:4], s[4:8], s[8:]


def _w_in_layout():
    off, o = {}, 0
    for name, w in IN_SPLITS:
        off[name] = o
        o += w
    cols, scale = [], []

    def seg(name, start, width, sc=1.0):
        cols.extend(range(off[name] + start, off[name] + start + width))
        scale.extend([sc] * width)

    def zeros(n):
        cols.extend([0] * n)
        scale.extend([0.0] * n)

    qs = HEAD_DIM ** -0.5 * LOG2E
    for h in range(4):
        seg('nsa_q', 64 * h, 64, qs)
        zeros(64)
    seg('fox_q', 0, 256, qs)
    seg('fox_k', 0, 256)
    seg('fox_v', 0, 256)
    seg('swa_q', 0, 256, qs)
    for name in ('swa_k', 'swa_v'):
        for kv in range(2):
            seg(name, 64 * kv, 64)
            seg(name, 64 * kv, 64)
    for a, b in (('nsa_k_slc', 'nsa_v_slc'), ('nsa_v_slc', 'nsa_k_slc'),
                 ('nsa_k_win', 'nsa_v_win'), ('nsa_v_win', 'nsa_k_win')):
        seg(a, 0, 64)
        seg(b, 0, 64)
    seg('nsa_k_cmp', 0, 64)
    seg('nsa_v_cmp', 0, 64)
    seg('dil_q', 0, 256, qs)
    seg('dil_k', 0, 256)
    seg('dil_v', 0, 256)
    seg('fox_f', 0, 4)
    seg('nsa_gate', 0, 12)
    zeros(LANES - 16)
    assert len(cols) == N_UNITS * LANES
    return np.asarray(cols, np.int32), np.asarray(scale, np.float32)


def _relayout_w_in(w_in, cols, scale):
    runs, start = [], 0
    for c in range(1, len(cols) + 1):
        same = c < len(cols) and scale[c] == scale[c - 1] and (scale[c] == 0.0 or cols[c] == cols[c - 1] + 1)
        if not same:
            runs.append((start, c))
            start = c
    parts = []
    for a, b in runs:
        if scale[a] == 0.0:
            parts.append(jnp.zeros((w_in.shape[0], b - a), BF16))
        else:
            parts.append((w_in[:, int(cols[a]):int(cols[a]) + (b - a)] * float(scale[a])).astype(BF16))
    return jnp.concatenate(parts, axis=1)


def _rms(x, g):
    ms = jnp.mean(x * x, axis=-1, keepdims=True)
    return x * lax.rsqrt(ms + RMS_EPS) * g


def _dot_nt(a, b):
    return lax.dot_general(a, b, (((1,), (1,)), ((), ())), preferred_element_type=F32)


def _params(sem, vmem=None):
    return pltpu.CompilerParams(dimension_semantics=sem, vmem_limit_bytes=vmem)


def _stack_packed_q(q):
    slot = lax.shift_right_logical(lax.broadcasted_iota(jnp.int32, q.shape, 1), 6)
    return jnp.concatenate([jnp.where(slot == h, q, jnp.zeros_like(q)) for h in range(4)], axis=0)


def _unstack_packed_o(o, t):
    slot = lax.shift_right_logical(lax.broadcasted_iota(jnp.int32, (t, 2 * LANES), 1), 6)
    out = o[:t]
    for h in range(1, 4):
        out = jnp.where(slot == h, o[h * t:(h + 1) * t], out)
    return out


def _stack_wide_q(q_ref):
    return jnp.concatenate([q_ref[:, h * LANES:(h + 1) * LANES] for h in range(4)], axis=0)


def _unstack_wide_o(o, t):
    lane = lax.broadcasted_iota(jnp.int32, (t, LANES), 1)
    pairs = [jnp.where(lane < HEAD_DIM, o[2 * pr * t:(2 * pr + 1) * t, LANES:], o[(2 * pr + 1) * t:(2 * pr + 2) * t, :LANES])
             for pr in range(2)]
    return jnp.concatenate(pairs, axis=1)


FLASH_ROW_BLOCK = 64


def _flash_consume(s_ref, p_ref, stats, v, t, diag):
    m_ref, alpha_ref, l_ref, acc_ref = stats
    n = s_ref.shape[1]
    rb = FLASH_ROW_BLOCK
    lanes = [slice(c * LANES, (c + 1) * LANES) for c in range(n // LANES)]

    def scores(r0):
        s = s_ref[r0:r0 + rb, :]
        if diag:
            row = lax.broadcasted_iota(jnp.int32, (rb, n), 0) + r0 % t
            col = lax.broadcasted_iota(jnp.int32, (rb, n), 1)
            s = jnp.where(col <= row, s, MASKED)
        return s

    for r0 in range(0, 4 * t, rb):
        s = scores(r0)
        mx = s[:, lanes[0]]
        for c in lanes[1:]:
            mx = jnp.maximum(mx, s[:, c])
        m_old = m_ref[r0:r0 + rb, :]
        m_new = jnp.maximum(m_old, jnp.max(mx, axis=-1, keepdims=True))
        alpha_ref[r0:r0 + rb, :] = jnp.exp2(m_old - m_new)
        m_ref[r0:r0 + rb, :] = m_new
    for r0 in range(0, 4 * t, rb):
        s = scores(r0)
        m_new = m_ref[r0:r0 + rb, :]
        psum = jnp.zeros((rb, LANES), F32)
        for c in lanes:
            p = jnp.exp2(s[:, c] - m_new)
            psum = psum + p
            p_ref[r0:r0 + rb, c] = p.astype(BF16)
        l_ref[r0:r0 + rb, :] = alpha_ref[r0:r0 + rb, :] * l_ref[r0:r0 + rb, :] + psum
    alpha = alpha_ref[...]
    acc_ref[...] = (jnp.concatenate([alpha, alpha], axis=1) * acc_ref[...]
                    + jnp.dot(p_ref[...], v, preferred_element_type=F32))


def _flash_causal(i, produce, consume, s_a, s_b, stats):
    m_ref, alpha_ref, l_ref, acc_ref = stats
    m_ref[...] = jnp.full(m_ref.shape, MASKED, F32)
    l_ref[...] = jnp.zeros(l_ref.shape, F32)
    acc_ref[...] = jnp.zeros(acc_ref.shape, F32)
    produce(s_a, 0)

    def pair(jj, carry):
        j = 2 * jj
        produce(s_b, j + 1)
        consume(s_a, j, False, 0)
        produce(s_a, j + 2)
        consume(s_b, j + 1, False, 1)
        return carry

    lax.fori_loop(0, i // 2, pair, 0)

    @pl.when(i % 2 == 1)
    def _():
        produce(s_b, i)
        consume(s_a, i - 1, False, 0)
        consume(s_b, i, True, 1)

    @pl.when(i % 2 == 0)
    def _():
        consume(s_a, i, True, 0)

    return acc_ref[...] * (1.0 / jnp.sum(l_ref[...], axis=-1, keepdims=True))


def _flash_scratch(t):
    return [pltpu.VMEM((4 * t, t), F32), pltpu.VMEM((4 * t, t), F32), pltpu.VMEM((2, 4 * t, t), BF16),
            pltpu.VMEM((4 * t, LANES), F32), pltpu.VMEM((4 * t, LANES), F32), pltpu.VMEM((4 * t, LANES), F32),
            pltpu.VMEM((4 * t, 2 * LANES), F32)]


def _ffn_body(x_ref, g_ref, wg_ref, wu_ref, wd_ref, *rest, final):
    o_ref = rest[-1]
    x = x_ref[...]
    xn = _rms(x, g_ref[...]).astype(BF16)
    a = jnp.dot(xn, wg_ref[...], preferred_element_type=F32)
    b = jnp.dot(xn, wu_ref[...], preferred_element_type=F32)
    h = (a * jax.nn.sigmoid(a) * b).astype(BF16)
    y = x + 0.5 * jnp.dot(h, wd_ref[...], preferred_element_type=F32)
    if final:
        y = _rms(y, rest[0][...])
    o_ref[...] = y


def _ffn(x, g, wg, wu, wd, g_final=None):
    n, d = x.shape
    f = wg.shape[1]
    tm = 512 if n % 512 == 0 else n
    const = lambda i: (0, 0)
    once = pl.Buffered(1)
    in_specs = [
        pl.BlockSpec((tm, d), lambda i: (i, 0)),
        pl.BlockSpec((1, d), const),
        pl.BlockSpec((d, f), const, pipeline_mode=once),
        pl.BlockSpec((d, f), const, pipeline_mode=once),
        pl.BlockSpec((f, d), const, pipeline_mode=once),
    ]
    args = [x, g.reshape(1, d), wg, wu, wd]
    if g_final is not None:
        in_specs.append(pl.BlockSpec((1, d), const))
        args.append(g_final.reshape(1, d))
    return pl.pallas_call(
        functools.partial(_ffn_body, final=g_final is not None),
        out_shape=jax.ShapeDtypeStruct((n, d), F32),
        grid=(n // tm,),
        in_specs=in_specs,
        out_specs=pl.BlockSpec((tm, d), lambda i: (i, 0)),
        compiler_params=_params(("parallel",), VMEM_LIMIT),
        name="ffn",
    )(*args)


def _inproj_body(x_ref, g_ref, w_ref, zm_ref, zs_ref, *rest):
    zd_refs, zscr = rest[:-1], rest[-1]
    tm = x_ref.shape[0]
    xn = _rms(x_ref[...], g_ref[...]).astype(BF16)
    z = jnp.dot(xn, w_ref[...], preferred_element_type=F32)
    zm_ref[...] = z[:, :N_MAIN * LANES].astype(BF16)
    zs_ref[...] = z[:, (N_MAIN + N_DIL) * LANES:]
    for u in range(N_DIL):
        zscr[u] = z[:, (N_MAIN + u) * LANES:(N_MAIN + u + 1) * LANES]
    for (_, dd), ref in zip(DIL_PAIRS, zd_refs):
        for u in range(N_DIL):
            if dd == 1:
                ref[0, :, u * LANES:(u + 1) * LANES] = zscr[u].astype(BF16)
            else:
                for r in range(dd):
                    ref[r, :, u * LANES:(u + 1) * LANES] = zscr[u, pl.ds(r, tm // dd, stride=dd), :].astype(BF16)


def _inproj(x, g, w):
    bsz, s, d = x.shape
    tm = min(512, s)
    const = lambda b, i: (0, 0)
    row = lambda b, i: (b, i, 0)
    wd = N_DIL * LANES
    zd_shapes = tuple(jax.ShapeDtypeStruct((bsz, dd, s // dd, wd), BF16) for _, dd in DIL_PAIRS)
    zd_specs = tuple(pl.BlockSpec((None, dd, tm // dd, wd), lambda b, i: (b, 0, i, 0)) for _, dd in DIL_PAIRS)
    return pl.pallas_call(
        _inproj_body,
        out_shape=(jax.ShapeDtypeStruct((bsz, s, N_MAIN * LANES), BF16), jax.ShapeDtypeStruct((bsz, s, LANES), F32))
        + zd_shapes,
        grid=(bsz, s // tm),
        in_specs=[pl.BlockSpec((None, tm, d), row), pl.BlockSpec((1, d), const),
                  pl.BlockSpec((d, N_UNITS * LANES), const, pipeline_mode=pl.Buffered(1))],
        out_specs=(pl.BlockSpec((None, tm, N_MAIN * LANES), row), pl.BlockSpec((None, tm, LANES), row)) + zd_specs,
        scratch_shapes=[pltpu.VMEM((N_DIL, tm, LANES), F32)],
        compiler_params=_params(("parallel", "parallel"), VMEM_LIMIT),
        name="inproj",
    )(x, g.reshape(1, d), w)


def _cumsum_body(x_ref, b_ref, u_ref, c_ref, *, tk):
    x = x_ref[...] + b_ref[...]
    lf = (jnp.minimum(x, 0.0) - jnp.log(1.0 + jnp.exp(-jnp.abs(x)))) * LOG2E
    u = u_ref[...]
    carry = jnp.zeros((8, 1), F32)
    per = tk // LANES
    for j in range(x.shape[1] // LANES):
        seg = lf[:, j * LANES:(j + 1) * LANES]
        cs = jnp.dot(seg, u, precision=lax.Precision.HIGHEST, preferred_element_type=F32) + carry
        c_ref[j // per, :, (j % per) * LANES:(j % per + 1) * LANES] = cs
        carry = cs[:, LANES - 1:LANES]


def _fox_cumsum(f_rows, b_rows, tk):
    bsz, _, _, s = f_rows.shape
    u = jnp.asarray(np.triu(np.ones((LANES, LANES), np.float32)))
    return pl.pallas_call(
        functools.partial(_cumsum_body, tk=tk),
        out_shape=jax.ShapeDtypeStruct((bsz, 2, s // tk, 8, tk), F32),
        grid=(bsz, 2),
        in_specs=[pl.BlockSpec((None, None, 8, s), lambda b, p: (b, p, 0, 0)),
                  pl.BlockSpec((None, 8, 1), lambda b, p: (p, 0, 0)),
                  pl.BlockSpec((LANES, LANES), lambda b, p: (0, 0))],
        out_specs=pl.BlockSpec((None, None, s // tk, 8, tk), lambda b, p: (b, p, 0, 0, 0)),
        compiler_params=_params(("parallel", "parallel")),
        name="fox_cumsum",
    )(f_rows, b_rows, u)


def _fox_body(q_ref, k_ref, v_ref, c_ref, o_ref, qs_ref, s_a, s_b, p_ref, *stats, t):
    i = pl.program_id(1)
    qs_ref[...] = _stack_packed_q(q_ref[...])

    def rows(j):
        return pl.ds(pl.multiple_of(j * t, t), t)

    def produce(s_ref, j):
        s = _dot_nt(qs_ref[...], k_ref[rows(j), :])
        for h in range(4):
            s_ref[h * t:(h + 1) * t, :] = s[h * t:(h + 1) * t] - c_ref[h // 2, j, (h % 2):(h % 2) + 1, :]

    def consume(s_ref, j, diag, slot):
        _flash_consume(s_ref, p_ref.at[slot], stats, v_ref[rows(j), :], t, diag)

    o = _flash_causal(i, produce, consume, s_a, s_b, stats)
    o_ref[...] = _unstack_packed_o(o, t).astype(o_ref.dtype)


def _fox(zm, c, t):
    bsz, s, _ = zm.shape
    return pl.pallas_call(
        functools.partial(_fox_body, t=t),
        out_shape=jax.ShapeDtypeStruct((bsz, s, 2 * LANES), BF16),
        grid=(bsz, s // t),
        in_specs=[pl.BlockSpec((None, t, 2 * LANES), lambda b, i: (b, i, U_FOX_Q // 2)),
                  pl.BlockSpec((None, s, 2 * LANES), lambda b, i: (b, 0, U_FOX_K // 2)),
                  pl.BlockSpec((None, s, 2 * LANES), lambda b, i: (b, 0, U_FOX_V // 2)),
                  pl.BlockSpec((None, 2, s // t, 8, t), lambda b, i: (b, 0, 0, 0, 0))],
        out_specs=pl.BlockSpec((None, t, 2 * LANES), lambda b, i: (b, i, 0)),
        scratch_shapes=[pltpu.VMEM((4 * t, 2 * LANES), BF16)] + _flash_scratch(t),
        compiler_params=_params(("parallel", "arbitrary"), VMEM_LIMIT),
        name="fox_attn",
    )(zm, zm, zm, c)


def _banded_body(*refs, gb, tq, sq, span, s_loc, wpad, max_dist, slopes, wide, with_lse):
    i = pl.program_id(1)
    o_ref = refs[2] if wide else refs[3]
    for g in range(gb):
        for r in range(tq // sq):
            t0 = i * tq + r * sq
            rows = slice(r * sq, (r + 1) * sq)
            start = pl.multiple_of(jnp.clip(t0 - wpad, 0, s_loc - span), LANES)
            kpos = start + lax.broadcasted_iota(jnp.int32, (1, span), 1)
            tpos = t0 + lax.broadcasted_iota(jnp.int32, (sq, 1), 0)
            dist = tpos - kpos
            mask = lax.bitcast_convert_type(dist, jnp.uint32) <= jnp.uint32(max_dist)
            krel = (kpos - t0).astype(F32)
            trel = (tpos - t0).astype(F32)
            if wide:
                q_ref, kv_ref = refs[:2]
                qs = jnp.concatenate([q_ref[g, rows, h * LANES:(h + 1) * LANES] for h in range(4)], axis=0)
                k = kv_ref[g, pl.ds(start, span), :LANES]
                v = kv_ref[g, pl.ds(start, span), :]
            else:
                q_ref, k_ref, v_ref = refs[:3]
                qs = _stack_packed_q(q_ref[g, rows, :])
                k = k_ref[g, pl.ds(start, span), :]
                v = v_ref[g, pl.ds(start, span), :]
            s = _dot_nt(qs, k)
            s = jnp.concatenate([jnp.where(mask, s[h * sq:(h + 1) * sq] + (float(slopes[h]) * LOG2E) * krel, MASKED)
                                 for h in range(4)], axis=0)
            m = jnp.max(s, axis=-1, keepdims=True)
            e = jnp.exp2(s - m)
            l = jnp.sum(e, axis=-1, keepdims=True)
            o = jnp.dot(e.astype(BF16), v, preferred_element_type=F32) * (1.0 / l)
            o_ref[g, rows, :] = (_unstack_wide_o(o, sq) if wide else _unstack_packed_o(o, sq)).astype(o_ref.dtype)
            if with_lse:
                lse = (m + jnp.log2(l)) * LN2
                lane = lax.broadcasted_iota(jnp.int32, (sq, LANES), 1)
                acc = jnp.zeros((sq, LANES), F32)
                for h in range(4):
                    acc = jnp.where(lane == h, lse[h * sq:(h + 1) * sq] - float(slopes[h]) * trel, acc)
                refs[-1][g, rows, :] = acc


BANDED_ROWS = 1024


def _banded(arrs, units, slopes, max_dist, wpad, wide, with_lse, name):
    g, s_loc, _ = arrs[0].shape
    tq = min(BANDED_ROWS, s_loc)
    gb = min(g, BANDED_ROWS // tq)
    sq = min(tq, wpad, 256)
    span = min(sq + wpad, s_loc)
    if wide:
        in_specs = [pl.BlockSpec((gb, tq, 4 * LANES), lambda b, i: (b, i, units[0])),
                    pl.BlockSpec((gb, s_loc, 2 * LANES), lambda b, i: (b, 0, units[1]))]
    else:
        in_specs = [pl.BlockSpec((gb, tq, 2 * LANES), lambda b, i: (b, i, units[0])),
                    pl.BlockSpec((gb, s_loc, 2 * LANES), lambda b, i: (b, 0, units[1])),
                    pl.BlockSpec((gb, s_loc, 2 * LANES), lambda b, i: (b, 0, units[2]))]
    out_shape = [jax.ShapeDtypeStruct((g, s_loc, 2 * LANES), BF16)]
    out_specs = [pl.BlockSpec((gb, tq, 2 * LANES), lambda b, i: (b, i, 0))]
    if with_lse:
        out_shape.append(jax.ShapeDtypeStruct((g, s_loc, LANES), F32))
        out_specs.append(pl.BlockSpec((gb, tq, LANES), lambda b, i: (b, i, 0)))
    body = functools.partial(_banded_body, gb=gb, tq=tq, sq=sq, span=span, s_loc=s_loc, wpad=wpad,
                             max_dist=max_dist, slopes=tuple(float(x) for x in slopes), wide=wide, with_lse=with_lse)
    return pl.pallas_call(
        body, out_shape=tuple(out_shape), grid=(g // gb, s_loc // tq), in_specs=in_specs, out_specs=tuple(out_specs),
        compiler_params=_params(("parallel", "arbitrary")), name=name,
    )(*arrs)


def _compress_body(k_ref, v_ref, pe_ref, w1_ref, w2_ref, o_ref):
    n = k_ref.shape[0]
    hid = []
    for idx, ref in enumerate((k_ref, v_ref)):
        ch = ref[...].astype(F32)
        a = jnp.dot((ch + pe_ref[idx, 0]).astype(BF16), w1_ref[idx, 0], preferred_element_type=F32)
        b = jnp.dot((ch + pe_ref[idx, 1]).astype(BF16), w1_ref[idx, 1], preferred_element_type=F32)
        pre = a + pltpu.roll(b, n - 1, 0)
        hid.append((pre * jax.nn.sigmoid(pre)).astype(BF16))
    o_ref[...] = (jnp.dot(hid[0], w2_ref[0], preferred_element_type=F32)
                  + jnp.dot(hid[1], w2_ref[1], preferred_element_type=F32)).astype(o_ref.dtype)


def _compress(kch, vch, pe, w1, w2):
    bsz, n, w = kch.shape
    full = lambda *shape: pl.BlockSpec(shape, lambda b: (0,) * len(shape))
    blk = pl.BlockSpec((None, n, w), lambda b: (b, 0, 0))
    return pl.pallas_call(
        _compress_body,
        out_shape=jax.ShapeDtypeStruct((bsz, n, 2 * LANES), BF16),
        grid=(bsz,),
        in_specs=[blk, blk, full(2, 2, 1, w), full(2, 2, w, NSA_CMP_HIDDEN), full(2, NSA_CMP_HIDDEN, 2 * LANES)],
        out_specs=pl.BlockSpec((None, n, 2 * LANES), lambda b: (b, 0, 0)),
        compiler_params=_params(("parallel",)),
        name="nsa_compress",
    )(kch, vch, pe, w1, w2)


def _cmp_select_body(q_ref, kv_ref, ov_ref, o_ref, sb_ref, used_ref, *, tq, slopes, n_top):
    i = pl.program_id(1)
    t0 = i * tq
    n_cmp = kv_ref.shape[0]
    tpos = t0 + lax.broadcasted_iota(jnp.int32, (tq, 1), 0)
    cend = NSA_CMP_STRIDE * lax.broadcasted_iota(jnp.int32, (1, n_cmp), 1) + (NSA_CMP_LEN - 1)
    dist = tpos - cend
    mask = dist >= 0
    distf = dist.astype(F32)
    kv = kv_ref[...]
    s_all = _dot_nt(_stack_wide_q(q_ref), kv[:, :LANES])
    psum = jnp.zeros((tq, n_cmp), F32)
    ps = []
    for h in range(4):
        s = jnp.where(mask, s_all[h * tq:(h + 1) * tq] - (float(slopes[h]) * LOG2E) * distf, -jnp.inf)
        m = jnp.max(s, axis=-1, keepdims=True)
        m = jnp.where(m == -jnp.inf, 0.0, m)
        e = jnp.where(mask, jnp.exp2(s - m), 0.0)
        p = e / jnp.maximum(jnp.sum(e, axis=-1, keepdims=True), 1e-30)
        ps.append(p.astype(BF16))
        psum = psum + p
    o = jnp.dot(jnp.concatenate(ps, axis=0), kv, preferred_element_type=F32)
    o_ref[...] = _unstack_wide_o(o, tq).astype(o_ref.dtype)

    imp = lax.dot_general(ov_ref[...], psum, (((1,), (1,)), ((), ())), precision=lax.Precision.HIGHEST,
                          preferred_element_type=F32)
    j = lax.broadcasted_iota(jnp.int32, (HEAD_DIM, tq), 0)
    tl = t0 + lax.broadcasted_iota(jnp.int32, (HEAD_DIM, tq), 1)
    cur = lax.shift_right_logical(tl, 6)
    causal = j * NSA_SEL_LEN <= tl
    forced = (j == 0) | (j == cur) | (j == cur - 1)
    score = jnp.where(causal, jnp.where(forced, jnp.inf, imp), -jnp.inf)
    rank = jnp.zeros((HEAD_DIM, tq), F32)
    for jp in range(HEAD_DIM):
        other = score[jp:jp + 1, :]
        rank = rank + jnp.where(other > score, 1.0, jnp.where((other == score) & (j > jp), 1.0, 0.0))
    keep = causal & (rank < float(n_top))
    bias_t = jnp.where(keep, 0.0, MASKED)
    sb_ref[...] = jnp.concatenate([jnp.zeros((HEAD_DIM, tq), F32), bias_t], axis=0).T.astype(sb_ref.dtype)
    used = jnp.max(jnp.where(keep, 1.0, 0.0), axis=1, keepdims=True)
    used_ref[...] = jnp.broadcast_to(used, (HEAD_DIM, LANES)).astype(jnp.int32)


def _cmp_select(zm, kv_c, overlap_t, tq, slopes, n_top):
    bsz, s, _ = zm.shape
    n_cmp = kv_c.shape[1]
    return pl.pallas_call(
        functools.partial(_cmp_select_body, tq=tq, slopes=tuple(float(x) for x in slopes), n_top=n_top),
        out_shape=(jax.ShapeDtypeStruct((bsz, s, 2 * LANES), BF16), jax.ShapeDtypeStruct((bsz, s, LANES), BF16),
                   jax.ShapeDtypeStruct((bsz, s // tq, HEAD_DIM, LANES), jnp.int32)),
        grid=(bsz, s // tq),
        in_specs=[pl.BlockSpec((None, tq, 4 * LANES), lambda b, i: (b, i, U_NSA_Q // 4)),
                  pl.BlockSpec((None, n_cmp, 2 * LANES), lambda b, i: (b, 0, 0)),
                  pl.BlockSpec((HEAD_DIM, n_cmp), lambda b, i: (0, 0))],
        out_specs=(pl.BlockSpec((None, tq, 2 * LANES), lambda b, i: (b, i, 0)),
                   pl.BlockSpec((None, tq, LANES), lambda b, i: (b, i, 0)),
                   pl.BlockSpec((None, None, HEAD_DIM, LANES), lambda b, i: (b, i, 0, 0))),
        compiler_params=_params(("parallel", "parallel")),
        name="nsa_cmp_select",
    )(zm, kv_c, overlap_t)


def _sel_body(used_ref, q_ref, sb_ref, kv_ref, o_ref, kp_ref, qs_ref, lst_ref, s_a, s_b, p_ref, *stats, t, slopes):
    b = pl.program_id(0)
    i = pl.program_id(1)
    s_len = kv_ref.shape[0]

    @pl.when(i == 0)
    def _():
        pos = lax.broadcasted_iota(jnp.int32, (s_len, LANES), 0)
        ln = lax.broadcasted_iota(jnp.int32, (s_len, LANES), 1)
        onehot = jnp.where(lax.shift_right_logical(pos, 6) == ln - HEAD_DIM, 1.0, 0.0).astype(BF16)
        kp_ref[...] = jnp.where(ln < HEAD_DIM, kv_ref[:, :LANES], onehot)

    per = t // NSA_SEL_LEN
    n = jnp.int32(0)
    for c in range(s_len // t - 1):
        any_used = used_ref[b, i, c * per]
        for k in range(1, per):
            any_used = any_used | used_ref[b, i, c * per + k]
        lst_ref[n] = jnp.int32(c)
        n = n + jnp.where((c < i) & (any_used != 0), 1, 0)
    lst_ref[n] = i

    lane = lax.broadcasted_iota(jnp.int32, (t, LANES), 1)
    sb = sb_ref[...]
    for h in range(4):
        qs_ref[h * t:(h + 1) * t, :] = jnp.where(lane < HEAD_DIM, q_ref[:, h * LANES:(h + 1) * LANES], sb)
    krel0 = lax.broadcasted_iota(jnp.int32, (1, t), 1)

    def rows(j):
        return pl.ds(pl.multiple_of(j * t, t), t)

    def produce(s_ref, pos):
        j = lst_ref[pos]
        s = _dot_nt(qs_ref[...], kp_ref[rows(j), :])
        krel = (krel0 + (j - i) * t).astype(F32)
        for h in range(4):
            s_ref[h * t:(h + 1) * t, :] = s[h * t:(h + 1) * t] + (float(slopes[h]) * LOG2E) * krel

    def consume(s_ref, pos, diag, slot):
        _flash_consume(s_ref, p_ref.at[slot], stats, kv_ref[rows(lst_ref[pos]), :], t, diag)

    o = _flash_causal(n, produce, consume, s_a, s_b, stats)
    o_ref[...] = _unstack_wide_o(o, t).astype(o_ref.dtype)


def _sel(zm, sbias, used, t, slopes):
    bsz, s, _ = zm.shape
    grid_spec = pltpu.PrefetchScalarGridSpec(
        num_scalar_prefetch=1,
        grid=(bsz, s // t),
        in_specs=[pl.BlockSpec((None, t, 4 * LANES), lambda b, i, u: (b, i, U_NSA_Q // 4)),
                  pl.BlockSpec((None, t, LANES), lambda b, i, u: (b, i, 0)),
                  pl.BlockSpec((None, s, 2 * LANES), lambda b, i, u: (b, 0, U_SLC // 2))],
        out_specs=pl.BlockSpec((None, t, 2 * LANES), lambda b, i, u: (b, i, 0)),
        scratch_shapes=[pltpu.VMEM((s, LANES), BF16), pltpu.VMEM((4 * t, LANES), BF16),
                        pltpu.SMEM((s // t + 1,), jnp.int32)] + _flash_scratch(t),
    )
    return pl.pallas_call(
        functools.partial(_sel_body, t=t, slopes=tuple(float(x) for x in slopes)),
        out_shape=jax.ShapeDtypeStruct((bsz, s, 2 * LANES), BF16),
        grid_spec=grid_spec,
        compiler_params=_params(("parallel", "arbitrary"), VMEM_LIMIT),
        name="nsa_sel_attn",
    )(used, zm, sbias, zm)


def _spread_heads(vals, lane0, stride, tm):
    lane = lax.broadcasted_iota(jnp.int32, (tm, LANES), 1)
    col = lambda h: vals[:, lane0 + stride * h:lane0 + stride * h + 1]
    halves = [jnp.where(lane < HEAD_DIM, col(2 * p), col(2 * p + 1)) for p in range(2)]
    return jnp.concatenate(halves, axis=1)


def _outproj_body(x_ref, ofox_ref, ocmp_ref, oslc_ref, owin_ref, zs_ref, oswa_ref, lswa_ref, sink_ref,
                  od1_ref, od4_ref, od16_ref, l1_ref, l4_ref, l16_ref, eg_ref, w_ref, o_ref, o_scr, l_scr):
    tm = x_ref.shape[0]

    def natural(ref, scr):
        dd, _, w = ref.shape
        if dd == 1:
            return ref[0].astype(F32)
        for u in range(w // LANES):
            for r in range(dd):
                scr[u, pl.ds(r, tm // dd, stride=dd), :] = ref[r, :, u * LANES:(u + 1) * LANES].astype(F32)
        return jnp.concatenate([scr[u] for u in range(w // LANES)], axis=1)

    gate = jax.nn.sigmoid(zs_ref[...])
    g_hi = gate.astype(BF16)
    g_lo = (gate - g_hi.astype(F32)).astype(BF16)
    spread = lambda br: (jnp.dot(g_hi, eg_ref[br], preferred_element_type=F32)
                         + jnp.dot(g_lo, eg_ref[br], preferred_element_type=F32))
    o_nsa = (spread(0) * ocmp_ref[...].astype(F32) + spread(1) * oslc_ref[...].astype(F32)
             + spread(2) * owin_ref[...].astype(F32))
    keep = jax.nn.sigmoid(lswa_ref[...] - sink_ref[...])
    o_swa = _spread_heads(keep, 0, 1, tm) * oswa_ref[...].astype(F32)
    l1, l4, l16 = natural(l1_ref, l_scr), natural(l4_ref, l_scr), natural(l16_ref, l_scr)
    m = jnp.maximum(jnp.maximum(l1, l4), l16)
    e1, e4, e16 = jnp.exp(l1 - m), jnp.exp(l4 - m), jnp.exp(l16 - m)
    inv = 1.0 / (e1 + e4 + e16)
    o_dil = _spread_heads(e1 * inv, 0, 1, tm) * natural(od1_ref, o_scr)
    o_dil = o_dil + _spread_heads(e4 * inv, 0, 1, tm) * natural(od4_ref, o_scr)
    o_dil = o_dil + _spread_heads(e16 * inv, 0, 1, tm) * natural(od16_ref, o_scr)
    y = x_ref[...]
    for g, o in enumerate((ofox_ref[...], o_nsa.astype(BF16), o_swa.astype(BF16), o_dil.astype(BF16))):
        y = y + jnp.dot(o, w_ref[g], preferred_element_type=F32)
    o_ref[...] = y


def _gate_spread_matrices():
    e = np.zeros((3, LANES, 2 * LANES), np.float32)
    for h in range(4):
        for br in range(3):
            e[br, GATE_LANE0 + 3 * h + br, HEAD_DIM * h:HEAD_DIM * (h + 1)] = 1.0
    return jnp.asarray(e, BF16)


def _outproj(x, heads, zs, lses, sinks, w):
    bsz, s, d = x.shape
    tm = min(512, s)
    row = lambda b, i: (b, i, 0)
    wide = pl.BlockSpec((None, tm, 2 * LANES), row)
    narrow = pl.BlockSpec((None, tm, LANES), row)

    def strided(a):
        dd = a.shape[1]
        return pl.BlockSpec((None, dd, tm // dd, a.shape[-1]), lambda b, i: (b, 0, i, 0))

    ofox, ocmp, oslc, owin, oswa, od1, od4, od16 = heads
    lswa, l1, l4, l16 = lses
    return pl.pallas_call(
        _outproj_body,
        out_shape=jax.ShapeDtypeStruct((bsz, s, d), F32),
        grid=(bsz, s // tm),
        in_specs=[pl.BlockSpec((None, tm, d), row), wide, wide, wide, wide, narrow, wide, narrow,
                  pl.BlockSpec((1, LANES), lambda b, i: (0, 0)), strided(od1), strided(od4), strided(od16),
                  strided(l1), strided(l4), strided(l16),
                  pl.BlockSpec((3, LANES, 2 * LANES), lambda b, i: (0, 0, 0)),
                  pl.BlockSpec((4, 2 * LANES, d), lambda b, i: (0, 0, 0))],
        out_specs=pl.BlockSpec((None, tm, d), row),
        scratch_shapes=[pltpu.VMEM((2, tm, LANES), F32), pltpu.VMEM((1, tm, LANES), F32)],
        compiler_params=_params(("parallel", "parallel"), VMEM_LIMIT),
        name="outproj",
    )(x, ofox, ocmp, oslc, owin, zs, oswa, lswa, sinks, od1, od4, od16, l1, l4, l16, _gate_spread_matrices(), w)


def _overlap_t(n_cmp, n_sel):
    cs = np.arange(n_cmp) * NSA_CMP_STRIDE
    ss = np.arange(n_sel) * NSA_SEL_LEN
    ov = (cs[None, :] <= ss[:, None] + NSA_SEL_LEN - 1) & (cs[None, :] + NSA_CMP_LEN - 1 >= ss[:, None])
    out = np.zeros((HEAD_DIM, n_cmp), np.float32)
    out[:n_sel] = ov
    return jnp.asarray(out)


def _mixer(x, g_mix, w_in, fox_b_f, cmp_pe, cmp_w1, cmp_w2, swa_sinks, w_out):
    bsz, s, d = x.shape
    sl_swa, sl_nsa, sl_dil = _alibi_slopes()
    cols, scale = _w_in_layout()
    w_perm = _relayout_w_in(w_in, cols, scale)
    zm, zs, *zds = _inproj(x, g_mix, w_perm)

    t_fox = min(512, s)
    f_rows = jnp.transpose(zs[:, :, :4], (0, 2, 1)).reshape(bsz, 2, 2, s)
    f_rows = jnp.pad(f_rows, ((0, 0), (0, 0), (0, 6), (0, 0)))
    b_rows = jnp.pad(fox_b_f.reshape(2, 2, 1), ((0, 0), (0, 6), (0, 0)))
    c = _fox_cumsum(f_rows, b_rows, t_fox)
    o_fox = _fox(zm, c, t_fox)

    n_chunk = s // NSA_CMP_STRIDE
    n_sel = s // NSA_SEL_LEN
    assert n_sel <= HEAD_DIM
    kvc = zm[:, :, U_CMP * LANES:(U_CMP + 1) * LANES]
    kch = kvc[:, :, :HEAD_DIM].reshape(bsz, n_chunk, NSA_CMP_STRIDE * HEAD_DIM)
    vch = kvc[:, :, HEAD_DIM:].reshape(bsz, n_chunk, NSA_CMP_STRIDE * HEAD_DIM)
    pe = cmp_pe.reshape(2, 2, 1, NSA_CMP_STRIDE * HEAD_DIM)
    w1 = cmp_w1.reshape(2, 2, NSA_CMP_STRIDE * HEAD_DIM, NSA_CMP_HIDDEN).astype(BF16)
    zpad = jnp.zeros((NSA_CMP_HIDDEN, HEAD_DIM), F32)
    w2 = jnp.stack([jnp.concatenate([cmp_w2[0], zpad, zpad, cmp_w2[0]], axis=1),
                    jnp.concatenate([zpad, cmp_w2[1], cmp_w2[1], zpad], axis=1)]).astype(BF16)
    kv_c = _compress(kch, vch, pe, w1, w2)
    t_sel = min(512, s)
    o_cmp, sbias, used = _cmp_select(zm, kv_c, _overlap_t(n_chunk, n_sel), t_sel, sl_nsa, min(NSA_TOPN, n_sel))
    o_slc = _sel(zm, sbias, used[:, :, :, 0], t_sel, sl_nsa)
    (o_win,) = _banded((zm, zm), (U_NSA_Q // 4, U_WIN // 2), sl_nsa, NSA_WINDOW - 1, NSA_WINDOW,
                       True, False, "nsa_win_attn")

    o_swa, l_swa = _banded((zm, zm, zm), (U_SWA_Q // 2, U_SWA_K // 2, U_SWA_V // 2), sl_swa, SWA_WINDOW - 1,
                           SWA_WINDOW, False, True, "swa_attn")

    o_dil, l_dil = [], []
    for (window, dd), zd in zip(DIL_PAIRS, zds):
        zz = zd.reshape(bsz * dd, s // dd, N_DIL * LANES)
        o, l = _banded((zz, zz, zz), (0, 1, 2), sl_dil * dd, window // dd, LANES, False, True, "dil%d_attn" % dd)
        o_dil.append(o.reshape(bsz, dd, s // dd, 2 * LANES))
        l_dil.append(l.reshape(bsz, dd, s // dd, LANES))

    sinks = jnp.pad(swa_sinks.reshape(1, 4), ((0, 0), (0, LANES - 4)))
    heads = (o_fox, o_cmp, o_slc, o_win, o_swa, *o_dil)
    return _outproj(x, heads, zs, (l_swa, *l_dil), sinks, w_out.reshape(4, 2 * LANES, d).astype(BF16))


def kernel(x, norm_ffn1, ffn1_w_gate, ffn1_w_up, ffn1_w_down, norm_mix, w_in, fox_b_f, nsa_cmp_pe, nsa_cmp_w1,
           nsa_cmp_w2, swa_sinks, w_out, norm_ffn2, ffn2_w_gate, ffn2_w_up, ffn2_w_down, norm_final):
    bsz, s, d = x.shape
    depth = norm_ffn1.shape[0]
    h = x.reshape(bsz * s, d)
    for l in range(depth):
        h = _ffn(h, norm_ffn1[l], ffn1_w_gate[l].astype(BF16), ffn1_w_up[l].astype(BF16), ffn1_w_down[l].astype(BF16))
        h = _mixer(h.reshape(bsz, s, d), norm_mix[l], w_in[l], fox_b_f[l], nsa_cmp_pe[l], nsa_cmp_w1[l],
                   nsa_cmp_w2[l], swa_sinks[l], w_out[l]).reshape(bsz * s, d)
        h = _ffn(h, norm_ffn2[l], ffn2_w_gate[l].astype(BF16), ffn2_w_up[l].astype(BF16), ffn2_w_down[l].astype(BF16),
                 norm_final if l == depth - 1 else None)
    return h.reshape(bsz, s, d)
```

```python
import functools

import numpy as np
import jax
import jax.numpy as jnp
from jax import lax
from jax.experimental import pallas as pl
from jax.experimental.pallas import tpu as pltpu

F32 = jnp.float32
BF16 = jnp.bfloat16

HEAD_DIM = 64
LANES = 128
NSA_CMP_LEN = 32
NSA_CMP_STRIDE = 16
NSA_CMP_HIDDEN = 128
NSA_SEL_LEN = 64
NSA_TOPN = 16
NSA_WINDOW = 512
SWA_WINDOW = 128
DIL_PAIRS = ((128, 1), (512, 4), (2048, 16))
RMS_EPS = 1e-6
LOG2E = 1.4426950408889634
LN2 = 0.6931471805599453
MASKED = -1e30
VMEM_LIMIT = 56 * 1024 * 1024

IN_SPLITS = (
    ('fox_q', 256), ('fox_k', 256), ('fox_v', 256), ('fox_f', 4),
    ('nsa_q', 256), ('nsa_k_cmp', 64), ('nsa_v_cmp', 64), ('nsa_k_slc', 64), ('nsa_v_slc', 64),
    ('nsa_k_win', 64), ('nsa_v_win', 64), ('nsa_gate', 12),
    ('swa_q', 256), ('swa_k', 128), ('swa_v', 128),
    ('dil_q', 256), ('dil_k', 256), ('dil_v', 256),
)

U_NSA_Q = 0
U_FOX_Q, U_FOX_K, U_FOX_V = 4, 6, 8
U_SWA_Q, U_SWA_K, U_SWA_V = 10, 12, 14
U_SLC = 16
U_WIN = 18
U_CMP = 20
N_MAIN = 21
N_DIL = 6
N_UNITS = N_MAIN + N_DIL + 1
GATE_LANE0 = 4


def _alibi_slopes():
    n = 12
    s = 2.0 ** (-8.0 * np.arange(1, n + 1) / n)
    return s[:4], s[4:8], s[8:]


def _w_in_layout():
    off, o = {}, 0
    for name, w in IN_SPLITS:
        off[name] = o
        o += w
    cols, scale = [], []

    def seg(name, start, width, sc=1.0):
        cols.extend(range(off[name] + start, off[name] + start + width))
        scale.extend([sc] * width)

    def zeros(n):
        cols.extend([0] * n)
        scale.extend([0.0] * n)

    qs = HEAD_DIM ** -0.5 * LOG2E
    for h in range(4):
        seg('nsa_q', 64 * h, 64, qs)
        zeros(64)
    seg('fox_q', 0, 256, qs)
    seg('fox_k', 0, 256)
    seg('fox_v', 0, 256)
    seg('swa_q', 0, 256, qs)
    for name in ('swa_k', 'swa_v'):
        for kv in range(2):
            seg(name, 64 * kv, 64)
            seg(name, 64 * kv, 64)
    for a, b in (('nsa_k_slc', 'nsa_v_slc'), ('nsa_v_slc', 'nsa_k_slc'),
                 ('nsa_k_win', 'nsa_v_win'), ('nsa_v_win', 'nsa_k_win')):
        seg(a, 0, 64)
        seg(b, 0, 64)
    seg('nsa_k_cmp', 0, 64)
    seg('nsa_v_cmp', 0, 64)
    seg('dil_q', 0, 256, qs)
    seg('dil_k', 0, 256)
    seg('dil_v', 0, 256)
    seg('fox_f', 0, 4)
    seg('nsa_gate', 0, 12)
    zeros(LANES - 16)
    assert len(cols) == N_UNITS * LANES
    return np.asarray(cols, np.int32), np.asarray(scale, np.float32)


def _relayout_w_in(w_in, cols, scale):
    runs, start = [], 0
    for c in range(1, len(cols) + 1):
        same = c < len(cols) and scale[c] == scale[c - 1] and (scale[c] == 0.0 or cols[c] == cols[c - 1] + 1)
        if not same:
            runs.append((start, c))
            start = c
    parts = []
    for a, b in runs:
        if scale[a] == 0.0:
            parts.append(jnp.zeros((w_in.shape[0], b - a), BF16))
        else:
            parts.append((w_in[:, int(cols[a]):int(cols[a]) + (b - a)] * float(scale[a])).astype(BF16))
    return jnp.concatenate(parts, axis=1)


def _rms(x, g):
    ms = jnp.mean(x * x, axis=-1, keepdims=True)
    return x * lax.rsqrt(ms + RMS_EPS) * g


def _dot_nt(a, b):
    return lax.dot_general(a, b, (((1,), (1,)), ((), ())), preferred_element_type=F32)


def _params(sem, vmem=None):
    return pltpu.CompilerParams(dimension_semantics=sem, vmem_limit_bytes=vmem)


def _stack_packed_q(q):
    slot = lax.shift_right_logical(lax.broadcasted_iota(jnp.int32, q.shape, 1), 6)
    return jnp.concatenate([jnp.where(slot == h, q, jnp.zeros_like(q)) for h in range(4)], axis=0)


def _unstack_packed_o(o, t):
    slot = lax.shift_right_logical(lax.broadcasted_iota(jnp.int32, (t, 2 * LANES), 1), 6)
    out = o[:t]
    for h in range(1, 4):
        out = jnp.where(slot == h, o[h * t:(h + 1) * t], out)
    return out


def _stack_wide_q(q_ref):
    return jnp.concatenate([q_ref[:, h * LANES:(h + 1) * LANES] for h in range(4)], axis=0)


def _unstack_wide_o(o, t):
    lane = lax.broadcasted_iota(jnp.int32, (t, LANES), 1)
    pairs = [jnp.where(lane < HEAD_DIM, o[2 * pr * t:(2 * pr + 1) * t, LANES:], o[(2 * pr + 1) * t:(2 * pr + 2) * t, :LANES])
             for pr in range(2)]
    return jnp.concatenate(pairs, axis=1)


FLASH_ROW_BLOCK = 64


def _flash_consume(s_ref, p_ref, stats, v, t, diag):
    m_ref, alpha_ref, l_ref, acc_ref = stats
    n = s_ref.shape[1]
    rb = FLASH_ROW_BLOCK
    lanes = [slice(c * LANES, (c + 1) * LANES) for c in range(n // LANES)]

    def scores(r0):
        s = s_ref[r0:r0 + rb, :]
        if diag:
            row = lax.broadcasted_iota(jnp.int32, (rb, n), 0) + r0 % t
            col = lax.broadcasted_iota(jnp.int32, (rb, n), 1)
            s = jnp.where(col <= row, s, MASKED)
        return s

    for r0 in range(0, 4 * t, rb):
        s = scores(r0)
        mx = s[:, lanes[0]]
        for c in lanes[1:]:
            mx = jnp.maximum(mx, s[:, c])
        m_old = m_ref[r0:r0 + rb, :]
        m_new = jnp.maximum(m_old, jnp.max(mx, axis=-1, keepdims=True))
        alpha_ref[r0:r0 + rb, :] = jnp.exp2(m_old - m_new)
        m_ref[r0:r0 + rb, :] = m_new
    for r0 in range(0, 4 * t, rb):
        s = scores(r0)
        m_new = m_ref[r0:r0 + rb, :]
        psum = jnp.zeros((rb, LANES), F32)
        for c in lanes:
            p = jnp.exp2(s[:, c] - m_new)
            psum = psum + p
            p_ref[r0:r0 + rb, c] = p.astype(BF16)
        l_ref[r0:r0 + rb, :] = alpha_ref[r0:r0 + rb, :] * l_ref[r0:r0 + rb, :] + psum
    alpha = alpha_ref[...]
    acc_ref[...] = (jnp.concatenate([alpha, alpha], axis=1) * acc_ref[...]
                    + jnp.dot(p_ref[...], v, preferred_element_type=F32))


def _flash_causal(i, produce, consume, s_a, s_b, stats):
    m_ref, alpha_ref, l_ref, acc_ref = stats
    m_ref[...] = jnp.full(m_ref.shape, MASKED, F32)
    l_ref[...] = jnp.zeros(l_ref.shape, F32)
    acc_ref[...] = jnp.zeros(acc_ref.shape, F32)
    produce(s_a, 0)

    def pair(jj, carry):
        j = 2 * jj
        produce(s_b, j + 1)
        consume(s_a, j, False, 0)
        produce(s_a, j + 2)
        consume(s_b, j + 1, False, 1)
        return carry

    lax.fori_loop(0, i // 2, pair, 0)

    @pl.when(i % 2 == 1)
    def _():
        produce(s_b, i)
        consume(s_a, i - 1, False, 0)
        consume(s_b, i, True, 1)

    @pl.when(i % 2 == 0)
    def _():
        consume(s_a, i, True, 0)

    return acc_ref[...] * (1.0 / jnp.sum(l_ref[...], axis=-1, keepdims=True))


def _flash_scratch(t):
    return [pltpu.VMEM((4 * t, t), F32), pltpu.VMEM((4 * t, t), F32), pltpu.VMEM((2, 4 * t, t), BF16),
            pltpu.VMEM((4 * t, LANES), F32), pltpu.VMEM((4 * t, LANES), F32), pltpu.VMEM((4 * t, LANES), F32),
            pltpu.VMEM((4 * t, 2 * LANES), F32)]


def _ffn_body(x_ref, g_ref, wg_ref, wu_ref, wd_ref, *rest, final):
    o_ref = rest[-1]
    x = x_ref[...]
    xn = _rms(x, g_ref[...]).astype(BF16)
    a = jnp.dot(xn, wg_ref[...], preferred_element_type=F32)
    b = jnp.dot(xn, wu_ref[...], preferred_element_type=F32)
    h = (a * jax.nn.sigmoid(a) * b).astype(BF16)
    y = x + 0.5 * jnp.dot(h, wd_ref[...], preferred_element_type=F32)
    if final:
        y = _rms(y, rest[0][...])
    o_ref[...] = y


def _ffn(x, g, wg, wu, wd, g_final=None):
    n, d = x.shape
    f = wg.shape[1]
    tm = 512 if n % 512 == 0 else n
    const = lambda i: (0, 0)
    once = pl.Buffered(1)
    in_specs = [
        pl.BlockSpec((tm, d), lambda i: (i, 0)),
        pl.BlockSpec((1, d), const),
        pl.BlockSpec((d, f), const, pipeline_mode=once),
        pl.BlockSpec((d, f), const, pipeline_mode=once),
        pl.BlockSpec((f, d), const, pipeline_mode=once),
    ]
    args = [x, g.reshape(1, d), wg, wu, wd]
    if g_final is not None:
        in_specs.append(pl.BlockSpec((1, d), const))
        args.append(g_final.reshape(1, d))
    return pl.pallas_call(
        functools.partial(_ffn_body, final=g_final is not None),
        out_shape=jax.ShapeDtypeStruct((n, d), F32),
        grid=(n // tm,),
        in_specs=in_specs,
        out_specs=pl.BlockSpec((tm, d), lambda i: (i, 0)),
        compiler_params=_params(("parallel",), VMEM_LIMIT),
        name="ffn",
    )(*args)


def _inproj_body(x_ref, g_ref, w_ref, zm_ref, zs_ref, *rest):
    zd_refs, zscr = rest[:-1], rest[-1]
    tm = x_ref.shape[0]
    xn = _rms(x_ref[...], g_ref[...]).astype(BF16)
    z = jnp.dot(xn, w_ref[...], preferred_element_type=F32)
    zm_ref[...] = z[:, :N_MAIN * LANES].astype(BF16)
    zs_ref[...] = z[:, (N_MAIN + N_DIL) * LANES:]
    for u in range(N_DIL):
        zscr[u] = z[:, (N_MAIN + u) * LANES:(N_MAIN + u + 1) * LANES]
    for (_, dd), ref in zip(DIL_PAIRS, zd_refs):
        for u in range(N_DIL):
            if dd == 1:
                ref[0, :, u * LANES:(u + 1) * LANES] = zscr[u].astype(BF16)
            else:
                for r in range(dd):
                    ref[r, :, u * LANES:(u + 1) * LANES] = zscr[u, pl.ds(r, tm // dd, stride=dd), :].astype(BF16)


def _inproj(x, g, w):
    bsz, s, d = x.shape
    tm = min(512, s)
    const = lambda b, i: (0, 0)
    row = lambda b, i: (b, i, 0)
    wd = N_DIL * LANES
    zd_shapes = tuple(jax.ShapeDtypeStruct((bsz, dd, s // dd, wd), BF16) for _, dd in DIL_PAIRS)
    zd_specs = tuple(pl.BlockSpec((None, dd, tm // dd, wd), lambda b, i: (b, 0, i, 0)) for _, dd in DIL_PAIRS)
    return pl.pallas_call(
        _inproj_body,
        out_shape=(jax.ShapeDtypeStruct((bsz, s, N_MAIN * LANES), BF16), jax.ShapeDtypeStruct((bsz, s, LANES), F32))
        + zd_shapes,
        grid=(bsz, s // tm),
        in_specs=[pl.BlockSpec((None, tm, d), row), pl.BlockSpec((1, d), const),
                  pl.BlockSpec((d, N_UNITS * LANES), const, pipeline_mode=pl.Buffered(1))],
        out_specs=(pl.BlockSpec((None, tm, N_MAIN * LANES), row), pl.BlockSpec((None, tm, LANES), row)) + zd_specs,
        scratch_shapes=[pltpu.VMEM((N_DIL, tm, LANES), F32)],
        compiler_params=_params(("parallel", "parallel"), VMEM_LIMIT),
        name="inproj",
    )(x, g.reshape(1, d), w)


def _cumsum_body(x_ref, b_ref, u_ref, c_ref, *, tk):
    x = x_ref[...] + b_ref[...]
    lf = (jnp.minimum(x, 0.0) - jnp.log(1.0 + jnp.exp(-jnp.abs(x)))) * LOG2E
    u = u_ref[...]
    per = tk // LANES
    local = [jnp.dot(lf[:, j * LANES:(j + 1) * LANES], u, precision=lax.Precision.HIGHEST, preferred_element_type=F32)
             for j in range(x.shape[1] // LANES)]
    carry = jnp.zeros((8, 1), F32)
    for j, cs in enumerate(local):
        c_ref[j // per, :, (j % per) * LANES:(j % per + 1) * LANES] = cs + carry
        carry = carry + cs[:, LANES - 1:LANES]


def _fox_cumsum(f_rows, b_rows, tk):
    bsz, _, s = f_rows.shape
    u = jnp.asarray(np.triu(np.ones((LANES, LANES), np.float32)))
    return pl.pallas_call(
        functools.partial(_cumsum_body, tk=tk),
        out_shape=jax.ShapeDtypeStruct((bsz, s // tk, 8, tk), F32),
        grid=(bsz,),
        in_specs=[pl.BlockSpec((None, 8, s), lambda b: (b, 0, 0)),
                  pl.BlockSpec((8, 1), lambda b: (0, 0)),
                  pl.BlockSpec((LANES, LANES), lambda b: (0, 0))],
        out_specs=pl.BlockSpec((None, s // tk, 8, tk), lambda b: (b, 0, 0, 0)),
        compiler_params=_params(("parallel",)),
        name="fox_cumsum",
    )(f_rows, b_rows, u)


def _fox_body(q_ref, k_ref, v_ref, c_ref, o_ref, qs_ref, s_a, s_b, p_ref, *stats, t):
    i = pl.program_id(1)
    qs_ref[...] = _stack_packed_q(q_ref[...])

    def rows(j):
        return pl.ds(pl.multiple_of(j * t, t), t)

    def produce(s_ref, j):
        s = _dot_nt(qs_ref[...], k_ref[rows(j), :])
        for h in range(4):
            s_ref[h * t:(h + 1) * t, :] = s[h * t:(h + 1) * t] - c_ref[j, h:h + 1, :]

    def consume(s_ref, j, diag, slot):
        _flash_consume(s_ref, p_ref.at[slot], stats, v_ref[rows(j), :], t, diag)

    o = _flash_causal(i, produce, consume, s_a, s_b, stats)
    o_ref[...] = _unstack_packed_o(o, t).astype(o_ref.dtype)


def _fox(zm, c, t):
    bsz, s, _ = zm.shape
    return pl.pallas_call(
        functools.partial(_fox_body, t=t),
        out_shape=jax.ShapeDtypeStruct((bsz, s, 2 * LANES), BF16),
        grid=(bsz, s // t),
        in_specs=[pl.BlockSpec((None, t, 2 * LANES), lambda b, i: (b, i, U_FOX_Q // 2)),
                  pl.BlockSpec((None, s, 2 * LANES), lambda b, i: (b, 0, U_FOX_K // 2)),
                  pl.BlockSpec((None, s, 2 * LANES), lambda b, i: (b, 0, U_FOX_V // 2)),
                  pl.BlockSpec((None, s // t, 8, t), lambda b, i: (b, 0, 0, 0))],
        out_specs=pl.BlockSpec((None, t, 2 * LANES), lambda b, i: (b, i, 0)),
        scratch_shapes=[pltpu.VMEM((4 * t, 2 * LANES), BF16)] + _flash_scratch(t),
        compiler_params=_params(("parallel", "arbitrary"), VMEM_LIMIT),
        name="fox_attn",
    )(zm, zm, zm, c)


def _banded_body(*refs, gb, tq, sq, span, s_loc, wpad, max_dist, slopes, wide, with_lse):
    i = pl.program_id(1)
    o_ref = refs[2] if wide else refs[3]
    trel = lax.broadcasted_iota(jnp.int32, (sq, 1), 0).astype(F32)

    def band_bias(kpos_minus_t0):
        dist = lax.broadcasted_iota(jnp.int32, (sq, 1), 0) - kpos_minus_t0
        mask = lax.bitcast_convert_type(dist, jnp.uint32) <= jnp.uint32(max_dist)
        krel = kpos_minus_t0.astype(F32)
        return [jnp.where(mask, (float(slopes[h]) * LOG2E) * krel, MASKED) for h in range(4)]

    unclamped = [span == sq + wpad and r * sq >= wpad for r in range(tq // sq)]
    shared = band_bias(lax.broadcasted_iota(jnp.int32, (1, span), 1) - wpad) if any(unclamped) else None
    for g in range(gb):
        for r in range(tq // sq):
            t0 = i * tq + r * sq
            rows = slice(r * sq, (r + 1) * sq)
            if unclamped[r]:
                start = pl.multiple_of(t0 - wpad, LANES)
                bias = shared
            else:
                start = pl.multiple_of(jnp.clip(t0 - wpad, 0, s_loc - span), LANES)
                bias = band_bias(start - t0 + lax.broadcasted_iota(jnp.int32, (1, span), 1))
            if wide:
                q_ref, kv_ref = refs[:2]
                qs = jnp.concatenate([q_ref[g, rows, h * LANES:(h + 1) * LANES] for h in range(4)], axis=0)
                k = kv_ref[g, pl.ds(start, span), :LANES]
                v = kv_ref[g, pl.ds(start, span), :]
            else:
                q_ref, k_ref, v_ref = refs[:3]
                qs = _stack_packed_q(q_ref[g, rows, :])
                k = k_ref[g, pl.ds(start, span), :]
                v = v_ref[g, pl.ds(start, span), :]
            s = _dot_nt(qs, k)
            s = jnp.concatenate([s[h * sq:(h + 1) * sq] + bias[h] for h in range(4)], axis=0)
            m = jnp.max(s, axis=-1, keepdims=True)
            e = jnp.exp2(s - m)
            l = jnp.sum(e, axis=-1, keepdims=True)
            o = jnp.dot(e.astype(BF16), v, preferred_element_type=F32) * (1.0 / l)
            o_ref[g, rows, :] = (_unstack_wide_o(o, sq) if wide else _unstack_packed_o(o, sq)).astype(o_ref.dtype)
            if with_lse:
                lse = (m + jnp.log2(l)) * LN2
                lane = lax.broadcasted_iota(jnp.int32, (sq, LANES), 1)
                acc = jnp.zeros((sq, LANES), F32)
                for h in range(4):
                    acc = jnp.where(lane == h, lse[h * sq:(h + 1) * sq] - float(slopes[h]) * trel, acc)
                refs[-1][g, rows, :] = acc


BANDED_ROWS = 1024


def _banded(arrs, units, slopes, max_dist, wpad, wide, with_lse, name):
    g, s_loc, _ = arrs[0].shape
    tq = min(BANDED_ROWS, s_loc)
    gb = min(g, BANDED_ROWS // tq)
    sq = min(tq, wpad, 256)
    span = min(sq + wpad, s_loc)
    if wide:
        in_specs = [pl.BlockSpec((gb, tq, 4 * LANES), lambda b, i: (b, i, units[0])),
                    pl.BlockSpec((gb, s_loc, 2 * LANES), lambda b, i: (b, 0, units[1]))]
    else:
        in_specs = [pl.BlockSpec((gb, tq, 2 * LANES), lambda b, i: (b, i, units[0])),
                    pl.BlockSpec((gb, s_loc, 2 * LANES), lambda b, i: (b, 0, units[1])),
                    pl.BlockSpec((gb, s_loc, 2 * LANES), lambda b, i: (b, 0, units[2]))]
    out_shape = [jax.ShapeDtypeStruct((g, s_loc, 2 * LANES), BF16)]
    out_specs = [pl.BlockSpec((gb, tq, 2 * LANES), lambda b, i: (b, i, 0))]
    if with_lse:
        out_shape.append(jax.ShapeDtypeStruct((g, s_loc, LANES), F32))
        out_specs.append(pl.BlockSpec((gb, tq, LANES), lambda b, i: (b, i, 0)))
    body = functools.partial(_banded_body, gb=gb, tq=tq, sq=sq, span=span, s_loc=s_loc, wpad=wpad,
                             max_dist=max_dist, slopes=tuple(float(x) for x in slopes), wide=wide, with_lse=with_lse)
    return pl.pallas_call(
        body, out_shape=tuple(out_shape), grid=(g // gb, s_loc // tq), in_specs=in_specs, out_specs=tuple(out_specs),
        compiler_params=_params(("parallel", "arbitrary")), name=name,
    )(*arrs)


def _compress_body(k_ref, v_ref, pe_ref, w1_ref, w2_ref, o_ref):
    n = k_ref.shape[0]
    hid = []
    for idx, ref in enumerate((k_ref, v_ref)):
        ch = ref[...].astype(F32)
        a = jnp.dot((ch + pe_ref[idx, 0]).astype(BF16), w1_ref[idx, 0], preferred_element_type=F32)
        b = jnp.dot((ch + pe_ref[idx, 1]).astype(BF16), w1_ref[idx, 1], preferred_element_type=F32)
        pre = a + pltpu.roll(b, n - 1, 0)
        hid.append((pre * jax.nn.sigmoid(pre)).astype(BF16))
    o_ref[...] = (jnp.dot(hid[0], w2_ref[0], preferred_element_type=F32)
                  + jnp.dot(hid[1], w2_ref[1], preferred_element_type=F32)).astype(o_ref.dtype)


def _compress(kch, vch, pe, w1, w2):
    bsz, n, w = kch.shape
    full = lambda *shape: pl.BlockSpec(shape, lambda b: (0,) * len(shape))
    blk = pl.BlockSpec((None, n, w), lambda b: (b, 0, 0))
    return pl.pallas_call(
        _compress_body,
        out_shape=jax.ShapeDtypeStruct((bsz, n, 2 * LANES), BF16),
        grid=(bsz,),
        in_specs=[blk, blk, full(2, 2, 1, w), full(2, 2, w, NSA_CMP_HIDDEN), full(2, NSA_CMP_HIDDEN, 2 * LANES)],
        out_specs=pl.BlockSpec((None, n, 2 * LANES), lambda b: (b, 0, 0)),
        compiler_params=_params(("parallel",)),
        name="nsa_compress",
    )(kch, vch, pe, w1, w2)


def _cmp_select_body(q_ref, kv_ref, ov_ref, o_ref, sb_ref, used_ref, *, tq, slopes, n_top):
    i = pl.program_id(1)
    t0 = i * tq
    n_cmp = kv_ref.shape[0]
    tpos = t0 + lax.broadcasted_iota(jnp.int32, (tq, 1), 0)
    cend = NSA_CMP_STRIDE * lax.broadcasted_iota(jnp.int32, (1, n_cmp), 1) + (NSA_CMP_LEN - 1)
    dist = tpos - cend
    mask = dist >= 0
    distf = dist.astype(F32)
    kv = kv_ref[...]
    s_all = _dot_nt(_stack_wide_q(q_ref), kv[:, :LANES])
    psum = jnp.zeros((tq, n_cmp), F32)
    ps = []
    for h in range(4):
        s = jnp.where(mask, s_all[h * tq:(h + 1) * tq] - (float(slopes[h]) * LOG2E) * distf, -jnp.inf)
        m = jnp.max(s, axis=-1, keepdims=True)
        m = jnp.where(m == -jnp.inf, 0.0, m)
        e = jnp.where(mask, jnp.exp2(s - m), 0.0)
        p = e / jnp.maximum(jnp.sum(e, axis=-1, keepdims=True), 1e-30)
        ps.append(p.astype(BF16))
        psum = psum + p
    o = jnp.dot(jnp.concatenate(ps, axis=0), kv, preferred_element_type=F32)
    o_ref[...] = _unstack_wide_o(o, tq).astype(o_ref.dtype)

    imp = lax.dot_general(ov_ref[...], psum, (((1,), (1,)), ((), ())), precision=lax.Precision.HIGHEST,
                          preferred_element_type=F32)
    j = lax.broadcasted_iota(jnp.int32, (HEAD_DIM, tq), 0)
    tl = t0 + lax.broadcasted_iota(jnp.int32, (HEAD_DIM, tq), 1)
    cur = lax.shift_right_logical(tl, 6)
    causal = j * NSA_SEL_LEN <= tl
    forced = (j == 0) | (j == cur) | (j == cur - 1)
    score = jnp.where(causal, jnp.where(forced, jnp.inf, imp), -jnp.inf)
    rank = jnp.zeros((HEAD_DIM, tq), F32)
    for jp in range(HEAD_DIM):
        other = score[jp:jp + 1, :]
        rank = rank + jnp.where(other > score, 1.0, jnp.where((other == score) & (j > jp), 1.0, 0.0))
    keep = causal & (rank < float(n_top))
    bias_t = jnp.where(keep, 0.0, MASKED)
    sb_ref[...] = jnp.concatenate([jnp.zeros((HEAD_DIM, tq), F32), bias_t], axis=0).T.astype(sb_ref.dtype)
    used = jnp.max(jnp.where(keep, 1.0, 0.0), axis=1, keepdims=True)
    used_ref[...] = jnp.broadcast_to(used, (HEAD_DIM, LANES)).astype(jnp.int32)


def _cmp_select(zm, kv_c, overlap_t, tq, slopes, n_top):
    bsz, s, _ = zm.shape
    n_cmp = kv_c.shape[1]
    return pl.pallas_call(
        functools.partial(_cmp_select_body, tq=tq, slopes=tuple(float(x) for x in slopes), n_top=n_top),
        out_shape=(jax.ShapeDtypeStruct((bsz, s, 2 * LANES), BF16), jax.ShapeDtypeStruct((bsz, s, LANES), BF16),
                   jax.ShapeDtypeStruct((bsz, s // tq, HEAD_DIM, LANES), jnp.int32)),
        grid=(bsz, s // tq),
        in_specs=[pl.BlockSpec((None, tq, 4 * LANES), lambda b, i: (b, i, U_NSA_Q // 4)),
                  pl.BlockSpec((None, n_cmp, 2 * LANES), lambda b, i: (b, 0, 0)),
                  pl.BlockSpec((HEAD_DIM, n_cmp), lambda b, i: (0, 0))],
        out_specs=(pl.BlockSpec((None, tq, 2 * LANES), lambda b, i: (b, i, 0)),
                   pl.BlockSpec((None, tq, LANES), lambda b, i: (b, i, 0)),
                   pl.BlockSpec((None, None, HEAD_DIM, LANES), lambda b, i: (b, i, 0, 0))),
        compiler_params=_params(("parallel", "parallel")),
        name="nsa_cmp_select",
    )(zm, kv_c, overlap_t)


def _sel_body(used_ref, q_ref, sb_ref, kv_ref, o_ref, kp_ref, qs_ref, lst_ref, s_a, s_b, p_ref, *stats, t, slopes):
    b = pl.program_id(0)
    i = pl.program_id(1)
    s_len = kv_ref.shape[0]

    @pl.when(i == 0)
    def _():
        pos = lax.broadcasted_iota(jnp.int32, (s_len, LANES), 0)
        ln = lax.broadcasted_iota(jnp.int32, (s_len, LANES), 1)
        onehot = jnp.where(lax.shift_right_logical(pos, 6) == ln - HEAD_DIM, 1.0, 0.0).astype(BF16)
        kp_ref[...] = jnp.where(ln < HEAD_DIM, kv_ref[:, :LANES], onehot)

    per = t // NSA_SEL_LEN
    n = jnp.int32(0)
    for c in range(s_len // t - 1):
        any_used = used_ref[b, i, c * per]
        for k in range(1, per):
            any_used = any_used | used_ref[b, i, c * per + k]
        lst_ref[n] = jnp.int32(c)
        n = n + jnp.where((c < i) & (any_used != 0), 1, 0)
    lst_ref[n] = i

    lane = lax.broadcasted_iota(jnp.int32, (t, LANES), 1)
    sb = sb_ref[...]
    for h in range(4):
        qs_ref[h * t:(h + 1) * t, :] = jnp.where(lane < HEAD_DIM, q_ref[:, h * LANES:(h + 1) * LANES], sb)
    krel0 = lax.broadcasted_iota(jnp.int32, (1, t), 1)

    def rows(j):
        return pl.ds(pl.multiple_of(j * t, t), t)

    def produce(s_ref, pos):
        j = lst_ref[pos]
        s = _dot_nt(qs_ref[...], kp_ref[rows(j), :])
        krel = (krel0 + (j - i) * t).astype(F32)
        for h in range(4):
            s_ref[h * t:(h + 1) * t, :] = s[h * t:(h + 1) * t] + (float(slopes[h]) * LOG2E) * krel

    def consume(s_ref, pos, diag, slot):
        _flash_consume(s_ref, p_ref.at[slot], stats, kv_ref[rows(lst_ref[pos]), :], t, diag)

    o = _flash_causal(n, produce, consume, s_a, s_b, stats)
    o_ref[...] = _unstack_wide_o(o, t).astype(o_ref.dtype)


def _sel(zm, sbias, used, t, slopes):
    bsz, s, _ = zm.shape
    grid_spec = pltpu.PrefetchScalarGridSpec(
        num_scalar_prefetch=1,
        grid=(bsz, s // t),
        in_specs=[pl.BlockSpec((None, t, 4 * LANES), lambda b, i, u: (b, i, U_NSA_Q // 4)),
                  pl.BlockSpec((None, t, LANES), lambda b, i, u: (b, i, 0)),
                  pl.BlockSpec((None, s, 2 * LANES), lambda b, i, u: (b, 0, U_SLC // 2))],
        out_specs=pl.BlockSpec((None, t, 2 * LANES), lambda b, i, u: (b, i, 0)),
        scratch_shapes=[pltpu.VMEM((s, LANES), BF16), pltpu.VMEM((4 * t, LANES), BF16),
                        pltpu.SMEM((s // t + 1,), jnp.int32)] + _flash_scratch(t),
    )
    return pl.pallas_call(
        functools.partial(_sel_body, t=t, slopes=tuple(float(x) for x in slopes)),
        out_shape=jax.ShapeDtypeStruct((bsz, s, 2 * LANES), BF16),
        grid_spec=grid_spec,
        compiler_params=_params(("parallel", "arbitrary"), VMEM_LIMIT),
        name="nsa_sel_attn",
    )(used, zm, sbias, zm)


def _spread_heads(vals, lane0, stride, tm):
    lane = lax.broadcasted_iota(jnp.int32, (tm, LANES), 1)
    col = lambda h: vals[:, lane0 + stride * h:lane0 + stride * h + 1]
    halves = [jnp.where(lane < HEAD_DIM, col(2 * p), col(2 * p + 1)) for p in range(2)]
    return jnp.concatenate(halves, axis=1)


def _outproj_body(x_ref, ofox_ref, ocmp_ref, oslc_ref, owin_ref, zs_ref, oswa_ref, lswa_ref, sink_ref,
                  od1_ref, od4_ref, od16_ref, l1_ref, l4_ref, l16_ref, eg_ref, w_ref, o_ref, o_scr, l_scr):
    tm = x_ref.shape[0]

    def natural(ref, scr):
        dd, _, w = ref.shape
        if dd == 1:
            return ref[0].astype(F32)
        for u in range(w // LANES):
            for r in range(dd):
                scr[u, pl.ds(r, tm // dd, stride=dd), :] = ref[r, :, u * LANES:(u + 1) * LANES].astype(F32)
        return jnp.concatenate([scr[u] for u in range(w // LANES)], axis=1)

    gate = jax.nn.sigmoid(zs_ref[...])
    g_hi = gate.astype(BF16)
    g_lo = (gate - g_hi.astype(F32)).astype(BF16)
    spread = lambda br: (jnp.dot(g_hi, eg_ref[br], preferred_element_type=F32)
                         + jnp.dot(g_lo, eg_ref[br], preferred_element_type=F32))
    o_nsa = (spread(0) * ocmp_ref[...].astype(F32) + spread(1) * oslc_ref[...].astype(F32)
             + spread(2) * owin_ref[...].astype(F32))
    keep = jax.nn.sigmoid(lswa_ref[...] - sink_ref[...])
    o_swa = _spread_heads(keep, 0, 1, tm) * oswa_ref[...].astype(F32)
    l1, l4, l16 = natural(l1_ref, l_scr), natural(l4_ref, l_scr), natural(l16_ref, l_scr)
    m = jnp.maximum(jnp.maximum(l1, l4), l16)
    e1, e4, e16 = jnp.exp(l1 - m), jnp.exp(l4 - m), jnp.exp(l16 - m)
    inv = 1.0 / (e1 + e4 + e16)
    o_dil = _spread_heads(e1 * inv, 0, 1, tm) * natural(od1_ref, o_scr)
    o_dil = o_dil + _spread_heads(e4 * inv, 0, 1, tm) * natural(od4_ref, o_scr)
    o_dil = o_dil + _spread_heads(e16 * inv, 0, 1, tm) * natural(od16_ref, o_scr)
    y = x_ref[...]
    for g, o in enumerate((ofox_ref[...], o_nsa.astype(BF16), o_swa.astype(BF16), o_dil.astype(BF16))):
        y = y + jnp.dot(o, w_ref[g], preferred_element_type=F32)
    o_ref[...] = y


def _gate_spread_matrices():
    e = np.zeros((3, LANES, 2 * LANES), np.float32)
    for h in range(4):
        for br in range(3):
            e[br, GATE_LANE0 + 3 * h + br, HEAD_DIM * h:HEAD_DIM * (h + 1)] = 1.0
    return jnp.asarray(e, BF16)


def _outproj(x, heads, zs, lses, sinks, w):
    bsz, s, d = x.shape
    tm = min(512, s)
    row = lambda b, i: (b, i, 0)
    wide = pl.BlockSpec((None, tm, 2 * LANES), row)
    narrow = pl.BlockSpec((None, tm, LANES), row)

    def strided(a):
        dd = a.shape[1]
        return pl.BlockSpec((None, dd, tm // dd, a.shape[-1]), lambda b, i: (b, 0, i, 0))

    ofox, ocmp, oslc, owin, oswa, od1, od4, od16 = heads
    lswa, l1, l4, l16 = lses
    return pl.pallas_call(
        _outproj_body,
        out_shape=jax.ShapeDtypeStruct((bsz, s, d), F32),
        grid=(bsz, s // tm),
        in_specs=[pl.BlockSpec((None, tm, d), row), wide, wide, wide, wide, narrow, wide, narrow,
                  pl.BlockSpec((1, LANES), lambda b, i: (0, 0)), strided(od1), strided(od4), strided(od16),
                  strided(l1), strided(l4), strided(l16),
                  pl.BlockSpec((3, LANES, 2 * LANES), lambda b, i: (0, 0, 0)),
                  pl.BlockSpec((4, 2 * LANES, d), lambda b, i: (0, 0, 0))],
        out_specs=pl.BlockSpec((None, tm, d), row),
        scratch_shapes=[pltpu.VMEM((2, tm, LANES), F32), pltpu.VMEM((1, tm, LANES), F32)],
        compiler_params=_params(("parallel", "parallel"), VMEM_LIMIT),
        name="outproj",
    )(x, ofox, ocmp, oslc, owin, zs, oswa, lswa, sinks, od1, od4, od16, l1, l4, l16, _gate_spread_matrices(), w)


def _overlap_t(n_cmp, n_sel):
    cs = np.arange(n_cmp) * NSA_CMP_STRIDE
    ss = np.arange(n_sel) * NSA_SEL_LEN
    ov = (cs[None, :] <= ss[:, None] + NSA_SEL_LEN - 1) & (cs[None, :] + NSA_CMP_LEN - 1 >= ss[:, None])
    out = np.zeros((HEAD_DIM, n_cmp), np.float32)
    out[:n_sel] = ov
    return jnp.asarray(out)


def _mixer(x, g_mix, w_in, fox_b_f, cmp_pe, cmp_w1, cmp_w2, swa_sinks, w_out):
    bsz, s, d = x.shape
    sl_swa, sl_nsa, sl_dil = _alibi_slopes()
    cols, scale = _w_in_layout()
    w_perm = _relayout_w_in(w_in, cols, scale)
    zm, zs, *zds = _inproj(x, g_mix, w_perm)

    t_fox = min(512, s)
    f_rows = jnp.transpose(zs[:, :, :8], (0, 2, 1))
    b_rows = jnp.pad(fox_b_f.reshape(4, 1), ((0, 4), (0, 0)))
    c = _fox_cumsum(f_rows, b_rows, t_fox)
    o_fox = _fox(zm, c, t_fox)

    n_chunk = s // NSA_CMP_STRIDE
    n_sel = s // NSA_SEL_LEN
    assert n_sel <= HEAD_DIM
    kvc = zm[:, :, U_CMP * LANES:(U_CMP + 1) * LANES]
    kch = kvc[:, :, :HEAD_DIM].reshape(bsz, n_chunk, NSA_CMP_STRIDE * HEAD_DIM)
    vch = kvc[:, :, HEAD_DIM:].reshape(bsz, n_chunk, NSA_CMP_STRIDE * HEAD_DIM)
    pe = cmp_pe.reshape(2, 2, 1, NSA_CMP_STRIDE * HEAD_DIM)
    w1 = cmp_w1.reshape(2, 2, NSA_CMP_STRIDE * HEAD_DIM, NSA_CMP_HIDDEN).astype(BF16)
    zpad = jnp.zeros((NSA_CMP_HIDDEN, HEAD_DIM), F32)
    w2 = jnp.stack([jnp.concatenate([cmp_w2[0], zpad, zpad, cmp_w2[0]], axis=1),
                    jnp.concatenate([zpad, cmp_w2[1], cmp_w2[1], zpad], axis=1)]).astype(BF16)
    kv_c = _compress(kch, vch, pe, w1, w2)
    t_sel = min(512, s)
    o_cmp, sbias, used = _cmp_select(zm, kv_c, _overlap_t(n_chunk, n_sel), t_sel, sl_nsa, min(NSA_TOPN, n_sel))
    o_slc = _sel(zm, sbias, used[:, :, :, 0], t_sel, sl_nsa)
    (o_win,) = _banded((zm, zm), (U_NSA_Q // 4, U_WIN // 2), sl_nsa, NSA_WINDOW - 1, NSA_WINDOW,
                       True, False, "nsa_win_attn")

    o_swa, l_swa = _banded((zm, zm, zm), (U_SWA_Q // 2, U_SWA_K // 2, U_SWA_V // 2), sl_swa, SWA_WINDOW - 1,
                           SWA_WINDOW, False, True, "swa_attn")

    o_dil, l_dil = [], []
    for (window, dd), zd in zip(DIL_PAIRS, zds):
        zz = zd.reshape(bsz * dd, s // dd, N_DIL * LANES)
        o, l = _banded((zz, zz, zz), (0, 1, 2), sl_dil * dd, window // dd, LANES, False, True, "dil%d_attn" % dd)
        o_dil.append(o.reshape(bsz, dd, s // dd, 2 * LANES))
        l_dil.append(l.reshape(bsz, dd, s // dd, LANES))

    sinks = jnp.pad(swa_sinks.reshape(1, 4), ((0, 0), (0, LANES - 4)))
    heads = (o_fox, o_cmp, o_slc, o_win, o_swa, *o_dil)
    return _outproj(x, heads, zs, (l_swa, *l_dil), sinks, w_out.reshape(4, 2 * LANES, d).astype(BF16))


def kernel(x, norm_ffn1, ffn1_w_gate, ffn1_w_up, ffn1_w_down, norm_mix, w_in, fox_b_f, nsa_cmp_pe, nsa_cmp_w1,
           nsa_cmp_w2, swa_sinks, w_out, norm_ffn2, ffn2_w_gate, ffn2_w_up, ffn2_w_down, norm_final):
    bsz, s, d = x.shape
    depth = norm_ffn1.shape[0]
    h = x.reshape(bsz * s, d)
    for l in range(depth):
        h = _ffn(h, norm_ffn1[l], ffn1_w_gate[l].astype(BF16), ffn1_w_up[l].astype(BF16), ffn1_w_down[l].astype(BF16))
        h = _mixer(h.reshape(bsz, s, d), norm_mix[l], w_in[l], fox_b_f[l], nsa_cmp_pe[l], nsa_cmp_w1[l],
                   nsa_cmp_w2[l], swa_sinks[l], w_out[l]).reshape(bsz * s, d)
        h = _ffn(h, norm_ffn2[l], ffn2_w_gate[l].astype(BF16), ffn2_w_up[l].astype(BF16), ffn2_w_down[l].astype(BF16),
                 norm_final if l == depth - 1 else None)
    return h.reshape(bsz, s, d)
```

```python
import functools

import numpy as np
import jax
import jax.numpy as jnp
from jax import lax
from jax.experimental import pallas as pl
from jax.experimental.pallas import tpu as pltpu

F32 = jnp.float32
BF16 = jnp.bfloat16

HEAD_DIM = 64
LANES = 128
NSA_CMP_LEN = 32
NSA_CMP_STRIDE = 16
NSA_CMP_HIDDEN = 128
NSA_SEL_LEN = 64
NSA_TOPN = 16
NSA_WINDOW = 512
SWA_WINDOW = 128
DIL_PAIRS = ((128, 1), (512, 4), (2048, 16))
RMS_EPS = 1e-6
LOG2E = 1.4426950408889634
LN2 = 0.6931471805599453
MASKED = -1e30
VMEM_LIMIT = 56 * 1024 * 1024
ROW_TILE = 512

IN_SPLITS = (
    ('fox_q', 256), ('fox_k', 256), ('fox_v', 256), ('fox_f', 4),
    ('nsa_q', 256), ('nsa_k_cmp', 64), ('nsa_v_cmp', 64), ('nsa_k_slc', 64), ('nsa_v_slc', 64),
    ('nsa_k_win', 64), ('nsa_v_win', 64), ('nsa_gate', 12),
    ('swa_q', 256), ('swa_k', 128), ('swa_v', 128),
    ('dil_q', 256), ('dil_k', 256), ('dil_v', 256),
)

U_NSA_Q = 0
U_FOX_Q, U_FOX_K, U_FOX_V = 4, 6, 8
U_SWA_Q, U_SWA_K, U_SWA_V = 10, 12, 14
U_SLC = 16
U_WIN = 18
U_CMP = 20
N_MAIN = 21
N_DIL = 6
N_UNITS = N_MAIN + N_DIL + 1
GATE_LANE0 = 4


def _alibi_slopes():
    n = 12
    s = 2.0 ** (-8.0 * np.arange(1, n + 1) / n)
    return s[:4], s[4:8], s[8:]


def _w_in_layout():
    off, o = {}, 0
    for name, w in IN_SPLITS:
        off[name] = o
        o += w
    cols, scale = [], []

    def seg(name, start, width, sc=1.0):
        cols.extend(range(off[name] + start, off[name] + start + width))
        scale.extend([sc] * width)

    def zeros(n):
        cols.extend([0] * n)
        scale.extend([0.0] * n)

    qs = HEAD_DIM ** -0.5 * LOG2E
    for h in range(4):
        seg('nsa_q', 64 * h, 64, qs)
        zeros(64)
    seg('fox_q', 0, 256, qs)
    seg('fox_k', 0, 256)
    seg('fox_v', 0, 256)
    seg('swa_q', 0, 256, qs)
    for name in ('swa_k', 'swa_v'):
        for kv in range(2):
            seg(name, 64 * kv, 64)
            seg(name, 64 * kv, 64)
    for a, b in (('nsa_k_slc', 'nsa_v_slc'), ('nsa_v_slc', 'nsa_k_slc'),
                 ('nsa_k_win', 'nsa_v_win'), ('nsa_v_win', 'nsa_k_win')):
        seg(a, 0, 64)
        seg(b, 0, 64)
    seg('nsa_k_cmp', 0, 64)
    seg('nsa_v_cmp', 0, 64)
    seg('dil_q', 0, 256, qs)
    seg('dil_k', 0, 256)
    seg('dil_v', 0, 256)
    seg('fox_f', 0, 4)
    seg('nsa_gate', 0, 12)
    zeros(LANES - 16)
    assert len(cols) == N_UNITS * LANES
    return np.asarray(cols, np.int32), np.asarray(scale, np.float32)


def _relayout_w_in(w_in, cols, scale):
    runs, start = [], 0
    for c in range(1, len(cols) + 1):
        same = c < len(cols) and scale[c] == scale[c - 1] and (scale[c] == 0.0 or cols[c] == cols[c - 1] + 1)
        if not same:
            runs.append((start, c))
            start = c
    parts = []
    for a, b in runs:
        if scale[a] == 0.0:
            parts.append(jnp.zeros((w_in.shape[0], b - a), BF16))
        else:
            parts.append((w_in[:, int(cols[a]):int(cols[a]) + (b - a)] * float(scale[a])).astype(BF16))
    return jnp.concatenate(parts, axis=1)


def _rms(x, g):
    ms = jnp.mean(x * x, axis=-1, keepdims=True)
    return x * lax.rsqrt(ms + RMS_EPS) * g


def _dot_nt(a, b):
    return lax.dot_general(a, b, (((1,), (1,)), ((), ())), preferred_element_type=F32)


def _params(sem, vmem=None):
    return pltpu.CompilerParams(dimension_semantics=sem, vmem_limit_bytes=vmem)


def _stack_packed_q(q):
    slot = lax.shift_right_logical(lax.broadcasted_iota(jnp.int32, q.shape, 1), 6)
    return jnp.concatenate([jnp.where(slot == h, q, jnp.zeros_like(q)) for h in range(4)], axis=0)


def _unstack_packed_o(o, t):
    slot = lax.shift_right_logical(lax.broadcasted_iota(jnp.int32, (t, 2 * LANES), 1), 6)
    out = o[:t]
    for h in range(1, 4):
        out = jnp.where(slot == h, o[h * t:(h + 1) * t], out)
    return out


def _stack_wide_q(q_ref):
    return jnp.concatenate([q_ref[:, h * LANES:(h + 1) * LANES] for h in range(4)], axis=0)


def _unstack_wide_o(o, t):
    lane = lax.broadcasted_iota(jnp.int32, (t, LANES), 1)
    pairs = [jnp.where(lane < HEAD_DIM, o[2 * pr * t:(2 * pr + 1) * t, LANES:], o[(2 * pr + 1) * t:(2 * pr + 2) * t, :LANES])
             for pr in range(2)]
    return jnp.concatenate(pairs, axis=1)


FLASH_ROW_BLOCK = 64


def _flash_consume(s_ref, p_ref, stats, v, t, diag):
    m_ref, alpha_ref, l_ref, acc_ref = stats
    n = s_ref.shape[1]
    rb = FLASH_ROW_BLOCK
    lanes = [slice(c * LANES, (c + 1) * LANES) for c in range(n // LANES)]

    def scores(r0):
        s = s_ref[r0:r0 + rb, :]
        if diag:
            row = lax.broadcasted_iota(jnp.int32, (rb, n), 0) + r0 % t
            col = lax.broadcasted_iota(jnp.int32, (rb, n), 1)
            s = jnp.where(col <= row, s, MASKED)
        return s

    for r0 in range(0, 4 * t, rb):
        s = scores(r0)
        mx = s[:, lanes[0]]
        for c in lanes[1:]:
            mx = jnp.maximum(mx, s[:, c])
        m_old = m_ref[r0:r0 + rb, :]
        m_new = jnp.maximum(m_old, jnp.max(mx, axis=-1, keepdims=True))
        alpha_ref[r0:r0 + rb, :] = jnp.exp2(m_old - m_new)
        m_ref[r0:r0 + rb, :] = m_new
    for r0 in range(0, 4 * t, rb):
        s = scores(r0)
        m_new = m_ref[r0:r0 + rb, :]
        psum = jnp.zeros((rb, LANES), F32)
        for c in lanes:
            p = jnp.exp2(s[:, c] - m_new)
            psum = psum + p
            p_ref[r0:r0 + rb, c] = p.astype(BF16)
        l_ref[r0:r0 + rb, :] = alpha_ref[r0:r0 + rb, :] * l_ref[r0:r0 + rb, :] + psum
    alpha = alpha_ref[...]
    acc_ref[...] = (jnp.concatenate([alpha, alpha], axis=1) * acc_ref[...]
                    + jnp.dot(p_ref[...], v, preferred_element_type=F32))


def _flash_causal(i, produce, consume, s_a, s_b, stats):
    m_ref, alpha_ref, l_ref, acc_ref = stats
    m_ref[...] = jnp.full(m_ref.shape, MASKED, F32)
    l_ref[...] = jnp.zeros(l_ref.shape, F32)
    acc_ref[...] = jnp.zeros(acc_ref.shape, F32)
    produce(s_a, 0)

    def pair(jj, carry):
        j = 2 * jj
        produce(s_b, j + 1)
        consume(s_a, j, False, 0)
        produce(s_a, j + 2)
        consume(s_b, j + 1, False, 1)
        return carry

    lax.fori_loop(0, i // 2, pair, 0)

    @pl.when(i % 2 == 1)
    def _():
        produce(s_b, i)
        consume(s_a, i - 1, False, 0)
        consume(s_b, i, True, 1)

    @pl.when(i % 2 == 0)
    def _():
        consume(s_a, i, True, 0)

    return acc_ref[...] * (1.0 / jnp.sum(l_ref[...], axis=-1, keepdims=True))


def _flash_scratch(t):
    return [pltpu.VMEM((4 * t, t), F32), pltpu.VMEM((4 * t, t), F32), pltpu.VMEM((2, 4 * t, t), BF16),
            pltpu.VMEM((4 * t, LANES), F32), pltpu.VMEM((4 * t, LANES), F32), pltpu.VMEM((4 * t, LANES), F32),
            pltpu.VMEM((4 * t, 2 * LANES), F32)]


def _ffn_body(x_ref, g_ref, wg_ref, wu_ref, wd_ref, *rest, final):
    o_ref = rest[-1]
    x = x_ref[...]
    xn = _rms(x, g_ref[...]).astype(BF16)
    a = jnp.dot(xn, wg_ref[...], preferred_element_type=F32)
    b = jnp.dot(xn, wu_ref[...], preferred_element_type=F32)
    h = (a * jax.nn.sigmoid(a) * b).astype(BF16)
    y = x + 0.5 * jnp.dot(h, wd_ref[...], preferred_element_type=F32)
    if final:
        y = _rms(y, rest[0][...])
    o_ref[...] = y


def _ffn(x, g, wg, wu, wd, g_final=None):
    n, d = x.shape
    f = wg.shape[1]
    tm = ROW_TILE if n % ROW_TILE == 0 else n
    const = lambda i: (0, 0)
    once = pl.Buffered(1)
    in_specs = [
        pl.BlockSpec((tm, d), lambda i: (i, 0)),
        pl.BlockSpec((1, d), const),
        pl.BlockSpec((d, f), const, pipeline_mode=once),
        pl.BlockSpec((d, f), const, pipeline_mode=once),
        pl.BlockSpec((f, d), const, pipeline_mode=once),
    ]
    args = [x, g.reshape(1, d), wg, wu, wd]
    if g_final is not None:
        in_specs.append(pl.BlockSpec((1, d), const))
        args.append(g_final.reshape(1, d))
    return pl.pallas_call(
        functools.partial(_ffn_body, final=g_final is not None),
        out_shape=jax.ShapeDtypeStruct((n, d), F32),
        grid=(n // tm,),
        in_specs=in_specs,
        out_specs=pl.BlockSpec((tm, d), lambda i: (i, 0)),
        compiler_params=_params(("parallel",), VMEM_LIMIT),
        name="ffn",
    )(*args)


def _inproj_body(x_ref, g_ref, w_ref, zm_ref, zs_ref, *rest):
    zd_refs, zscr = rest[:-1], rest[-1]
    tm = x_ref.shape[0]
    xn = _rms(x_ref[...], g_ref[...]).astype(BF16)
    z = jnp.dot(xn, w_ref[...], preferred_element_type=F32)
    zm_ref[...] = z[:, :N_MAIN * LANES].astype(BF16)
    zs_ref[...] = z[:, (N_MAIN + N_DIL) * LANES:]
    for u in range(N_DIL):
        zscr[u] = z[:, (N_MAIN + u) * LANES:(N_MAIN + u + 1) * LANES]
    for (_, dd), ref in zip(DIL_PAIRS, zd_refs):
        for u in range(N_DIL):
            if dd == 1:
                ref[0, :, u * LANES:(u + 1) * LANES] = zscr[u].astype(BF16)
            else:
                for r in range(dd):
                    ref[r, :, u * LANES:(u + 1) * LANES] = zscr[u, pl.ds(r, tm // dd, stride=dd), :].astype(BF16)


def _inproj(x, g, w):
    bsz, s, d = x.shape
    tm = min(ROW_TILE, s)
    const = lambda b, i: (0, 0)
    row = lambda b, i: (b, i, 0)
    wd = N_DIL * LANES
    zd_shapes = tuple(jax.ShapeDtypeStruct((bsz, dd, s // dd, wd), BF16) for _, dd in DIL_PAIRS)
    zd_specs = tuple(pl.BlockSpec((None, dd, tm // dd, wd), lambda b, i: (b, 0, i, 0)) for _, dd in DIL_PAIRS)
    return pl.pallas_call(
        _inproj_body,
        out_shape=(jax.ShapeDtypeStruct((bsz, s, N_MAIN * LANES), BF16), jax.ShapeDtypeStruct((bsz, s, LANES), F32))
        + zd_shapes,
        grid=(bsz, s // tm),
        in_specs=[pl.BlockSpec((None, tm, d), row), pl.BlockSpec((1, d), const),
                  pl.BlockSpec((d, N_UNITS * LANES), const, pipeline_mode=pl.Buffered(1))],
        out_specs=(pl.BlockSpec((None, tm, N_MAIN * LANES), row), pl.BlockSpec((None, tm, LANES), row)) + zd_specs,
        scratch_shapes=[pltpu.VMEM((N_DIL, tm, LANES), F32)],
        compiler_params=_params(("parallel", "parallel"), VMEM_LIMIT),
        name="inproj",
    )(x, g.reshape(1, d), w)


def _cumsum_body(x_ref, b_ref, u_ref, c_ref, *, tk):
    x = x_ref[...] + b_ref[...]
    lf = (jnp.minimum(x, 0.0) - jnp.log(1.0 + jnp.exp(-jnp.abs(x)))) * LOG2E
    u = u_ref[...]
    per = tk // LANES
    local = [jnp.dot(lf[:, j * LANES:(j + 1) * LANES], u, precision=lax.Precision.HIGHEST, preferred_element_type=F32)
             for j in range(x.shape[1] // LANES)]
    carry = jnp.zeros((8, 1), F32)
    for j, cs in enumerate(local):
        c_ref[j // per, :, (j % per) * LANES:(j % per + 1) * LANES] = cs + carry
        carry = carry + cs[:, LANES - 1:LANES]


def _fox_cumsum(f_rows, b_rows, tk):
    bsz, _, s = f_rows.shape
    u = jnp.asarray(np.triu(np.ones((LANES, LANES), np.float32)))
    return pl.pallas_call(
        functools.partial(_cumsum_body, tk=tk),
        out_shape=jax.ShapeDtypeStruct((bsz, s // tk, 8, tk), F32),
        grid=(bsz,),
        in_specs=[pl.BlockSpec((None, 8, s), lambda b: (b, 0, 0)),
                  pl.BlockSpec((8, 1), lambda b: (0, 0)),
                  pl.BlockSpec((LANES, LANES), lambda b: (0, 0))],
        out_specs=pl.BlockSpec((None, s // tk, 8, tk), lambda b: (b, 0, 0, 0)),
        compiler_params=_params(("parallel",)),
        name="fox_cumsum",
    )(f_rows, b_rows, u)


def _fox_body(q_ref, k_ref, v_ref, c_ref, o_ref, qs_ref, s_a, s_b, p_ref, *stats, t):
    i = pl.program_id(1)
    qs_ref[...] = _stack_packed_q(q_ref[...])

    def rows(j):
        return pl.ds(pl.multiple_of(j * t, t), t)

    def produce(s_ref, j):
        s = _dot_nt(qs_ref[...], k_ref[rows(j), :])
        for h in range(4):
            s_ref[h * t:(h + 1) * t, :] = s[h * t:(h + 1) * t] - c_ref[j, h:h + 1, :]

    def consume(s_ref, j, diag, slot):
        _flash_consume(s_ref, p_ref.at[slot], stats, v_ref[rows(j), :], t, diag)

    o = _flash_causal(i, produce, consume, s_a, s_b, stats)
    o_ref[...] = _unstack_packed_o(o, t).astype(o_ref.dtype)


def _fox(zm, c, t):
    bsz, s, _ = zm.shape
    return pl.pallas_call(
        functools.partial(_fox_body, t=t),
        out_shape=jax.ShapeDtypeStruct((bsz, s, 2 * LANES), BF16),
        grid=(bsz, s // t),
        in_specs=[pl.BlockSpec((None, t, 2 * LANES), lambda b, i: (b, i, U_FOX_Q // 2)),
                  pl.BlockSpec((None, s, 2 * LANES), lambda b, i: (b, 0, U_FOX_K // 2)),
                  pl.BlockSpec((None, s, 2 * LANES), lambda b, i: (b, 0, U_FOX_V // 2)),
                  pl.BlockSpec((None, s // t, 8, t), lambda b, i: (b, 0, 0, 0))],
        out_specs=pl.BlockSpec((None, t, 2 * LANES), lambda b, i: (b, i, 0)),
        scratch_shapes=[pltpu.VMEM((4 * t, 2 * LANES), BF16)] + _flash_scratch(t),
        compiler_params=_params(("parallel", "arbitrary"), VMEM_LIMIT),
        name="fox_attn",
    )(zm, zm, zm, c)


def _banded_body(*refs, gb, tq, sq, span, s_loc, wpad, max_dist, slopes, wide, with_lse):
    i = pl.program_id(1)
    o_ref = refs[2] if wide else refs[3]
    trel = lax.broadcasted_iota(jnp.int32, (sq, 1), 0).astype(F32)

    def band_bias(kpos_minus_t0):
        dist = lax.broadcasted_iota(jnp.int32, (sq, 1), 0) - kpos_minus_t0
        mask = lax.bitcast_convert_type(dist, jnp.uint32) <= jnp.uint32(max_dist)
        krel = kpos_minus_t0.astype(F32)
        return [jnp.where(mask, (float(slopes[h]) * LOG2E) * krel, MASKED) for h in range(4)]

    unclamped = [span == sq + wpad and r * sq >= wpad for r in range(tq // sq)]
    shared = band_bias(lax.broadcasted_iota(jnp.int32, (1, span), 1) - wpad) if any(unclamped) else None
    for g in range(gb):
        for r in range(tq // sq):
            t0 = i * tq + r * sq
            rows = slice(r * sq, (r + 1) * sq)
            if unclamped[r]:
                start = pl.multiple_of(t0 - wpad, LANES)
                bias = shared
            else:
                start = pl.multiple_of(jnp.clip(t0 - wpad, 0, s_loc - span), LANES)
                bias = band_bias(start - t0 + lax.broadcasted_iota(jnp.int32, (1, span), 1))
            if wide:
                q_ref, kv_ref = refs[:2]
                qs = jnp.concatenate([q_ref[g, rows, h * LANES:(h + 1) * LANES] for h in range(4)], axis=0)
                k = kv_ref[g, pl.ds(start, span), :LANES]
                v = kv_ref[g, pl.ds(start, span), :]
            else:
                q_ref, k_ref, v_ref = refs[:3]
                qs = _stack_packed_q(q_ref[g, rows, :])
                k = k_ref[g, pl.ds(start, span), :]
                v = v_ref[g, pl.ds(start, span), :]
            s = _dot_nt(qs, k)
            s = jnp.concatenate([s[h * sq:(h + 1) * sq] + bias[h] for h in range(4)], axis=0)
            m = jnp.max(s, axis=-1, keepdims=True)
            e = jnp.exp2(s - m)
            l = jnp.sum(e, axis=-1, keepdims=True)
            o = jnp.dot(e.astype(BF16), v, preferred_element_type=F32) * (1.0 / l)
            o_ref[g, rows, :] = (_unstack_wide_o(o, sq) if wide else _unstack_packed_o(o, sq)).astype(o_ref.dtype)
            if with_lse:
                lse = (m + jnp.log2(l)) * LN2
                lane = lax.broadcasted_iota(jnp.int32, (sq, LANES), 1)
                acc = jnp.zeros((sq, LANES), F32)
                for h in range(4):
                    acc = jnp.where(lane == h, lse[h * sq:(h + 1) * sq] - float(slopes[h]) * trel, acc)
                refs[-1][g, rows, :] = acc


BANDED_ROWS = 1024


def _banded(arrs, units, slopes, max_dist, wpad, wide, with_lse, name):
    g, s_loc, _ = arrs[0].shape
    tq = min(BANDED_ROWS, s_loc)
    gb = min(g, BANDED_ROWS // tq)
    sq = min(tq, wpad, 256)
    span = min(sq + wpad, s_loc)
    if wide:
        in_specs = [pl.BlockSpec((gb, tq, 4 * LANES), lambda b, i: (b, i, units[0])),
                    pl.BlockSpec((gb, s_loc, 2 * LANES), lambda b, i: (b, 0, units[1]))]
    else:
        in_specs = [pl.BlockSpec((gb, tq, 2 * LANES), lambda b, i: (b, i, units[0])),
                    pl.BlockSpec((gb, s_loc, 2 * LANES), lambda b, i: (b, 0, units[1])),
                    pl.BlockSpec((gb, s_loc, 2 * LANES), lambda b, i: (b, 0, units[2]))]
    out_shape = [jax.ShapeDtypeStruct((g, s_loc, 2 * LANES), BF16)]
    out_specs = [pl.BlockSpec((gb, tq, 2 * LANES), lambda b, i: (b, i, 0))]
    if with_lse:
        out_shape.append(jax.ShapeDtypeStruct((g, s_loc, LANES), F32))
        out_specs.append(pl.BlockSpec((gb, tq, LANES), lambda b, i: (b, i, 0)))
    body = functools.partial(_banded_body, gb=gb, tq=tq, sq=sq, span=span, s_loc=s_loc, wpad=wpad,
                             max_dist=max_dist, slopes=tuple(float(x) for x in slopes), wide=wide, with_lse=with_lse)
    return pl.pallas_call(
        body, out_shape=tuple(out_shape), grid=(g // gb, s_loc // tq), in_specs=in_specs, out_specs=tuple(out_specs),
        compiler_params=_params(("parallel", "arbitrary")), name=name,
    )(*arrs)


def _compress_body(k_ref, v_ref, pe_ref, w1_ref, w2_ref, o_ref):
    n = k_ref.shape[0]
    hid = []
    for idx, ref in enumerate((k_ref, v_ref)):
        ch = ref[...].astype(F32)
        a = jnp.dot((ch + pe_ref[idx, 0]).astype(BF16), w1_ref[idx, 0], preferred_element_type=F32)
        b = jnp.dot((ch + pe_ref[idx, 1]).astype(BF16), w1_ref[idx, 1], preferred_element_type=F32)
        pre = a + pltpu.roll(b, n - 1, 0)
        hid.append((pre * jax.nn.sigmoid(pre)).astype(BF16))
    o_ref[...] = (jnp.dot(hid[0], w2_ref[0], preferred_element_type=F32)
                  + jnp.dot(hid[1], w2_ref[1], preferred_element_type=F32)).astype(o_ref.dtype)


def _compress(kch, vch, pe, w1, w2):
    bsz, n, w = kch.shape
    full = lambda *shape: pl.BlockSpec(shape, lambda b: (0,) * len(shape))
    blk = pl.BlockSpec((None, n, w), lambda b: (b, 0, 0))
    return pl.pallas_call(
        _compress_body,
        out_shape=jax.ShapeDtypeStruct((bsz, n, 2 * LANES), BF16),
        grid=(bsz,),
        in_specs=[blk, blk, full(2, 2, 1, w), full(2, 2, w, NSA_CMP_HIDDEN), full(2, NSA_CMP_HIDDEN, 2 * LANES)],
        out_specs=pl.BlockSpec((None, n, 2 * LANES), lambda b: (b, 0, 0)),
        compiler_params=_params(("parallel",)),
        name="nsa_compress",
    )(kch, vch, pe, w1, w2)


def _cmp_select_body(q_ref, kv_ref, ov_ref, o_ref, sb_ref, used_ref, *, tq, slopes, n_top):
    i = pl.program_id(1)
    t0 = i * tq
    n_cmp = kv_ref.shape[0]
    tpos = t0 + lax.broadcasted_iota(jnp.int32, (tq, 1), 0)
    cend = NSA_CMP_STRIDE * lax.broadcasted_iota(jnp.int32, (1, n_cmp), 1) + (NSA_CMP_LEN - 1)
    dist = tpos - cend
    mask = dist >= 0
    distf = dist.astype(F32)
    kv = kv_ref[...]
    s_all = _dot_nt(_stack_wide_q(q_ref), kv[:, :LANES])
    psum = jnp.zeros((tq, n_cmp), F32)
    ps = []
    for h in range(4):
        s = jnp.where(mask, s_all[h * tq:(h + 1) * tq] - (float(slopes[h]) * LOG2E) * distf, -jnp.inf)
        m = jnp.max(s, axis=-1, keepdims=True)
        m = jnp.where(m == -jnp.inf, 0.0, m)
        e = jnp.exp2(s - m)
        p = e / jnp.maximum(jnp.sum(e, axis=-1, keepdims=True), 1e-30)
        ps.append(p.astype(BF16))
        psum = psum + p
    o = jnp.dot(jnp.concatenate(ps, axis=0), kv, preferred_element_type=F32)
    o_ref[...] = _unstack_wide_o(o, tq).astype(o_ref.dtype)

    imp = lax.dot_general(ov_ref[...], psum, (((1,), (1,)), ((), ())), precision=lax.Precision.HIGHEST,
                          preferred_element_type=F32)
    j = lax.broadcasted_iota(jnp.int32, (HEAD_DIM, tq), 0)
    tl = t0 + lax.broadcasted_iota(jnp.int32, (HEAD_DIM, tq), 1)
    cur = lax.shift_right_logical(tl, 6)
    causal = j * NSA_SEL_LEN <= tl
    forced = (j == 0) | (j == cur) | (j == cur - 1)
    score = jnp.where(causal, jnp.where(forced, jnp.inf, imp), -jnp.inf)
    rank = jnp.zeros((HEAD_DIM, tq), F32)
    for jp in range(HEAD_DIM):
        other = score[jp:jp + 1, :]
        rank = rank + jnp.where(other > score, 1.0, jnp.where((other == score) & (j > jp), 1.0, 0.0))
    keep = causal & (rank < float(n_top))
    bias_t = jnp.where(keep, 0.0, MASKED)
    sb_ref[...] = jnp.concatenate([jnp.zeros((HEAD_DIM, tq), F32), bias_t], axis=0).T.astype(sb_ref.dtype)
    used = jnp.max(jnp.where(keep, 1.0, 0.0), axis=1, keepdims=True)
    used_ref[...] = jnp.broadcast_to(used, (HEAD_DIM, LANES)).astype(jnp.int32)


def _cmp_select(zm, kv_c, overlap_t, tq, slopes, n_top):
    bsz, s, _ = zm.shape
    n_cmp = kv_c.shape[1]
    return pl.pallas_call(
        functools.partial(_cmp_select_body, tq=tq, slopes=tuple(float(x) for x in slopes), n_top=n_top),
        out_shape=(jax.ShapeDtypeStruct((bsz, s, 2 * LANES), BF16), jax.ShapeDtypeStruct((bsz, s, LANES), BF16),
                   jax.ShapeDtypeStruct((bsz, s // tq, HEAD_DIM, LANES), jnp.int32)),
        grid=(bsz, s // tq),
        in_specs=[pl.BlockSpec((None, tq, 4 * LANES), lambda b, i: (b, i, U_NSA_Q // 4)),
                  pl.BlockSpec((None, n_cmp, 2 * LANES), lambda b, i: (b, 0, 0)),
                  pl.BlockSpec((HEAD_DIM, n_cmp), lambda b, i: (0, 0))],
        out_specs=(pl.BlockSpec((None, tq, 2 * LANES), lambda b, i: (b, i, 0)),
                   pl.BlockSpec((None, tq, LANES), lambda b, i: (b, i, 0)),
                   pl.BlockSpec((None, None, HEAD_DIM, LANES), lambda b, i: (b, i, 0, 0))),
        compiler_params=_params(("parallel", "parallel")),
        name="nsa_cmp_select",
    )(zm, kv_c, overlap_t)


def _sel_body(used_ref, q_ref, sb_ref, kv_ref, o_ref, kp_ref, qs_ref, lst_ref, s_a, s_b, p_ref, *stats, t, slopes):
    b = pl.program_id(0)
    i = pl.program_id(1)
    s_len = kv_ref.shape[0]

    @pl.when(i == 0)
    def _():
        pos = lax.broadcasted_iota(jnp.int32, (s_len, LANES), 0)
        ln = lax.broadcasted_iota(jnp.int32, (s_len, LANES), 1)
        onehot = jnp.where(lax.shift_right_logical(pos, 6) == ln - HEAD_DIM, 1.0, 0.0).astype(BF16)
        kp_ref[...] = jnp.where(ln < HEAD_DIM, kv_ref[:, :LANES], onehot)

    per = t // NSA_SEL_LEN
    n = jnp.int32(0)
    for c in range(s_len // t - 1):
        any_used = used_ref[b, i, c * per]
        for k in range(1, per):
            any_used = any_used | used_ref[b, i, c * per + k]
        lst_ref[n] = jnp.int32(c)
        n = n + jnp.where((c < i) & (any_used != 0), 1, 0)
    lst_ref[n] = i

    lane = lax.broadcasted_iota(jnp.int32, (t, LANES), 1)
    sb = sb_ref[...]
    for h in range(4):
        qs_ref[h * t:(h + 1) * t, :] = jnp.where(lane < HEAD_DIM, q_ref[:, h * LANES:(h + 1) * LANES], sb)
    krel0 = lax.broadcasted_iota(jnp.int32, (1, t), 1)

    def rows(j):
        return pl.ds(pl.multiple_of(j * t, t), t)

    def produce(s_ref, pos):
        j = lst_ref[pos]
        s = _dot_nt(qs_ref[...], kp_ref[rows(j), :])
        krel = (krel0 + (j - i) * t).astype(F32)
        for h in range(4):
            s_ref[h * t:(h + 1) * t, :] = s[h * t:(h + 1) * t] + (float(slopes[h]) * LOG2E) * krel

    def consume(s_ref, pos, diag, slot):
        _flash_consume(s_ref, p_ref.at[slot], stats, kv_ref[rows(lst_ref[pos]), :], t, diag)

    o = _flash_causal(n, produce, consume, s_a, s_b, stats)
    o_ref[...] = _unstack_wide_o(o, t).astype(o_ref.dtype)


def _sel(zm, sbias, used, t, slopes):
    bsz, s, _ = zm.shape
    grid_spec = pltpu.PrefetchScalarGridSpec(
        num_scalar_prefetch=1,
        grid=(bsz, s // t),
        in_specs=[pl.BlockSpec((None, t, 4 * LANES), lambda b, i, u: (b, i, U_NSA_Q // 4)),
                  pl.BlockSpec((None, t, LANES), lambda b, i, u: (b, i, 0)),
                  pl.BlockSpec((None, s, 2 * LANES), lambda b, i, u: (b, 0, U_SLC // 2))],
        out_specs=pl.BlockSpec((None, t, 2 * LANES), lambda b, i, u: (b, i, 0)),
        scratch_shapes=[pltpu.VMEM((s, LANES), BF16), pltpu.VMEM((4 * t, LANES), BF16),
                        pltpu.SMEM((s // t + 1,), jnp.int32)] + _flash_scratch(t),
    )
    return pl.pallas_call(
        functools.partial(_sel_body, t=t, slopes=tuple(float(x) for x in slopes)),
        out_shape=jax.ShapeDtypeStruct((bsz, s, 2 * LANES), BF16),
        grid_spec=grid_spec,
        compiler_params=_params(("parallel", "arbitrary"), VMEM_LIMIT),
        name="nsa_sel_attn",
    )(used, zm, sbias, zm)


def _spread_heads(vals, lane0, stride, tm):
    lane = lax.broadcasted_iota(jnp.int32, (tm, LANES), 1)
    col = lambda h: vals[:, lane0 + stride * h:lane0 + stride * h + 1]
    halves = [jnp.where(lane < HEAD_DIM, col(2 * p), col(2 * p + 1)) for p in range(2)]
    return jnp.concatenate(halves, axis=1)


def _outproj_body(x_ref, ofox_ref, ocmp_ref, oslc_ref, owin_ref, zs_ref, oswa_ref, lswa_ref, sink_ref,
                  od1_ref, od4_ref, od16_ref, l1_ref, l4_ref, l16_ref, eg_ref, w_ref, o_ref, o_scr, l_scr):
    tm = x_ref.shape[0]

    def natural(ref, scr):
        dd, _, w = ref.shape
        if dd == 1:
            return ref[0].astype(F32)
        for u in range(w // LANES):
            for r in range(dd):
                scr[u, pl.ds(r, tm // dd, stride=dd), :] = ref[r, :, u * LANES:(u + 1) * LANES].astype(F32)
        return jnp.concatenate([scr[u] for u in range(w // LANES)], axis=1)

    gate = jax.nn.sigmoid(zs_ref[...])
    g_hi = gate.astype(BF16)
    g_lo = (gate - g_hi.astype(F32)).astype(BF16)
    spread = lambda br: (jnp.dot(g_hi, eg_ref[br], preferred_element_type=F32)
                         + jnp.dot(g_lo, eg_ref[br], preferred_element_type=F32))
    o_nsa = (spread(0) * ocmp_ref[...].astype(F32) + spread(1) * oslc_ref[...].astype(F32)
             + spread(2) * owin_ref[...].astype(F32))
    keep = jax.nn.sigmoid(lswa_ref[...] - sink_ref[...])
    o_swa = _spread_heads(keep, 0, 1, tm) * oswa_ref[...].astype(F32)
    l1, l4, l16 = natural(l1_ref, l_scr), natural(l4_ref, l_scr), natural(l16_ref, l_scr)
    m = jnp.maximum(jnp.maximum(l1, l4), l16)
    e1, e4, e16 = jnp.exp(l1 - m), jnp.exp(l4 - m), jnp.exp(l16 - m)
    inv = 1.0 / (e1 + e4 + e16)
    o_dil = _spread_heads(e1 * inv, 0, 1, tm) * natural(od1_ref, o_scr)
    o_dil = o_dil + _spread_heads(e4 * inv, 0, 1, tm) * natural(od4_ref, o_scr)
    o_dil = o_dil + _spread_heads(e16 * inv, 0, 1, tm) * natural(od16_ref, o_scr)
    y = x_ref[...]
    for g, o in enumerate((ofox_ref[...], o_nsa.astype(BF16), o_swa.astype(BF16), o_dil.astype(BF16))):
        y = y + jnp.dot(o, w_ref[g], preferred_element_type=F32)
    o_ref[...] = y


def _gate_spread_matrices():
    e = np.zeros((3, LANES, 2 * LANES), np.float32)
    for h in range(4):
        for br in range(3):
            e[br, GATE_LANE0 + 3 * h + br, HEAD_DIM * h:HEAD_DIM * (h + 1)] = 1.0
    return jnp.asarray(e, BF16)


def _outproj(x, heads, zs, lses, sinks, w):
    bsz, s, d = x.shape
    tm = min(ROW_TILE, s)
    row = lambda b, i: (b, i, 0)
    wide = pl.BlockSpec((None, tm, 2 * LANES), row)
    narrow = pl.BlockSpec((None, tm, LANES), row)

    def strided(a):
        dd = a.shape[1]
        return pl.BlockSpec((None, dd, tm // dd, a.shape[-1]), lambda b, i: (b, 0, i, 0))

    ofox, ocmp, oslc, owin, oswa, od1, od4, od16 = heads
    lswa, l1, l4, l16 = lses
    return pl.pallas_call(
        _outproj_body,
        out_shape=jax.ShapeDtypeStruct((bsz, s, d), F32),
        grid=(bsz, s // tm),
        in_specs=[pl.BlockSpec((None, tm, d), row), wide, wide, wide, wide, narrow, wide, narrow,
                  pl.BlockSpec((1, LANES), lambda b, i: (0, 0)), strided(od1), strided(od4), strided(od16),
                  strided(l1), strided(l4), strided(l16),
                  pl.BlockSpec((3, LANES, 2 * LANES), lambda b, i: (0, 0, 0)),
                  pl.BlockSpec((4, 2 * LANES, d), lambda b, i: (0, 0, 0))],
        out_specs=pl.BlockSpec((None, tm, d), row),
        scratch_shapes=[pltpu.VMEM((2, tm, LANES), F32), pltpu.VMEM((1, tm, LANES), F32)],
        compiler_params=_params(("parallel", "parallel"), VMEM_LIMIT),
        name="outproj",
    )(x, ofox, ocmp, oslc, owin, zs, oswa, lswa, sinks, od1, od4, od16, l1, l4, l16, _gate_spread_matrices(), w)


def _overlap_t(n_cmp, n_sel):
    cs = np.arange(n_cmp) * NSA_CMP_STRIDE
    ss = np.arange(n_sel) * NSA_SEL_LEN
    ov = (cs[None, :] <= ss[:, None] + NSA_SEL_LEN - 1) & (cs[None, :] + NSA_CMP_LEN - 1 >= ss[:, None])
    out = np.zeros((HEAD_DIM, n_cmp), np.float32)
    out[:n_sel] = ov
    return jnp.asarray(out)


def _mixer(x, g_mix, w_in, fox_b_f, cmp_pe, cmp_w1, cmp_w2, swa_sinks, w_out):
    bsz, s, d = x.shape
    sl_swa, sl_nsa, sl_dil = _alibi_slopes()
    cols, scale = _w_in_layout()
    w_perm = _relayout_w_in(w_in, cols, scale)
    zm, zs, *zds = _inproj(x, g_mix, w_perm)

    t_fox = min(ROW_TILE, s)
    f_rows = jnp.transpose(zs[:, :, :8], (0, 2, 1))
    b_rows = jnp.pad(fox_b_f.reshape(4, 1), ((0, 4), (0, 0)))
    c = _fox_cumsum(f_rows, b_rows, t_fox)
    o_fox = _fox(zm, c, t_fox)

    n_chunk = s // NSA_CMP_STRIDE
    n_sel = s // NSA_SEL_LEN
    assert n_sel <= HEAD_DIM
    kvc = zm[:, :, U_CMP * LANES:(U_CMP + 1) * LANES]
    kch = kvc[:, :, :HEAD_DIM].reshape(bsz, n_chunk, NSA_CMP_STRIDE * HEAD_DIM)
    vch = kvc[:, :, HEAD_DIM:].reshape(bsz, n_chunk, NSA_CMP_STRIDE * HEAD_DIM)
    pe = cmp_pe.reshape(2, 2, 1, NSA_CMP_STRIDE * HEAD_DIM)
    w1 = cmp_w1.reshape(2, 2, NSA_CMP_STRIDE * HEAD_DIM, NSA_CMP_HIDDEN).astype(BF16)
    zpad = jnp.zeros((NSA_CMP_HIDDEN, HEAD_DIM), F32)
    w2 = jnp.stack([jnp.concatenate([cmp_w2[0], zpad, zpad, cmp_w2[0]], axis=1),
                    jnp.concatenate([zpad, cmp_w2[1], cmp_w2[1], zpad], axis=1)]).astype(BF16)
    kv_c = _compress(kch, vch, pe, w1, w2)
    t_sel = min(ROW_TILE, s)
    o_cmp, sbias, used = _cmp_select(zm, kv_c, _overlap_t(n_chunk, n_sel), t_sel, sl_nsa, min(NSA_TOPN, n_sel))
    o_slc = _sel(zm, sbias, used[:, :, :, 0], t_sel, sl_nsa)
    (o_win,) = _banded((zm, zm), (U_NSA_Q // 4, U_WIN // 2), sl_nsa, NSA_WINDOW - 1, NSA_WINDOW,
                       True, False, "nsa_win_attn")

    o_swa, l_swa = _banded((zm, zm, zm), (U_SWA_Q // 2, U_SWA_K // 2, U_SWA_V // 2), sl_swa, SWA_WINDOW - 1,
                           SWA_WINDOW, False, True, "swa_attn")

    o_dil, l_dil = [], []
    for (window, dd), zd in zip(DIL_PAIRS, zds):
        zz = zd.reshape(bsz * dd, s // dd, N_DIL * LANES)
        o, l = _banded((zz, zz, zz), (0, 1, 2), sl_dil * dd, window // dd, LANES, False, True, "dil%d_attn" % dd)
        o_dil.append(o.reshape(bsz, dd, s // dd, 2 * LANES))
        l_dil.append(l.reshape(bsz, dd, s // dd, LANES))

    sinks = jnp.pad(swa_sinks.reshape(1, 4), ((0, 0), (0, LANES - 4)))
    heads = (o_fox, o_cmp, o_slc, o_win, o_swa, *o_dil)
    return _outproj(x, heads, zs, (l_swa, *l_dil), sinks, w_out.reshape(4, 2 * LANES, d).astype(BF16))


def kernel(x, norm_ffn1, ffn1_w_gate, ffn1_w_up, ffn1_w_down, norm_mix, w_in, fox_b_f, nsa_cmp_pe, nsa_cmp_w1,
           nsa_cmp_w2, swa_sinks, w_out, norm_ffn2, ffn2_w_gate, ffn2_w_up, ffn2_w_down, norm_final):
    bsz, s, d = x.shape
    depth = norm_ffn1.shape[0]
    h = x.reshape(bsz * s, d)
    for l in range(depth):
        h = _ffn(h, norm_ffn1[l], ffn1_w_gate[l].astype(BF16), ffn1_w_up[l].astype(BF16), ffn1_w_down[l].astype(BF16))
        h = _mixer(h.reshape(bsz, s, d), norm_mix[l], w_in[l], fox_b_f[l], nsa_cmp_pe[l], nsa_cmp_w1[l],
                   nsa_cmp_w2[l], swa_sinks[l], w_out[l]).reshape(bsz * s, d)
        h = _ffn(h, norm_ffn2[l], ffn2_w_gate[l].astype(BF16), ffn2_w_up[l].astype(BF16), ffn2_w_down[l].astype(BF16),
                 norm_final if l == depth - 1 else None)
    return h.reshape(bsz, s, d)
```

```python
import functools

import numpy as np
import jax
import jax.numpy as jnp
from jax import lax
from jax.experimental import pallas as pl
from jax.experimental.pallas import tpu as pltpu

F32 = jnp.float32
BF16 = jnp.bfloat16

HEAD_DIM = 64
LANES = 128
NSA_CMP_LEN = 32
NSA_CMP_STRIDE = 16
NSA_CMP_HIDDEN = 128
NSA_SEL_LEN = 64
NSA_TOPN = 16
NSA_WINDOW = 512
SWA_WINDOW = 128
DIL_PAIRS = ((128, 1), (512, 4), (2048, 16))
RMS_EPS = 1e-6
LOG2E = 1.4426950408889634
LN2 = 0.6931471805599453
MASKED = -1e30
VMEM_LIMIT = 56 * 1024 * 1024
ROW_TILE = 512

IN_SPLITS = (
    ('fox_q', 256), ('fox_k', 256), ('fox_v', 256), ('fox_f', 4),
    ('nsa_q', 256), ('nsa_k_cmp', 64), ('nsa_v_cmp', 64), ('nsa_k_slc', 64), ('nsa_v_slc', 64),
    ('nsa_k_win', 64), ('nsa_v_win', 64), ('nsa_gate', 12),
    ('swa_q', 256), ('swa_k', 128), ('swa_v', 128),
    ('dil_q', 256), ('dil_k', 256), ('dil_v', 256),
)

U_NSA_Q = 0
U_FOX_Q, U_FOX_K, U_FOX_V = 4, 6, 8
U_SWA_Q, U_SWA_K, U_SWA_V = 10, 12, 14
U_SLC = 16
U_WIN = 18
U_CMP = 20
N_MAIN = 21
N_DIL = 6
N_UNITS = N_MAIN + N_DIL + 1
GATE_LANE0 = 4


def _alibi_slopes():
    n = 12
    s = 2.0 ** (-8.0 * np.arange(1, n + 1) / n)
    return s[:4], s[4:8], s[8:]


def _w_in_layout():
    off, o = {}, 0
    for name, w in IN_SPLITS:
        off[name] = o
        o += w
    cols, scale = [], []

    def seg(name, start, width, sc=1.0):
        cols.extend(range(off[name] + start, off[name] + start + width))
        scale.extend([sc] * width)

    def zeros(n):
        cols.extend([0] * n)
        scale.extend([0.0] * n)

    qs = HEAD_DIM ** -0.5 * LOG2E
    for h in range(4):
        seg('nsa_q', 64 * h, 64, qs)
        zeros(64)
    seg('fox_q', 0, 256, qs)
    seg('fox_k', 0, 256)
    seg('fox_v', 0, 256)
    seg('swa_q', 0, 256, qs)
    for name in ('swa_k', 'swa_v'):
        for kv in range(2):
            seg(name, 64 * kv, 64)
            seg(name, 64 * kv, 64)
    for a, b in (('nsa_k_slc', 'nsa_v_slc'), ('nsa_v_slc', 'nsa_k_slc'),
                 ('nsa_k_win', 'nsa_v_win'), ('nsa_v_win', 'nsa_k_win')):
        seg(a, 0, 64)
        seg(b, 0, 64)
    seg('nsa_k_cmp', 0, 64)
    seg('nsa_v_cmp', 0, 64)
    seg('dil_q', 0, 256, qs)
    seg('dil_k', 0, 256)
    seg('dil_v', 0, 256)
    seg('fox_f', 0, 4)
    seg('nsa_gate', 0, 12)
    zeros(LANES - 16)
    assert len(cols) == N_UNITS * LANES
    return np.asarray(cols, np.int32), np.asarray(scale, np.float32)


def _relayout_w_in(w_in, cols, scale):
    runs, start = [], 0
    for c in range(1, len(cols) + 1):
        same = c < len(cols) and scale[c] == scale[c - 1] and (scale[c] == 0.0 or cols[c] == cols[c - 1] + 1)
        if not same:
            runs.append((start, c))
            start = c
    parts = []
    for a, b in runs:
        if scale[a] == 0.0:
            parts.append(jnp.zeros((w_in.shape[0], b - a), BF16))
        else:
            parts.append((w_in[:, int(cols[a]):int(cols[a]) + (b - a)] * float(scale[a])).astype(BF16))
    return jnp.concatenate(parts, axis=1)


def _rms(x, g):
    ms = jnp.mean(x * x, axis=-1, keepdims=True)
    return x * lax.rsqrt(ms + RMS_EPS) * g


def _dot_nt(a, b):
    return lax.dot_general(a, b, (((1,), (1,)), ((), ())), preferred_element_type=F32)


def _params(sem, vmem=None):
    return pltpu.CompilerParams(dimension_semantics=sem, vmem_limit_bytes=vmem)


def _stack_packed_q(q):
    slot = lax.shift_right_logical(lax.broadcasted_iota(jnp.int32, q.shape, 1), 6)
    return jnp.concatenate([jnp.where(slot == h, q, jnp.zeros_like(q)) for h in range(4)], axis=0)


def _unstack_packed_o(o, t):
    slot = lax.shift_right_logical(lax.broadcasted_iota(jnp.int32, (t, 2 * LANES), 1), 6)
    out = o[:t]
    for h in range(1, 4):
        out = jnp.where(slot == h, o[h * t:(h + 1) * t], out)
    return out


def _stack_wide_q(q_ref):
    return jnp.concatenate([q_ref[:, h * LANES:(h + 1) * LANES] for h in range(4)], axis=0)


def _unstack_wide_o(o, t):
    lane = lax.broadcasted_iota(jnp.int32, (t, LANES), 1)
    pairs = [jnp.where(lane < HEAD_DIM, o[2 * pr * t:(2 * pr + 1) * t, LANES:], o[(2 * pr + 1) * t:(2 * pr + 2) * t, :LANES])
             for pr in range(2)]
    return jnp.concatenate(pairs, axis=1)


FLASH_ROW_BLOCK = 64


def _flash_consume(s_ref, p_ref, stats, v, t, diag):
    m_ref, alpha_ref, l_ref, acc_ref = stats
    n = s_ref.shape[1]
    rb = FLASH_ROW_BLOCK
    lanes = [slice(c * LANES, (c + 1) * LANES) for c in range(n // LANES)]

    def scores(r0):
        s = s_ref[r0:r0 + rb, :]
        if diag:
            row = lax.broadcasted_iota(jnp.int32, (rb, n), 0) + r0 % t
            col = lax.broadcasted_iota(jnp.int32, (rb, n), 1)
            s = jnp.where(col <= row, s, MASKED)
        return s

    for r0 in range(0, 4 * t, rb):
        s = scores(r0)
        mx = s[:, lanes[0]]
        for c in lanes[1:]:
            mx = jnp.maximum(mx, s[:, c])
        m_old = m_ref[r0:r0 + rb, :]
        m_new = jnp.maximum(m_old, jnp.max(mx, axis=-1, keepdims=True))
        alpha_ref[r0:r0 + rb, :] = jnp.exp2(m_old - m_new)
        m_ref[r0:r0 + rb, :] = m_new
    for r0 in range(0, 4 * t, rb):
        s = scores(r0)
        m_new = m_ref[r0:r0 + rb, :]
        psum = jnp.zeros((rb, LANES), F32)
        for c in lanes:
            p = jnp.exp2(s[:, c] - m_new)
            psum = psum + p
            p_ref[r0:r0 + rb, c] = p.astype(BF16)
        l_ref[r0:r0 + rb, :] = alpha_ref[r0:r0 + rb, :] * l_ref[r0:r0 + rb, :] + psum
    alpha = alpha_ref[...]
    acc_ref[...] = (jnp.concatenate([alpha, alpha], axis=1) * acc_ref[...]
                    + jnp.dot(p_ref[...], v, preferred_element_type=F32))


def _flash_causal(i, produce, consume, s_a, s_b, stats):
    m_ref, alpha_ref, l_ref, acc_ref = stats
    m_ref[...] = jnp.full(m_ref.shape, MASKED, F32)
    l_ref[...] = jnp.zeros(l_ref.shape, F32)
    acc_ref[...] = jnp.zeros(acc_ref.shape, F32)
    produce(s_a, 0)

    def pair(jj, carry):
        j = 2 * jj
        produce(s_b, j + 1)
        consume(s_a, j, False, 0)
        produce(s_a, j + 2)
        consume(s_b, j + 1, False, 1)
        return carry

    lax.fori_loop(0, i // 2, pair, 0)

    @pl.when(i % 2 == 1)
    def _():
        produce(s_b, i)
        consume(s_a, i - 1, False, 0)
        consume(s_b, i, True, 1)

    @pl.when(i % 2 == 0)
    def _():
        consume(s_a, i, True, 0)

    return acc_ref[...] * (1.0 / jnp.sum(l_ref[...], axis=-1, keepdims=True))


def _flash_scratch(t):
    return [pltpu.VMEM((4 * t, t), F32), pltpu.VMEM((4 * t, t), F32), pltpu.VMEM((2, 4 * t, t), BF16),
            pltpu.VMEM((4 * t, LANES), F32), pltpu.VMEM((4 * t, LANES), F32), pltpu.VMEM((4 * t, LANES), F32),
            pltpu.VMEM((4 * t, 2 * LANES), F32)]


def _ffn_body(x_ref, g_ref, wg_ref, wu_ref, wd_ref, *rest, final):
    o_ref = rest[-1]
    x = x_ref[...]
    xn = _rms(x, g_ref[...]).astype(BF16)
    a = jnp.dot(xn, wg_ref[...], preferred_element_type=F32)
    b = jnp.dot(xn, wu_ref[...], preferred_element_type=F32)
    h = (a * jax.nn.sigmoid(a) * b).astype(BF16)
    y = x + 0.5 * jnp.dot(h, wd_ref[...], preferred_element_type=F32)
    if final:
        y = _rms(y, rest[0][...])
    o_ref[...] = y


def _ffn(x, g, wg, wu, wd, g_final=None):
    n, d = x.shape
    f = wg.shape[1]
    tm = ROW_TILE if n % ROW_TILE == 0 else n
    const = lambda i: (0, 0)
    once = pl.Buffered(1)
    in_specs = [
        pl.BlockSpec((tm, d), lambda i: (i, 0)),
        pl.BlockSpec((1, d), const),
        pl.BlockSpec((d, f), const, pipeline_mode=once),
        pl.BlockSpec((d, f), const, pipeline_mode=once),
        pl.BlockSpec((f, d), const, pipeline_mode=once),
    ]
    args = [x, g.reshape(1, d), wg, wu, wd]
    if g_final is not None:
        in_specs.append(pl.BlockSpec((1, d), const))
        args.append(g_final.reshape(1, d))
    return pl.pallas_call(
        functools.partial(_ffn_body, final=g_final is not None),
        out_shape=jax.ShapeDtypeStruct((n, d), F32),
        grid=(n // tm,),
        in_specs=in_specs,
        out_specs=pl.BlockSpec((tm, d), lambda i: (i, 0)),
        compiler_params=_params(("parallel",), VMEM_LIMIT),
        name="ffn",
    )(*args)


def _inproj_body(x_ref, g_ref, w_ref, zm_ref, zs_ref, *rest):
    zd_refs, zscr = rest[:-1], rest[-1]
    tm = x_ref.shape[0]
    xn = _rms(x_ref[...], g_ref[...]).astype(BF16)
    z = jnp.dot(xn, w_ref[...], preferred_element_type=F32)
    zm_ref[...] = z[:, :N_MAIN * LANES].astype(BF16)
    zs_ref[...] = z[:, (N_MAIN + N_DIL) * LANES:]
    for u in range(N_DIL):
        zscr[u] = z[:, (N_MAIN + u) * LANES:(N_MAIN + u + 1) * LANES]
    for (_, dd), ref in zip(DIL_PAIRS, zd_refs):
        for u in range(N_DIL):
            if dd == 1:
                ref[0, :, u * LANES:(u + 1) * LANES] = zscr[u].astype(BF16)
            else:
                for r in range(dd):
                    ref[r, :, u * LANES:(u + 1) * LANES] = zscr[u, pl.ds(r, tm // dd, stride=dd), :].astype(BF16)


def _inproj(x, g, w):
    bsz, s, d = x.shape
    tm = min(ROW_TILE, s)
    const = lambda b, i: (0, 0)
    row = lambda b, i: (b, i, 0)
    wd = N_DIL * LANES
    zd_shapes = tuple(jax.ShapeDtypeStruct((bsz, dd, s // dd, wd), BF16) for _, dd in DIL_PAIRS)
    zd_specs = tuple(pl.BlockSpec((None, dd, tm // dd, wd), lambda b, i: (b, 0, i, 0)) for _, dd in DIL_PAIRS)
    return pl.pallas_call(
        _inproj_body,
        out_shape=(jax.ShapeDtypeStruct((bsz, s, N_MAIN * LANES), BF16), jax.ShapeDtypeStruct((bsz, s, LANES), F32))
        + zd_shapes,
        grid=(bsz, s // tm),
        in_specs=[pl.BlockSpec((None, tm, d), row), pl.BlockSpec((1, d), const),
                  pl.BlockSpec((d, N_UNITS * LANES), const, pipeline_mode=pl.Buffered(1))],
        out_specs=(pl.BlockSpec((None, tm, N_MAIN * LANES), row), pl.BlockSpec((None, tm, LANES), row)) + zd_specs,
        scratch_shapes=[pltpu.VMEM((N_DIL, tm, LANES), F32)],
        compiler_params=_params(("parallel", "parallel"), VMEM_LIMIT),
        name="inproj",
    )(x, g.reshape(1, d), w)


def _cumsum_body(x_ref, b_ref, u_ref, c_ref, *, tk):
    x = x_ref[...] + b_ref[...]
    lf = (jnp.minimum(x, 0.0) - jnp.log(1.0 + jnp.exp(-jnp.abs(x)))) * LOG2E
    u = u_ref[...]
    per = tk // LANES
    local = [jnp.dot(lf[:, j * LANES:(j + 1) * LANES], u, precision=lax.Precision.HIGHEST, preferred_element_type=F32)
             for j in range(x.shape[1] // LANES)]
    carry = jnp.zeros((8, 1), F32)
    for j, cs in enumerate(local):
        c_ref[j // per, :, (j % per) * LANES:(j % per + 1) * LANES] = cs + carry
        carry = carry + cs[:, LANES - 1:LANES]


def _fox_cumsum(f_rows, b_rows, tk):
    bsz, _, s = f_rows.shape
    u = jnp.asarray(np.triu(np.ones((LANES, LANES), np.float32)))
    return pl.pallas_call(
        functools.partial(_cumsum_body, tk=tk),
        out_shape=jax.ShapeDtypeStruct((bsz, s // tk, 8, tk), F32),
        grid=(bsz,),
        in_specs=[pl.BlockSpec((None, 8, s), lambda b: (b, 0, 0)),
                  pl.BlockSpec((8, 1), lambda b: (0, 0)),
                  pl.BlockSpec((LANES, LANES), lambda b: (0, 0))],
        out_specs=pl.BlockSpec((None, s // tk, 8, tk), lambda b: (b, 0, 0, 0)),
        compiler_params=_params(("parallel",)),
        name="fox_cumsum",
    )(f_rows, b_rows, u)


def _fox_body(q_ref, k_ref, v_ref, c_ref, o_ref, qs_ref, s_a, s_b, p_ref, *stats, t):
    i = pl.program_id(1)
    qs_ref[...] = _stack_packed_q(q_ref[...])

    def rows(j):
        return pl.ds(pl.multiple_of(j * t, t), t)

    def produce(s_ref, j):
        s = _dot_nt(qs_ref[...], k_ref[rows(j), :])
        for h in range(4):
            s_ref[h * t:(h + 1) * t, :] = s[h * t:(h + 1) * t] - c_ref[j, h:h + 1, :]

    def consume(s_ref, j, diag, slot):
        _flash_consume(s_ref, p_ref.at[slot], stats, v_ref[rows(j), :], t, diag)

    o = _flash_causal(i, produce, consume, s_a, s_b, stats)
    o_ref[...] = _unstack_packed_o(o, t).astype(o_ref.dtype)


def _fox(zm, c, t):
    bsz, s, _ = zm.shape
    return pl.pallas_call(
        functools.partial(_fox_body, t=t),
        out_shape=jax.ShapeDtypeStruct((bsz, s, 2 * LANES), BF16),
        grid=(bsz, s // t),
        in_specs=[pl.BlockSpec((None, t, 2 * LANES), lambda b, i: (b, i, U_FOX_Q // 2)),
                  pl.BlockSpec((None, s, 2 * LANES), lambda b, i: (b, 0, U_FOX_K // 2)),
                  pl.BlockSpec((None, s, 2 * LANES), lambda b, i: (b, 0, U_FOX_V // 2)),
                  pl.BlockSpec((None, s // t, 8, t), lambda b, i: (b, 0, 0, 0))],
        out_specs=pl.BlockSpec((None, t, 2 * LANES), lambda b, i: (b, i, 0)),
        scratch_shapes=[pltpu.VMEM((4 * t, 2 * LANES), BF16)] + _flash_scratch(t),
        compiler_params=_params(("parallel", "arbitrary"), VMEM_LIMIT),
        name="fox_attn",
    )(zm, zm, zm, c)


def _banded_body(*refs, gb, tq, sq, span, s_loc, wpad, max_dist, slopes, wide, with_lse):
    i = pl.program_id(1)
    o_ref = refs[2] if wide else refs[3]
    trel = lax.broadcasted_iota(jnp.int32, (sq, 1), 0).astype(F32)

    def band_bias(kpos_minus_t0):
        dist = lax.broadcasted_iota(jnp.int32, (sq, 1), 0) - kpos_minus_t0
        mask = lax.bitcast_convert_type(dist, jnp.uint32) <= jnp.uint32(max_dist)
        krel = kpos_minus_t0.astype(F32)
        return [jnp.where(mask, (float(slopes[h]) * LOG2E) * krel, MASKED) for h in range(4)]

    unclamped = [span == sq + wpad and r * sq >= wpad for r in range(tq // sq)]
    shared = band_bias(lax.broadcasted_iota(jnp.int32, (1, span), 1) - wpad) if any(unclamped) else None
    for g in range(gb):
        for r in range(tq // sq):
            t0 = i * tq + r * sq
            rows = slice(r * sq, (r + 1) * sq)
            if unclamped[r]:
                start = pl.multiple_of(t0 - wpad, LANES)
                bias = shared
            else:
                start = pl.multiple_of(jnp.clip(t0 - wpad, 0, s_loc - span), LANES)
                bias = band_bias(start - t0 + lax.broadcasted_iota(jnp.int32, (1, span), 1))
            if wide:
                q_ref, kv_ref = refs[:2]
                qs = jnp.concatenate([q_ref[g, rows, h * LANES:(h + 1) * LANES] for h in range(4)], axis=0)
                k = kv_ref[g, pl.ds(start, span), :LANES]
                v = kv_ref[g, pl.ds(start, span), :]
            else:
                q_ref, k_ref, v_ref = refs[:3]
                qs = _stack_packed_q(q_ref[g, rows, :])
                k = k_ref[g, pl.ds(start, span), :]
                v = v_ref[g, pl.ds(start, span), :]
            s = _dot_nt(qs, k)
            s = jnp.concatenate([s[h * sq:(h + 1) * sq] + bias[h] for h in range(4)], axis=0)
            m = jnp.max(s, axis=-1, keepdims=True)
            e = jnp.exp2(s - m)
            l = jnp.sum(e, axis=-1, keepdims=True)
            o = jnp.dot(e.astype(BF16), v, preferred_element_type=F32) * (1.0 / l)
            o_ref[g, rows, :] = (_unstack_wide_o(o, sq) if wide else _unstack_packed_o(o, sq)).astype(o_ref.dtype)
            if with_lse:
                lse = (m + jnp.log2(l)) * LN2
                lane = lax.broadcasted_iota(jnp.int32, (sq, LANES), 1)
                acc = jnp.zeros((sq, LANES), F32)
                for h in range(4):
                    acc = jnp.where(lane == h, lse[h * sq:(h + 1) * sq] - float(slopes[h]) * trel, acc)
                refs[-1][g, rows, :] = acc


BANDED_ROWS = 2048


def _banded(arrs, units, slopes, max_dist, wpad, wide, with_lse, name):
    g, s_loc, _ = arrs[0].shape
    tq = min(BANDED_ROWS, s_loc)
    gb = min(g, BANDED_ROWS // tq)
    sq = min(tq, wpad, 256)
    span = min(sq + wpad, s_loc)
    if wide:
        in_specs = [pl.BlockSpec((gb, tq, 4 * LANES), lambda b, i: (b, i, units[0])),
                    pl.BlockSpec((gb, s_loc, 2 * LANES), lambda b, i: (b, 0, units[1]))]
    else:
        in_specs = [pl.BlockSpec((gb, tq, 2 * LANES), lambda b, i: (b, i, units[0])),
                    pl.BlockSpec((gb, s_loc, 2 * LANES), lambda b, i: (b, 0, units[1])),
                    pl.BlockSpec((gb, s_loc, 2 * LANES), lambda b, i: (b, 0, units[2]))]
    out_shape = [jax.ShapeDtypeStruct((g, s_loc, 2 * LANES), BF16)]
    out_specs = [pl.BlockSpec((gb, tq, 2 * LANES), lambda b, i: (b, i, 0))]
    if with_lse:
        out_shape.append(jax.ShapeDtypeStruct((g, s_loc, LANES), F32))
        out_specs.append(pl.BlockSpec((gb, tq, LANES), lambda b, i: (b, i, 0)))
    body = functools.partial(_banded_body, gb=gb, tq=tq, sq=sq, span=span, s_loc=s_loc, wpad=wpad,
                             max_dist=max_dist, slopes=tuple(float(x) for x in slopes), wide=wide, with_lse=with_lse)
    return pl.pallas_call(
        body, out_shape=tuple(out_shape), grid=(g // gb, s_loc // tq), in_specs=in_specs, out_specs=tuple(out_specs),
        compiler_params=_params(("parallel", "arbitrary")), name=name,
    )(*arrs)


def _compress_body(k_ref, v_ref, pe_ref, w1_ref, w2_ref, o_ref):
    n = k_ref.shape[0]
    hid = []
    for idx, ref in enumerate((k_ref, v_ref)):
        ch = ref[...].astype(F32)
        a = jnp.dot((ch + pe_ref[idx, 0]).astype(BF16), w1_ref[idx, 0], preferred_element_type=F32)
        b = jnp.dot((ch + pe_ref[idx, 1]).astype(BF16), w1_ref[idx, 1], preferred_element_type=F32)
        pre = a + pltpu.roll(b, n - 1, 0)
        hid.append((pre * jax.nn.sigmoid(pre)).astype(BF16))
    o_ref[...] = (jnp.dot(hid[0], w2_ref[0], preferred_element_type=F32)
                  + jnp.dot(hid[1], w2_ref[1], preferred_element_type=F32)).astype(o_ref.dtype)


def _compress(kch, vch, pe, w1, w2):
    bsz, n, w = kch.shape
    full = lambda *shape: pl.BlockSpec(shape, lambda b: (0,) * len(shape))
    blk = pl.BlockSpec((None, n, w), lambda b: (b, 0, 0))
    return pl.pallas_call(
        _compress_body,
        out_shape=jax.ShapeDtypeStruct((bsz, n, 2 * LANES), BF16),
        grid=(bsz,),
        in_specs=[blk, blk, full(2, 2, 1, w), full(2, 2, w, NSA_CMP_HIDDEN), full(2, NSA_CMP_HIDDEN, 2 * LANES)],
        out_specs=pl.BlockSpec((None, n, 2 * LANES), lambda b: (b, 0, 0)),
        compiler_params=_params(("parallel",)),
        name="nsa_compress",
    )(kch, vch, pe, w1, w2)


def _cmp_select_body(q_ref, kv_ref, ov_ref, o_ref, sb_ref, used_ref, *, tq, slopes, n_top):
    i = pl.program_id(1)
    t0 = i * tq
    n_cmp = kv_ref.shape[0]
    tpos = t0 + lax.broadcasted_iota(jnp.int32, (tq, 1), 0)
    cend = NSA_CMP_STRIDE * lax.broadcasted_iota(jnp.int32, (1, n_cmp), 1) + (NSA_CMP_LEN - 1)
    dist = tpos - cend
    mask = dist >= 0
    distf = dist.astype(F32)
    kv = kv_ref[...]
    s_all = _dot_nt(_stack_wide_q(q_ref), kv[:, :LANES])
    psum = jnp.zeros((tq, n_cmp), F32)
    ps = []
    for h in range(4):
        s = jnp.where(mask, s_all[h * tq:(h + 1) * tq] - (float(slopes[h]) * LOG2E) * distf, -jnp.inf)
        m = jnp.max(s, axis=-1, keepdims=True)
        m = jnp.where(m == -jnp.inf, 0.0, m)
        e = jnp.exp2(s - m)
        p = e / jnp.maximum(jnp.sum(e, axis=-1, keepdims=True), 1e-30)
        ps.append(p.astype(BF16))
        psum = psum + p
    o = jnp.dot(jnp.concatenate(ps, axis=0), kv, preferred_element_type=F32)
    o_ref[...] = _unstack_wide_o(o, tq).astype(o_ref.dtype)

    imp = lax.dot_general(ov_ref[...], psum, (((1,), (1,)), ((), ())), precision=lax.Precision.HIGHEST,
                          preferred_element_type=F32)
    j = lax.broadcasted_iota(jnp.int32, (HEAD_DIM, tq), 0)
    tl = t0 + lax.broadcasted_iota(jnp.int32, (HEAD_DIM, tq), 1)
    cur = lax.shift_right_logical(tl, 6)
    causal = j * NSA_SEL_LEN <= tl
    forced = (j == 0) | (j == cur) | (j == cur - 1)
    score = jnp.where(causal, jnp.where(forced, jnp.inf, imp), -jnp.inf)
    rank = jnp.zeros((HEAD_DIM, tq), F32)
    for jp in range(HEAD_DIM):
        other = score[jp:jp + 1, :]
        rank = rank + jnp.where(other > score, 1.0, jnp.where((other == score) & (j > jp), 1.0, 0.0))
    keep = causal & (rank < float(n_top))
    bias_t = jnp.where(keep, 0.0, MASKED)
    sb_ref[...] = jnp.concatenate([jnp.zeros((HEAD_DIM, tq), F32), bias_t], axis=0).T.astype(sb_ref.dtype)
    used = jnp.max(jnp.where(keep, 1.0, 0.0), axis=1, keepdims=True)
    used_ref[...] = jnp.broadcast_to(used, (HEAD_DIM, LANES)).astype(jnp.int32)


def _cmp_select(zm, kv_c, overlap_t, tq, slopes, n_top):
    bsz, s, _ = zm.shape
    n_cmp = kv_c.shape[1]
    return pl.pallas_call(
        functools.partial(_cmp_select_body, tq=tq, slopes=tuple(float(x) for x in slopes), n_top=n_top),
        out_shape=(jax.ShapeDtypeStruct((bsz, s, 2 * LANES), BF16), jax.ShapeDtypeStruct((bsz, s, LANES), BF16),
                   jax.ShapeDtypeStruct((bsz, s // tq, HEAD_DIM, LANES), jnp.int32)),
        grid=(bsz, s // tq),
        in_specs=[pl.BlockSpec((None, tq, 4 * LANES), lambda b, i: (b, i, U_NSA_Q // 4)),
                  pl.BlockSpec((None, n_cmp, 2 * LANES), lambda b, i: (b, 0, 0)),
                  pl.BlockSpec((HEAD_DIM, n_cmp), lambda b, i: (0, 0))],
        out_specs=(pl.BlockSpec((None, tq, 2 * LANES), lambda b, i: (b, i, 0)),
                   pl.BlockSpec((None, tq, LANES), lambda b, i: (b, i, 0)),
                   pl.BlockSpec((None, None, HEAD_DIM, LANES), lambda b, i: (b, i, 0, 0))),
        compiler_params=_params(("parallel", "parallel")),
        name="nsa_cmp_select",
    )(zm, kv_c, overlap_t)


def _sel_body(used_ref, q_ref, sb_ref, kv_ref, o_ref, kp_ref, qs_ref, lst_ref, s_a, s_b, p_ref, *stats, t, slopes):
    b = pl.program_id(0)
    i = pl.program_id(1)
    s_len = kv_ref.shape[0]

    @pl.when(i == 0)
    def _():
        pos = lax.broadcasted_iota(jnp.int32, (s_len, LANES), 0)
        ln = lax.broadcasted_iota(jnp.int32, (s_len, LANES), 1)
        onehot = jnp.where(lax.shift_right_logical(pos, 6) == ln - HEAD_DIM, 1.0, 0.0).astype(BF16)
        kp_ref[...] = jnp.where(ln < HEAD_DIM, kv_ref[:, :LANES], onehot)

    per = t // NSA_SEL_LEN
    n = jnp.int32(0)
    for c in range(s_len // t - 1):
        any_used = used_ref[b, i, c * per]
        for k in range(1, per):
            any_used = any_used | used_ref[b, i, c * per + k]
        lst_ref[n] = jnp.int32(c)
        n = n + jnp.where((c < i) & (any_used != 0), 1, 0)
    lst_ref[n] = i

    lane = lax.broadcasted_iota(jnp.int32, (t, LANES), 1)
    sb = sb_ref[...]
    for h in range(4):
        qs_ref[h * t:(h + 1) * t, :] = jnp.where(lane < HEAD_DIM, q_ref[:, h * LANES:(h + 1) * LANES], sb)
    krel0 = lax.broadcasted_iota(jnp.int32, (1, t), 1)

    def rows(j):
        return pl.ds(pl.multiple_of(j * t, t), t)

    def produce(s_ref, pos):
        j = lst_ref[pos]
        s = _dot_nt(qs_ref[...], kp_ref[rows(j), :])
        krel = (krel0 + (j - i) * t).astype(F32)
        for h in range(4):
            s_ref[h * t:(h + 1) * t, :] = s[h * t:(h + 1) * t] + (float(slopes[h]) * LOG2E) * krel

    def consume(s_ref, pos, diag, slot):
        _flash_consume(s_ref, p_ref.at[slot], stats, kv_ref[rows(lst_ref[pos]), :], t, diag)

    o = _flash_causal(n, produce, consume, s_a, s_b, stats)
    o_ref[...] = _unstack_wide_o(o, t).astype(o_ref.dtype)


def _sel(zm, sbias, used, t, slopes):
    bsz, s, _ = zm.shape
    grid_spec = pltpu.PrefetchScalarGridSpec(
        num_scalar_prefetch=1,
        grid=(bsz, s // t),
        in_specs=[pl.BlockSpec((None, t, 4 * LANES), lambda b, i, u: (b, i, U_NSA_Q // 4)),
                  pl.BlockSpec((None, t, LANES), lambda b, i, u: (b, i, 0)),
                  pl.BlockSpec((None, s, 2 * LANES), lambda b, i, u: (b, 0, U_SLC // 2))],
        out_specs=pl.BlockSpec((None, t, 2 * LANES), lambda b, i, u: (b, i, 0)),
        scratch_shapes=[pltpu.VMEM((s, LANES), BF16), pltpu.VMEM((4 * t, LANES), BF16),
                        pltpu.SMEM((s // t + 1,), jnp.int32)] + _flash_scratch(t),
    )
    return pl.pallas_call(
        functools.partial(_sel_body, t=t, slopes=tuple(float(x) for x in slopes)),
        out_shape=jax.ShapeDtypeStruct((bsz, s, 2 * LANES), BF16),
        grid_spec=grid_spec,
        compiler_params=_params(("parallel", "arbitrary"), VMEM_LIMIT),
        name="nsa_sel_attn",
    )(used, zm, sbias, zm)


def _spread_heads(vals, lane0, stride, tm):
    lane = lax.broadcasted_iota(jnp.int32, (tm, LANES), 1)
    col = lambda h: vals[:, lane0 + stride * h:lane0 + stride * h + 1]
    halves = [jnp.where(lane < HEAD_DIM, col(2 * p), col(2 * p + 1)) for p in range(2)]
    return jnp.concatenate(halves, axis=1)


def _outproj_body(x_ref, ofox_ref, ocmp_ref, oslc_ref, owin_ref, zs_ref, oswa_ref, lswa_ref, sink_ref,
                  od1_ref, od4_ref, od16_ref, l1_ref, l4_ref, l16_ref, eg_ref, w_ref, o_ref, o_scr, l_scr):
    tm = x_ref.shape[0]

    def natural(ref, scr):
        dd, _, w = ref.shape
        if dd == 1:
            return ref[0].astype(F32)
        for u in range(w // LANES):
            for r in range(dd):
                scr[u, pl.ds(r, tm // dd, stride=dd), :] = ref[r, :, u * LANES:(u + 1) * LANES].astype(F32)
        return jnp.concatenate([scr[u] for u in range(w // LANES)], axis=1)

    gate = jax.nn.sigmoid(zs_ref[...])
    g_hi = gate.astype(BF16)
    g_lo = (gate - g_hi.astype(F32)).astype(BF16)
    spread = lambda br: (jnp.dot(g_hi, eg_ref[br], preferred_element_type=F32)
                         + jnp.dot(g_lo, eg_ref[br], preferred_element_type=F32))
    o_nsa = (spread(0) * ocmp_ref[...].astype(F32) + spread(1) * oslc_ref[...].astype(F32)
             + spread(2) * owin_ref[...].astype(F32))
    keep = jax.nn.sigmoid(lswa_ref[...] - sink_ref[...])
    o_swa = _spread_heads(keep, 0, 1, tm) * oswa_ref[...].astype(F32)
    l1, l4, l16 = natural(l1_ref, l_scr), natural(l4_ref, l_scr), natural(l16_ref, l_scr)
    m = jnp.maximum(jnp.maximum(l1, l4), l16)
    e1, e4, e16 = jnp.exp(l1 - m), jnp.exp(l4 - m), jnp.exp(l16 - m)
    inv = 1.0 / (e1 + e4 + e16)
    o_dil = _spread_heads(e1 * inv, 0, 1, tm) * natural(od1_ref, o_scr)
    o_dil = o_dil + _spread_heads(e4 * inv, 0, 1, tm) * natural(od4_ref, o_scr)
    o_dil = o_dil + _spread_heads(e16 * inv, 0, 1, tm) * natural(od16_ref, o_scr)
    y = x_ref[...]
    for g, o in enumerate((ofox_ref[...], o_nsa.astype(BF16), o_swa.astype(BF16), o_dil.astype(BF16))):
        y = y + jnp.dot(o, w_ref[g], preferred_element_type=F32)
    o_ref[...] = y


def _gate_spread_matrices():
    e = np.zeros((3, LANES, 2 * LANES), np.float32)
    for h in range(4):
        for br in range(3):
            e[br, GATE_LANE0 + 3 * h + br, HEAD_DIM * h:HEAD_DIM * (h + 1)] = 1.0
    return jnp.asarray(e, BF16)


def _outproj(x, heads, zs, lses, sinks, w):
    bsz, s, d = x.shape
    tm = min(ROW_TILE, s)
    row = lambda b, i: (b, i, 0)
    wide = pl.BlockSpec((None, tm, 2 * LANES), row)
    narrow = pl.BlockSpec((None, tm, LANES), row)

    def strided(a):
        dd = a.shape[1]
        return pl.BlockSpec((None, dd, tm // dd, a.shape[-1]), lambda b, i: (b, 0, i, 0))

    ofox, ocmp, oslc, owin, oswa, od1, od4, od16 = heads
    lswa, l1, l4, l16 = lses
    return pl.pallas_call(
        _outproj_body,
        out_shape=jax.ShapeDtypeStruct((bsz, s, d), F32),
        grid=(bsz, s // tm),
        in_specs=[pl.BlockSpec((None, tm, d), row), wide, wide, wide, wide, narrow, wide, narrow,
                  pl.BlockSpec((1, LANES), lambda b, i: (0, 0)), strided(od1), strided(od4), strided(od16),
                  strided(l1), strided(l4), strided(l16),
                  pl.BlockSpec((3, LANES, 2 * LANES), lambda b, i: (0, 0, 0)),
                  pl.BlockSpec((4, 2 * LANES, d), lambda b, i: (0, 0, 0))],
        out_specs=pl.BlockSpec((None, tm, d), row),
        scratch_shapes=[pltpu.VMEM((2, tm, LANES), F32), pltpu.VMEM((1, tm, LANES), F32)],
        compiler_params=_params(("parallel", "parallel"), VMEM_LIMIT),
        name="outproj",
    )(x, ofox, ocmp, oslc, owin, zs, oswa, lswa, sinks, od1, od4, od16, l1, l4, l16, _gate_spread_matrices(), w)


def _overlap_t(n_cmp, n_sel):
    cs = np.arange(n_cmp) * NSA_CMP_STRIDE
    ss = np.arange(n_sel) * NSA_SEL_LEN
    ov = (cs[None, :] <= ss[:, None] + NSA_SEL_LEN - 1) & (cs[None, :] + NSA_CMP_LEN - 1 >= ss[:, None])
    out = np.zeros((HEAD_DIM, n_cmp), np.float32)
    out[:n_sel] = ov
    return jnp.asarray(out)


def _mixer(x, g_mix, w_in, fox_b_f, cmp_pe, cmp_w1, cmp_w2, swa_sinks, w_out):
    bsz, s, d = x.shape
    sl_swa, sl_nsa, sl_dil = _alibi_slopes()
    cols, scale = _w_in_layout()
    w_perm = _relayout_w_in(w_in, cols, scale)
    zm, zs, *zds = _inproj(x, g_mix, w_perm)

    t_fox = min(ROW_TILE, s)
    f_rows = jnp.transpose(zs[:, :, :8], (0, 2, 1))
    b_rows = jnp.pad(fox_b_f.reshape(4, 1), ((0, 4), (0, 0)))
    c = _fox_cumsum(f_rows, b_rows, t_fox)
    o_fox = _fox(zm, c, t_fox)

    n_chunk = s // NSA_CMP_STRIDE
    n_sel = s // NSA_SEL_LEN
    assert n_sel <= HEAD_DIM
    kvc = zm[:, :, U_CMP * LANES:(U_CMP + 1) * LANES]
    kch = kvc[:, :, :HEAD_DIM].reshape(bsz, n_chunk, NSA_CMP_STRIDE * HEAD_DIM)
    vch = kvc[:, :, HEAD_DIM:].reshape(bsz, n_chunk, NSA_CMP_STRIDE * HEAD_DIM)
    pe = cmp_pe.reshape(2, 2, 1, NSA_CMP_STRIDE * HEAD_DIM)
    w1 = cmp_w1.reshape(2, 2, NSA_CMP_STRIDE * HEAD_DIM, NSA_CMP_HIDDEN).astype(BF16)
    zpad = jnp.zeros((NSA_CMP_HIDDEN, HEAD_DIM), F32)
    w2 = jnp.stack([jnp.concatenate([cmp_w2[0], zpad, zpad, cmp_w2[0]], axis=1),
                    jnp.concatenate([zpad, cmp_w2[1], cmp_w2[1], zpad], axis=1)]).astype(BF16)
    kv_c = _compress(kch, vch, pe, w1, w2)
    t_sel = min(ROW_TILE, s)
    o_cmp, sbias, used = _cmp_select(zm, kv_c, _overlap_t(n_chunk, n_sel), t_sel, sl_nsa, min(NSA_TOPN, n_sel))
    o_slc = _sel(zm, sbias, used[:, :, :, 0], t_sel, sl_nsa)
    (o_win,) = _banded((zm, zm), (U_NSA_Q // 4, U_WIN // 2), sl_nsa, NSA_WINDOW - 1, NSA_WINDOW,
                       True, False, "nsa_win_attn")

    o_swa, l_swa = _banded((zm, zm, zm), (U_SWA_Q // 2, U_SWA_K // 2, U_SWA_V // 2), sl_swa, SWA_WINDOW - 1,
                           SWA_WINDOW, False, True, "swa_attn")

    o_dil, l_dil = [], []
    for (window, dd), zd in zip(DIL_PAIRS, zds):
        zz = zd.reshape(bsz * dd, s // dd, N_DIL * LANES)
        o, l = _banded((zz, zz, zz), (0, 1, 2), sl_dil * dd, window // dd, LANES, False, True, "dil%d_attn" % dd)
        o_dil.append(o.reshape(bsz, dd, s // dd, 2 * LANES))
        l_dil.append(l.reshape(bsz, dd, s // dd, LANES))

    sinks = jnp.pad(swa_sinks.reshape(1, 4), ((0, 0), (0, LANES - 4)))
    heads = (o_fox, o_cmp, o_slc, o_win, o_swa, *o_dil)
    return _outproj(x, heads, zs, (l_swa, *l_dil), sinks, w_out.reshape(4, 2 * LANES, d).astype(BF16))


def kernel(x, norm_ffn1, ffn1_w_gate, ffn1_w_up, ffn1_w_down, norm_mix, w_in, fox_b_f, nsa_cmp_pe, nsa_cmp_w1,
           nsa_cmp_w2, swa_sinks, w_out, norm_ffn2, ffn2_w_gate, ffn2_w_up, ffn2_w_down, norm_final):
    bsz, s, d = x.shape
    depth = norm_ffn1.shape[0]
    h = x.reshape(bsz * s, d)
    for l in range(depth):
        h = _ffn(h, norm_ffn1[l], ffn1_w_gate[l].astype(BF16), ffn1_w_up[l].astype(BF16), ffn1_w_down[l].astype(BF16))
        h = _mixer(h.reshape(bsz, s, d), norm_mix[l], w_in[l], fox_b_f[l], nsa_cmp_pe[l], nsa_cmp_w1[l],
                   nsa_cmp_w2[l], swa_sinks[l], w_out[l]).reshape(bsz * s, d)
        h = _ffn(h, norm_ffn2[l], ffn2_w_gate[l].astype(BF16), ffn2_w_up[l].astype(BF16), ffn2_w_down[l].astype(BF16),
                 norm_final if l == depth - 1 else None)
    return h.reshape(bsz, s, d)
```

```python
import functools

import numpy as np
import jax
import jax.numpy as jnp
from jax import lax
from jax.experimental import pallas as pl
from jax.experimental.pallas import tpu as pltpu

F32 = jnp.float32
BF16 = jnp.bfloat16

HEAD_DIM = 64
LANES = 128
NSA_CMP_LEN = 32
NSA_CMP_STRIDE = 16
NSA_CMP_HIDDEN = 128
NSA_SEL_LEN = 64
NSA_TOPN = 16
NSA_WINDOW = 512
SWA_WINDOW = 128
DIL_PAIRS = ((128, 1), (512, 4), (2048, 16))
RMS_EPS = 1e-6
LOG2E = 1.4426950408889634
LN2 = 0.6931471805599453
MASKED = -1e30
VMEM_LIMIT = 56 * 1024 * 1024
ROW_TILE = 512

IN_SPLITS = (
    ('fox_q', 256), ('fox_k', 256), ('fox_v', 256), ('fox_f', 4),
    ('nsa_q', 256), ('nsa_k_cmp', 64), ('nsa_v_cmp', 64), ('nsa_k_slc', 64), ('nsa_v_slc', 64),
    ('nsa_k_win', 64), ('nsa_v_win', 64), ('nsa_gate', 12),
    ('swa_q', 256), ('swa_k', 128), ('swa_v', 128),
    ('dil_q', 256), ('dil_k', 256), ('dil_v', 256),
)

U_NSA_Q = 0
U_FOX_Q, U_FOX_K, U_FOX_V = 4, 6, 8
U_SWA_Q, U_SWA_K, U_SWA_V = 10, 12, 14
U_SLC = 16
U_WIN = 18
U_CMP = 20
N_MAIN = 21
N_DIL = 6
N_UNITS = N_MAIN + N_DIL + 1
GATE_LANE0 = 4


def _alibi_slopes():
    n = 12
    s = 2.0 ** (-8.0 * np.arange(1, n + 1) / n)
    return s[:4], s[4:8], s[8:]


def _w_in_layout():
    off, o = {}, 0
    for name, w in IN_SPLITS:
        off[name] = o
        o += w
    cols, scale = [], []

    def seg(name, start, width, sc=1.0):
        cols.extend(range(off[name] + start, off[name] + start + width))
        scale.extend([sc] * width)

    def zeros(n):
        cols.extend([0] * n)
        scale.extend([0.0] * n)

    qs = HEAD_DIM ** -0.5 * LOG2E
    for h in range(4):
        seg('nsa_q', 64 * h, 64, qs)
        zeros(64)
    seg('fox_q', 0, 256, qs)
    seg('fox_k', 0, 256)
    seg('fox_v', 0, 256)
    seg('swa_q', 0, 256, qs)
    for name in ('swa_k', 'swa_v'):
        for kv in range(2):
            seg(name, 64 * kv, 64)
            seg(name, 64 * kv, 64)
    for a, b in (('nsa_k_slc', 'nsa_v_slc'), ('nsa_v_slc', 'nsa_k_slc'),
                 ('nsa_k_win', 'nsa_v_win'), ('nsa_v_win', 'nsa_k_win')):
        seg(a, 0, 64)
        seg(b, 0, 64)
    seg('nsa_k_cmp', 0, 64)
    seg('nsa_v_cmp', 0, 64)
    seg('dil_q', 0, 256, qs)
    seg('dil_k', 0, 256)
    seg('dil_v', 0, 256)
    seg('fox_f', 0, 4)
    seg('nsa_gate', 0, 12)
    zeros(LANES - 16)
    assert len(cols) == N_UNITS * LANES
    return np.asarray(cols, np.int32), np.asarray(scale, np.float32)


def _relayout_w_in(w_in, cols, scale):
    runs, start = [], 0
    for c in range(1, len(cols) + 1):
        same = c < len(cols) and scale[c] == scale[c - 1] and (scale[c] == 0.0 or cols[c] == cols[c - 1] + 1)
        if not same:
            runs.append((start, c))
            start = c
    parts = []
    for a, b in runs:
        if scale[a] == 0.0:
            parts.append(jnp.zeros((w_in.shape[0], b - a), BF16))
        else:
            parts.append((w_in[:, int(cols[a]):int(cols[a]) + (b - a)] * float(scale[a])).astype(BF16))
    return jnp.concatenate(parts, axis=1)


def _rms(x, g):
    ms = jnp.mean(x * x, axis=-1, keepdims=True)
    return x * lax.rsqrt(ms + RMS_EPS) * g


def _dot_nt(a, b):
    return lax.dot_general(a, b, (((1,), (1,)), ((), ())), preferred_element_type=F32)


def _params(sem, vmem=None):
    return pltpu.CompilerParams(dimension_semantics=sem, vmem_limit_bytes=vmem)


def _stack_packed_q(q):
    slot = lax.shift_right_logical(lax.broadcasted_iota(jnp.int32, q.shape, 1), 6)
    return jnp.concatenate([jnp.where(slot == h, q, jnp.zeros_like(q)) for h in range(4)], axis=0)


def _unstack_packed_o(o, t):
    slot = lax.shift_right_logical(lax.broadcasted_iota(jnp.int32, (t, 2 * LANES), 1), 6)
    out = o[:t]
    for h in range(1, 4):
        out = jnp.where(slot == h, o[h * t:(h + 1) * t], out)
    return out


def _stack_wide_q(q_ref):
    return jnp.concatenate([q_ref[:, h * LANES:(h + 1) * LANES] for h in range(4)], axis=0)


def _unstack_wide_o(o, t):
    lane = lax.broadcasted_iota(jnp.int32, (t, LANES), 1)
    pairs = [jnp.where(lane < HEAD_DIM, o[2 * pr * t:(2 * pr + 1) * t, LANES:], o[(2 * pr + 1) * t:(2 * pr + 2) * t, :LANES])
             for pr in range(2)]
    return jnp.concatenate(pairs, axis=1)


FLASH_ROW_BLOCK = 64


def _flash_consume(s_ref, p_ref, stats, v, t, diag):
    m_ref, alpha_ref, l_ref, acc_ref = stats
    n = s_ref.shape[1]
    rb = FLASH_ROW_BLOCK
    lanes = [slice(c * LANES, (c + 1) * LANES) for c in range(n // LANES)]

    def scores(r0):
        s = s_ref[r0:r0 + rb, :]
        if diag:
            row = lax.broadcasted_iota(jnp.int32, (rb, n), 0) + r0 % t
            col = lax.broadcasted_iota(jnp.int32, (rb, n), 1)
            s = jnp.where(col <= row, s, MASKED)
        return s

    for r0 in range(0, 4 * t, rb):
        s = scores(r0)
        mx = s[:, lanes[0]]
        for c in lanes[1:]:
            mx = jnp.maximum(mx, s[:, c])
        m_old = m_ref[r0:r0 + rb, :]
        m_new = jnp.maximum(m_old, jnp.max(mx, axis=-1, keepdims=True))
        alpha_ref[r0:r0 + rb, :] = jnp.exp2(m_old - m_new)
        m_ref[r0:r0 + rb, :] = m_new
    for r0 in range(0, 4 * t, rb):
        s = scores(r0)
        m_new = m_ref[r0:r0 + rb, :]
        psum = jnp.zeros((rb, LANES), F32)
        for c in lanes:
            p = jnp.exp2(s[:, c] - m_new)
            psum = psum + p
            p_ref[r0:r0 + rb, c] = p.astype(BF16)
        l_ref[r0:r0 + rb, :] = alpha_ref[r0:r0 + rb, :] * l_ref[r0:r0 + rb, :] + psum
    alpha = alpha_ref[...]
    acc_ref[...] = (jnp.concatenate([alpha, alpha], axis=1) * acc_ref[...]
                    + jnp.dot(p_ref[...], v, preferred_element_type=F32))


def _flash_causal(i, produce, consume, s_a, s_b, stats):
    m_ref, alpha_ref, l_ref, acc_ref = stats
    m_ref[...] = jnp.full(m_ref.shape, MASKED, F32)
    l_ref[...] = jnp.zeros(l_ref.shape, F32)
    acc_ref[...] = jnp.zeros(acc_ref.shape, F32)
    produce(s_a, 0)

    def pair(jj, carry):
        j = 2 * jj
        produce(s_b, j + 1)
        consume(s_a, j, False, 0)
        produce(s_a, j + 2)
        consume(s_b, j + 1, False, 1)
        return carry

    lax.fori_loop(0, i // 2, pair, 0)

    @pl.when(i % 2 == 1)
    def _():
        produce(s_b, i)
        consume(s_a, i - 1, False, 0)
        consume(s_b, i, True, 1)

    @pl.when(i % 2 == 0)
    def _():
        consume(s_a, i, True, 0)

    return acc_ref[...] * (1.0 / jnp.sum(l_ref[...], axis=-1, keepdims=True))


def _flash_scratch(t):
    return [pltpu.VMEM((4 * t, t), F32), pltpu.VMEM((4 * t, t), F32), pltpu.VMEM((2, 4 * t, t), BF16),
            pltpu.VMEM((4 * t, LANES), F32), pltpu.VMEM((4 * t, LANES), F32), pltpu.VMEM((4 * t, LANES), F32),
            pltpu.VMEM((4 * t, 2 * LANES), F32)]


def _ffn_body(x_ref, g_ref, wg_ref, wu_ref, wd_ref, *rest, final):
    o_ref = rest[-1]
    x = x_ref[...]
    xn = _rms(x, g_ref[...]).astype(BF16)
    a = jnp.dot(xn, wg_ref[...], preferred_element_type=F32)
    b = jnp.dot(xn, wu_ref[...], preferred_element_type=F32)
    h = (a * jax.nn.sigmoid(a) * b).astype(BF16)
    y = x + 0.5 * jnp.dot(h, wd_ref[...], preferred_element_type=F32)
    if final:
        y = _rms(y, rest[0][...])
    o_ref[...] = y


def _ffn(x, g, wg, wu, wd, g_final=None):
    n, d = x.shape
    f = wg.shape[1]
    tm = ROW_TILE if n % ROW_TILE == 0 else n
    const = lambda i: (0, 0)
    once = pl.Buffered(1)
    in_specs = [
        pl.BlockSpec((tm, d), lambda i: (i, 0)),
        pl.BlockSpec((1, d), const),
        pl.BlockSpec((d, f), const, pipeline_mode=once),
        pl.BlockSpec((d, f), const, pipeline_mode=once),
        pl.BlockSpec((f, d), const, pipeline_mode=once),
    ]
    args = [x, g.reshape(1, d), wg, wu, wd]
    if g_final is not None:
        in_specs.append(pl.BlockSpec((1, d), const))
        args.append(g_final.reshape(1, d))
    return pl.pallas_call(
        functools.partial(_ffn_body, final=g_final is not None),
        out_shape=jax.ShapeDtypeStruct((n, d), F32),
        grid=(n // tm,),
        in_specs=in_specs,
        out_specs=pl.BlockSpec((tm, d), lambda i: (i, 0)),
        compiler_params=_params(("parallel",), VMEM_LIMIT),
        name="ffn",
    )(*args)


def _inproj_body(x_ref, g_ref, w_ref, zm_ref, zs_ref, *rest):
    zd_refs, zscr = rest[:-1], rest[-1]
    tm = x_ref.shape[0]
    xn = _rms(x_ref[...], g_ref[...]).astype(BF16)
    z = jnp.dot(xn, w_ref[...], preferred_element_type=F32)
    zm_ref[...] = z[:, :N_MAIN * LANES].astype(BF16)
    zs_ref[...] = z[:, (N_MAIN + N_DIL) * LANES:]
    for u in range(N_DIL):
        zscr[u] = z[:, (N_MAIN + u) * LANES:(N_MAIN + u + 1) * LANES]
    for (_, dd), ref in zip(DIL_PAIRS, zd_refs):
        for u in range(N_DIL):
            if dd == 1:
                ref[0, :, u * LANES:(u + 1) * LANES] = zscr[u].astype(BF16)
            else:
                for r in range(dd):
                    ref[r, :, u * LANES:(u + 1) * LANES] = zscr[u, pl.ds(r, tm // dd, stride=dd), :].astype(BF16)


def _inproj(x, g, w):
    bsz, s, d = x.shape
    tm = min(ROW_TILE, s)
    const = lambda b, i: (0, 0)
    row = lambda b, i: (b, i, 0)
    wd = N_DIL * LANES
    zd_shapes = tuple(jax.ShapeDtypeStruct((bsz, dd, s // dd, wd), BF16) for _, dd in DIL_PAIRS)
    zd_specs = tuple(pl.BlockSpec((None, dd, tm // dd, wd), lambda b, i: (b, 0, i, 0)) for _, dd in DIL_PAIRS)
    return pl.pallas_call(
        _inproj_body,
        out_shape=(jax.ShapeDtypeStruct((bsz, s, N_MAIN * LANES), BF16), jax.ShapeDtypeStruct((bsz, s, LANES), F32))
        + zd_shapes,
        grid=(bsz, s // tm),
        in_specs=[pl.BlockSpec((None, tm, d), row), pl.BlockSpec((1, d), const),
                  pl.BlockSpec((d, N_UNITS * LANES), const, pipeline_mode=pl.Buffered(1))],
        out_specs=(pl.BlockSpec((None, tm, N_MAIN * LANES), row), pl.BlockSpec((None, tm, LANES), row)) + zd_specs,
        scratch_shapes=[pltpu.VMEM((N_DIL, tm, LANES), F32)],
        compiler_params=_params(("parallel", "parallel"), VMEM_LIMIT),
        name="inproj",
    )(x, g.reshape(1, d), w)


def _cumsum_body(x_ref, b_ref, u_ref, c_ref, *, tk):
    x = x_ref[...] + b_ref[...]
    lf = (jnp.minimum(x, 0.0) - jnp.log(1.0 + jnp.exp(-jnp.abs(x)))) * LOG2E
    u = u_ref[...]
    per = tk // LANES
    local = [jnp.dot(lf[:, j * LANES:(j + 1) * LANES], u, precision=lax.Precision.HIGHEST, preferred_element_type=F32)
             for j in range(x.shape[1] // LANES)]
    carry = jnp.zeros((8, 1), F32)
    for j, cs in enumerate(local):
        c_ref[j // per, :, (j % per) * LANES:(j % per + 1) * LANES] = cs + carry
        carry = carry + cs[:, LANES - 1:LANES]


def _fox_cumsum(f_rows, b_rows, tk):
    bsz, _, s = f_rows.shape
    u = jnp.asarray(np.triu(np.ones((LANES, LANES), np.float32)))
    return pl.pallas_call(
        functools.partial(_cumsum_body, tk=tk),
        out_shape=jax.ShapeDtypeStruct((bsz, s // tk, 8, tk), F32),
        grid=(bsz,),
        in_specs=[pl.BlockSpec((None, 8, s), lambda b: (b, 0, 0)),
                  pl.BlockSpec((8, 1), lambda b: (0, 0)),
                  pl.BlockSpec((LANES, LANES), lambda b: (0, 0))],
        out_specs=pl.BlockSpec((None, s // tk, 8, tk), lambda b: (b, 0, 0, 0)),
        compiler_params=_params(("parallel",)),
        name="fox_cumsum",
    )(f_rows, b_rows, u)


def _fox_body(q_ref, k_ref, v_ref, c_ref, o_ref, qs_ref, s_a, s_b, p_ref, *stats, t):
    i = pl.program_id(1)
    qs_ref[...] = _stack_packed_q(q_ref[...])

    def rows(j):
        return pl.ds(pl.multiple_of(j * t, t), t)

    def produce(s_ref, j):
        s = _dot_nt(qs_ref[...], k_ref[rows(j), :])
        for h in range(4):
            s_ref[h * t:(h + 1) * t, :] = s[h * t:(h + 1) * t] - c_ref[j, h:h + 1, :]

    def consume(s_ref, j, diag, slot):
        _flash_consume(s_ref, p_ref.at[slot], stats, v_ref[rows(j), :], t, diag)

    o = _flash_causal(i, produce, consume, s_a, s_b, stats)
    o_ref[...] = _unstack_packed_o(o, t).astype(o_ref.dtype)


def _fox(zm, c, t):
    bsz, s, _ = zm.shape
    return pl.pallas_call(
        functools.partial(_fox_body, t=t),
        out_shape=jax.ShapeDtypeStruct((bsz, s, 2 * LANES), BF16),
        grid=(bsz, s // t),
        in_specs=[pl.BlockSpec((None, t, 2 * LANES), lambda b, i: (b, i, U_FOX_Q // 2)),
                  pl.BlockSpec((None, s, 2 * LANES), lambda b, i: (b, 0, U_FOX_K // 2)),
                  pl.BlockSpec((None, s, 2 * LANES), lambda b, i: (b, 0, U_FOX_V // 2)),
                  pl.BlockSpec((None, s // t, 8, t), lambda b, i: (b, 0, 0, 0))],
        out_specs=pl.BlockSpec((None, t, 2 * LANES), lambda b, i: (b, i, 0)),
        scratch_shapes=[pltpu.VMEM((4 * t, 2 * LANES), BF16)] + _flash_scratch(t),
        compiler_params=_params(("parallel", "arbitrary"), VMEM_LIMIT),
        name="fox_attn",
    )(zm, zm, zm, c)


def _banded_body(*refs, gb, tq, sq, span, s_loc, wpad, max_dist, slopes, wide, with_lse):
    i = pl.program_id(1)
    o_ref = refs[2] if wide else refs[3]
    trel = lax.broadcasted_iota(jnp.int32, (sq, 1), 0).astype(F32)

    def band_bias(kpos_minus_t0):
        dist = lax.broadcasted_iota(jnp.int32, (sq, 1), 0) - kpos_minus_t0
        mask = lax.bitcast_convert_type(dist, jnp.uint32) <= jnp.uint32(max_dist)
        krel = kpos_minus_t0.astype(F32)
        return [jnp.where(mask, (float(slopes[h]) * LOG2E) * krel, MASKED) for h in range(4)]

    unclamped = [span == sq + wpad and r * sq >= wpad for r in range(tq // sq)]
    shared = band_bias(lax.broadcasted_iota(jnp.int32, (1, span), 1) - wpad) if any(unclamped) else None
    for g in range(gb):
        for r in range(tq // sq):
            t0 = i * tq + r * sq
            rows = slice(r * sq, (r + 1) * sq)
            if unclamped[r]:
                start = pl.multiple_of(t0 - wpad, LANES)
                bias = shared
            else:
                start = pl.multiple_of(jnp.clip(t0 - wpad, 0, s_loc - span), LANES)
                bias = band_bias(start - t0 + lax.broadcasted_iota(jnp.int32, (1, span), 1))
            if wide:
                q_ref, kv_ref = refs[:2]
                qs = jnp.concatenate([q_ref[g, rows, h * LANES:(h + 1) * LANES] for h in range(4)], axis=0)
                k = kv_ref[g, pl.ds(start, span), :LANES]
                v = kv_ref[g, pl.ds(start, span), :]
            else:
                q_ref, k_ref, v_ref = refs[:3]
                qs = _stack_packed_q(q_ref[g, rows, :])
                k = k_ref[g, pl.ds(start, span), :]
                v = v_ref[g, pl.ds(start, span), :]
            s = _dot_nt(qs, k)
            s = jnp.concatenate([s[h * sq:(h + 1) * sq] + bias[h] for h in range(4)], axis=0)
            m = jnp.max(s, axis=-1, keepdims=True)
            e = jnp.exp2(s - m)
            l = jnp.sum(e, axis=-1, keepdims=True)
            o = jnp.dot(e.astype(BF16), v, preferred_element_type=F32) * (1.0 / l)
            o_ref[g, rows, :] = (_unstack_wide_o(o, sq) if wide else _unstack_packed_o(o, sq)).astype(o_ref.dtype)
            if with_lse:
                lse = (m + jnp.log2(l)) * LN2
                lane = lax.broadcasted_iota(jnp.int32, (sq, LANES), 1)
                acc = jnp.zeros((sq, LANES), F32)
                for h in range(4):
                    acc = jnp.where(lane == h, lse[h * sq:(h + 1) * sq] - float(slopes[h]) * trel, acc)
                refs[-1][g, rows, :] = acc


BANDED_ROWS = 4096


def _banded(arrs, units, slopes, max_dist, wpad, wide, with_lse, name):
    g, s_loc, _ = arrs[0].shape
    tq = min(BANDED_ROWS, s_loc)
    gb = min(g, BANDED_ROWS // tq)
    sq = min(tq, wpad, 256)
    span = min(sq + wpad, s_loc)
    if wide:
        in_specs = [pl.BlockSpec((gb, tq, 4 * LANES), lambda b, i: (b, i, units[0])),
                    pl.BlockSpec((gb, s_loc, 2 * LANES), lambda b, i: (b, 0, units[1]))]
    else:
        in_specs = [pl.BlockSpec((gb, tq, 2 * LANES), lambda b, i: (b, i, units[0])),
                    pl.BlockSpec((gb, s_loc, 2 * LANES), lambda b, i: (b, 0, units[1])),
                    pl.BlockSpec((gb, s_loc, 2 * LANES), lambda b, i: (b, 0, units[2]))]
    out_shape = [jax.ShapeDtypeStruct((g, s_loc, 2 * LANES), BF16)]
    out_specs = [pl.BlockSpec((gb, tq, 2 * LANES), lambda b, i: (b, i, 0))]
    if with_lse:
        out_shape.append(jax.ShapeDtypeStruct((g, s_loc, LANES), F32))
        out_specs.append(pl.BlockSpec((gb, tq, LANES), lambda b, i: (b, i, 0)))
    body = functools.partial(_banded_body, gb=gb, tq=tq, sq=sq, span=span, s_loc=s_loc, wpad=wpad,
                             max_dist=max_dist, slopes=tuple(float(x) for x in slopes), wide=wide, with_lse=with_lse)
    return pl.pallas_call(
        body, out_shape=tuple(out_shape), grid=(g // gb, s_loc // tq), in_specs=in_specs, out_specs=tuple(out_specs),
        compiler_params=_params(("parallel", "arbitrary")), name=name,
    )(*arrs)


def _compress_body(k_ref, v_ref, pe_ref, w1_ref, w2_ref, o_ref):
    n = k_ref.shape[0]
    hid = []
    for idx, ref in enumerate((k_ref, v_ref)):
        ch = ref[...].astype(F32)
        a = jnp.dot((ch + pe_ref[idx, 0]).astype(BF16), w1_ref[idx, 0], preferred_element_type=F32)
        b = jnp.dot((ch + pe_ref[idx, 1]).astype(BF16), w1_ref[idx, 1], preferred_element_type=F32)
        pre = a + pltpu.roll(b, n - 1, 0)
        hid.append((pre * jax.nn.sigmoid(pre)).astype(BF16))
    o_ref[...] = (jnp.dot(hid[0], w2_ref[0], preferred_element_type=F32)
                  + jnp.dot(hid[1], w2_ref[1], preferred_element_type=F32)).astype(o_ref.dtype)


def _compress(kch, vch, pe, w1, w2):
    bsz, n, w = kch.shape
    full = lambda *shape: pl.BlockSpec(shape, lambda b: (0,) * len(shape))
    blk = pl.BlockSpec((None, n, w), lambda b: (b, 0, 0))
    return pl.pallas_call(
        _compress_body,
        out_shape=jax.ShapeDtypeStruct((bsz, n, 2 * LANES), BF16),
        grid=(bsz,),
        in_specs=[blk, blk, full(2, 2, 1, w), full(2, 2, w, NSA_CMP_HIDDEN), full(2, NSA_CMP_HIDDEN, 2 * LANES)],
        out_specs=pl.BlockSpec((None, n, 2 * LANES), lambda b: (b, 0, 0)),
        compiler_params=_params(("parallel",)),
        name="nsa_compress",
    )(kch, vch, pe, w1, w2)


def _cmp_select_body(q_ref, kv_ref, ov_ref, o_ref, sb_ref, used_ref, *, tq, slopes, n_top):
    i = pl.program_id(1)
    t0 = i * tq
    n_cmp = kv_ref.shape[0]
    tpos = t0 + lax.broadcasted_iota(jnp.int32, (tq, 1), 0)
    cend = NSA_CMP_STRIDE * lax.broadcasted_iota(jnp.int32, (1, n_cmp), 1) + (NSA_CMP_LEN - 1)
    dist = tpos - cend
    mask = dist >= 0
    distf = dist.astype(F32)
    kv = kv_ref[...]
    s_all = _dot_nt(_stack_wide_q(q_ref), kv[:, :LANES])
    psum = jnp.zeros((tq, n_cmp), F32)
    ps = []
    for h in range(4):
        s = jnp.where(mask, s_all[h * tq:(h + 1) * tq] - (float(slopes[h]) * LOG2E) * distf, -jnp.inf)
        m = jnp.max(s, axis=-1, keepdims=True)
        m = jnp.where(m == -jnp.inf, 0.0, m)
        e = jnp.exp2(s - m)
        p = e / jnp.maximum(jnp.sum(e, axis=-1, keepdims=True), 1e-30)
        ps.append(p.astype(BF16))
        psum = psum + p
    o = jnp.dot(jnp.concatenate(ps, axis=0), kv, preferred_element_type=F32)
    o_ref[...] = _unstack_wide_o(o, tq).astype(o_ref.dtype)

    imp = lax.dot_general(ov_ref[...], psum, (((1,), (1,)), ((), ())), precision=lax.Precision.HIGHEST,
                          preferred_element_type=F32)
    j = lax.broadcasted_iota(jnp.int32, (HEAD_DIM, tq), 0)
    tl = t0 + lax.broadcasted_iota(jnp.int32, (HEAD_DIM, tq), 1)
    cur = lax.shift_right_logical(tl, 6)
    causal = j * NSA_SEL_LEN <= tl
    forced = (j == 0) | (j == cur) | (j == cur - 1)
    score = jnp.where(causal, jnp.where(forced, jnp.inf, imp), -jnp.inf)
    rank = jnp.zeros((HEAD_DIM, tq), F32)
    for jp in range(HEAD_DIM):
        other = score[jp:jp + 1, :]
        rank = rank + jnp.where(other > score, 1.0, jnp.where((other == score) & (j > jp), 1.0, 0.0))
    keep = causal & (rank < float(n_top))
    bias_t = jnp.where(keep, 0.0, MASKED)
    sb_ref[...] = jnp.concatenate([jnp.zeros((HEAD_DIM, tq), F32), bias_t], axis=0).T.astype(sb_ref.dtype)
    used = jnp.max(jnp.where(keep, 1.0, 0.0), axis=1, keepdims=True)
    used_ref[...] = jnp.broadcast_to(used, (HEAD_DIM, LANES)).astype(jnp.int32)


def _cmp_select(zm, kv_c, overlap_t, tq, slopes, n_top):
    bsz, s, _ = zm.shape
    n_cmp = kv_c.shape[1]
    return pl.pallas_call(
        functools.partial(_cmp_select_body, tq=tq, slopes=tuple(float(x) for x in slopes), n_top=n_top),
        out_shape=(jax.ShapeDtypeStruct((bsz, s, 2 * LANES), BF16), jax.ShapeDtypeStruct((bsz, s, LANES), BF16),
                   jax.ShapeDtypeStruct((bsz, s // tq, HEAD_DIM, LANES), jnp.int32)),
        grid=(bsz, s // tq),
        in_specs=[pl.BlockSpec((None, tq, 4 * LANES), lambda b, i: (b, i, U_NSA_Q // 4)),
                  pl.BlockSpec((None, n_cmp, 2 * LANES), lambda b, i: (b, 0, 0)),
                  pl.BlockSpec((HEAD_DIM, n_cmp), lambda b, i: (0, 0))],
        out_specs=(pl.BlockSpec((None, tq, 2 * LANES), lambda b, i: (b, i, 0)),
                   pl.BlockSpec((None, tq, LANES), lambda b, i: (b, i, 0)),
                   pl.BlockSpec((None, None, HEAD_DIM, LANES), lambda b, i: (b, i, 0, 0))),
        compiler_params=_params(("parallel", "parallel")),
        name="nsa_cmp_select",
    )(zm, kv_c, overlap_t)


def _sel_body(used_ref, q_ref, sb_ref, kv_ref, o_ref, kp_ref, qs_ref, lst_ref, s_a, s_b, p_ref, *stats, t, slopes):
    b = pl.program_id(0)
    i = pl.program_id(1)
    s_len = kv_ref.shape[0]

    @pl.when(i == 0)
    def _():
        pos = lax.broadcasted_iota(jnp.int32, (s_len, LANES), 0)
        ln = lax.broadcasted_iota(jnp.int32, (s_len, LANES), 1)
        onehot = jnp.where(lax.shift_right_logical(pos, 6) == ln - HEAD_DIM, 1.0, 0.0).astype(BF16)
        kp_ref[...] = jnp.where(ln < HEAD_DIM, kv_ref[:, :LANES], onehot)

    per = t // NSA_SEL_LEN
    n = jnp.int32(0)
    for c in range(s_len // t - 1):
        any_used = used_ref[b, i, c * per]
        for k in range(1, per):
            any_used = any_used | used_ref[b, i, c * per + k]
        lst_ref[n] = jnp.int32(c)
        n = n + jnp.where((c < i) & (any_used != 0), 1, 0)
    lst_ref[n] = i

    lane = lax.broadcasted_iota(jnp.int32, (t, LANES), 1)
    sb = sb_ref[...]
    for h in range(4):
        qs_ref[h * t:(h + 1) * t, :] = jnp.where(lane < HEAD_DIM, q_ref[:, h * LANES:(h + 1) * LANES], sb)
    krel0 = lax.broadcasted_iota(jnp.int32, (1, t), 1)

    def rows(j):
        return pl.ds(pl.multiple_of(j * t, t), t)

    def produce(s_ref, pos):
        j = lst_ref[pos]
        s = _dot_nt(qs_ref[...], kp_ref[rows(j), :])
        krel = (krel0 + (j - i) * t).astype(F32)
        for h in range(4):
            s_ref[h * t:(h + 1) * t, :] = s[h * t:(h + 1) * t] + (float(slopes[h]) * LOG2E) * krel

    def consume(s_ref, pos, diag, slot):
        _flash_consume(s_ref, p_ref.at[slot], stats, kv_ref[rows(lst_ref[pos]), :], t, diag)

    o = _flash_causal(n, produce, consume, s_a, s_b, stats)
    o_ref[...] = _unstack_wide_o(o, t).astype(o_ref.dtype)


def _sel(zm, sbias, used, t, slopes):
    bsz, s, _ = zm.shape
    grid_spec = pltpu.PrefetchScalarGridSpec(
        num_scalar_prefetch=1,
        grid=(bsz, s // t),
        in_specs=[pl.BlockSpec((None, t, 4 * LANES), lambda b, i, u: (b, i, U_NSA_Q // 4)),
                  pl.BlockSpec((None, t, LANES), lambda b, i, u: (b, i, 0)),
                  pl.BlockSpec((None, s, 2 * LANES), lambda b, i, u: (b, 0, U_SLC // 2))],
        out_specs=pl.BlockSpec((None, t, 2 * LANES), lambda b, i, u: (b, i, 0)),
        scratch_shapes=[pltpu.VMEM((s, LANES), BF16), pltpu.VMEM((4 * t, LANES), BF16),
                        pltpu.SMEM((s // t + 1,), jnp.int32)] + _flash_scratch(t),
    )
    return pl.pallas_call(
        functools.partial(_sel_body, t=t, slopes=tuple(float(x) for x in slopes)),
        out_shape=jax.ShapeDtypeStruct((bsz, s, 2 * LANES), BF16),
        grid_spec=grid_spec,
        compiler_params=_params(("parallel", "arbitrary"), VMEM_LIMIT),
        name="nsa_sel_attn",
    )(used, zm, sbias, zm)


def _spread_heads(vals, lane0, stride, tm):
    lane = lax.broadcasted_iota(jnp.int32, (tm, LANES), 1)
    col = lambda h: vals[:, lane0 + stride * h:lane0 + stride * h + 1]
    halves = [jnp.where(lane < HEAD_DIM, col(2 * p), col(2 * p + 1)) for p in range(2)]
    return jnp.concatenate(halves, axis=1)


def _outproj_body(x_ref, ofox_ref, ocmp_ref, oslc_ref, owin_ref, zs_ref, oswa_ref, lswa_ref, sink_ref,
                  od1_ref, od4_ref, od16_ref, l1_ref, l4_ref, l16_ref, eg_ref, w_ref, o_ref, o_scr, l_scr):
    tm = x_ref.shape[0]

    def natural(ref, scr):
        dd, _, w = ref.shape
        if dd == 1:
            return ref[0].astype(F32)
        for u in range(w // LANES):
            for r in range(dd):
                scr[u, pl.ds(r, tm // dd, stride=dd), :] = ref[r, :, u * LANES:(u + 1) * LANES].astype(F32)
        return jnp.concatenate([scr[u] for u in range(w // LANES)], axis=1)

    gate = jax.nn.sigmoid(zs_ref[...])
    g_hi = gate.astype(BF16)
    g_lo = (gate - g_hi.astype(F32)).astype(BF16)
    spread = lambda br: (jnp.dot(g_hi, eg_ref[br], preferred_element_type=F32)
                         + jnp.dot(g_lo, eg_ref[br], preferred_element_type=F32))
    o_nsa = (spread(0) * ocmp_ref[...].astype(F32) + spread(1) * oslc_ref[...].astype(F32)
             + spread(2) * owin_ref[...].astype(F32))
    keep = jax.nn.sigmoid(lswa_ref[...] - sink_ref[...])
    o_swa = _spread_heads(keep, 0, 1, tm) * oswa_ref[...].astype(F32)
    l1, l4, l16 = natural(l1_ref, l_scr), natural(l4_ref, l_scr), natural(l16_ref, l_scr)
    m = jnp.maximum(jnp.maximum(l1, l4), l16)
    e1, e4, e16 = jnp.exp(l1 - m), jnp.exp(l4 - m), jnp.exp(l16 - m)
    inv = 1.0 / (e1 + e4 + e16)
    o_dil = _spread_heads(e1 * inv, 0, 1, tm) * natural(od1_ref, o_scr)
    o_dil = o_dil + _spread_heads(e4 * inv, 0, 1, tm) * natural(od4_ref, o_scr)
    o_dil = o_dil + _spread_heads(e16 * inv, 0, 1, tm) * natural(od16_ref, o_scr)
    y = x_ref[...]
    for g, o in enumerate((ofox_ref[...], o_nsa.astype(BF16), o_swa.astype(BF16), o_dil.astype(BF16))):
        y = y + jnp.dot(o, w_ref[g], preferred_element_type=F32)
    o_ref[...] = y


def _gate_spread_matrices():
    e = np.zeros((3, LANES, 2 * LANES), np.float32)
    for h in range(4):
        for br in range(3):
            e[br, GATE_LANE0 + 3 * h + br, HEAD_DIM * h:HEAD_DIM * (h + 1)] = 1.0
    return jnp.asarray(e, BF16)


def _outproj(x, heads, zs, lses, sinks, w):
    bsz, s, d = x.shape
    tm = min(ROW_TILE, s)
    row = lambda b, i: (b, i, 0)
    wide = pl.BlockSpec((None, tm, 2 * LANES), row)
    narrow = pl.BlockSpec((None, tm, LANES), row)

    def strided(a):
        dd = a.shape[1]
        return pl.BlockSpec((None, dd, tm // dd, a.shape[-1]), lambda b, i: (b, 0, i, 0))

    ofox, ocmp, oslc, owin, oswa, od1, od4, od16 = heads
    lswa, l1, l4, l16 = lses
    return pl.pallas_call(
        _outproj_body,
        out_shape=jax.ShapeDtypeStruct((bsz, s, d), F32),
        grid=(bsz, s // tm),
        in_specs=[pl.BlockSpec((None, tm, d), row), wide, wide, wide, wide, narrow, wide, narrow,
                  pl.BlockSpec((1, LANES), lambda b, i: (0, 0)), strided(od1), strided(od4), strided(od16),
                  strided(l1), strided(l4), strided(l16),
                  pl.BlockSpec((3, LANES, 2 * LANES), lambda b, i: (0, 0, 0)),
                  pl.BlockSpec((4, 2 * LANES, d), lambda b, i: (0, 0, 0))],
        out_specs=pl.BlockSpec((None, tm, d), row),
        scratch_shapes=[pltpu.VMEM((2, tm, LANES), F32), pltpu.VMEM((1, tm, LANES), F32)],
        compiler_params=_params(("parallel", "parallel"), VMEM_LIMIT),
        name="outproj",
    )(x, ofox, ocmp, oslc, owin, zs, oswa, lswa, sinks, od1, od4, od16, l1, l4, l16, _gate_spread_matrices(), w)


def _overlap_t(n_cmp, n_sel):
    cs = np.arange(n_cmp) * NSA_CMP_STRIDE
    ss = np.arange(n_sel) * NSA_SEL_LEN
    ov = (cs[None, :] <= ss[:, None] + NSA_SEL_LEN - 1) & (cs[None, :] + NSA_CMP_LEN - 1 >= ss[:, None])
    out = np.zeros((HEAD_DIM, n_cmp), np.float32)
    out[:n_sel] = ov
    return jnp.asarray(out)


def _mixer(x, g_mix, w_in, fox_b_f, cmp_pe, cmp_w1, cmp_w2, swa_sinks, w_out):
    bsz, s, d = x.shape
    sl_swa, sl_nsa, sl_dil = _alibi_slopes()
    cols, scale = _w_in_layout()
    w_perm = _relayout_w_in(w_in, cols, scale)
    zm, zs, *zds = _inproj(x, g_mix, w_perm)

    t_fox = min(ROW_TILE, s)
    f_rows = jnp.transpose(zs[:, :, :8], (0, 2, 1))
    b_rows = jnp.pad(fox_b_f.reshape(4, 1), ((0, 4), (0, 0)))
    c = _fox_cumsum(f_rows, b_rows, t_fox)
    o_fox = _fox(zm, c, t_fox)

    n_chunk = s // NSA_CMP_STRIDE
    n_sel = s // NSA_SEL_LEN
    assert n_sel <= HEAD_DIM
    kvc = zm[:, :, U_CMP * LANES:(U_CMP + 1) * LANES]
    kch = kvc[:, :, :HEAD_DIM].reshape(bsz, n_chunk, NSA_CMP_STRIDE * HEAD_DIM)
    vch = kvc[:, :, HEAD_DIM:].reshape(bsz, n_chunk, NSA_CMP_STRIDE * HEAD_DIM)
    pe = cmp_pe.reshape(2, 2, 1, NSA_CMP_STRIDE * HEAD_DIM)
    w1 = cmp_w1.reshape(2, 2, NSA_CMP_STRIDE * HEAD_DIM, NSA_CMP_HIDDEN).astype(BF16)
    zpad = jnp.zeros((NSA_CMP_HIDDEN, HEAD_DIM), F32)
    w2 = jnp.stack([jnp.concatenate([cmp_w2[0], zpad, zpad, cmp_w2[0]], axis=1),
                    jnp.concatenate([zpad, cmp_w2[1], cmp_w2[1], zpad], axis=1)]).astype(BF16)
    kv_c = _compress(kch, vch, pe, w1, w2)
    t_sel = min(ROW_TILE, s)
    o_cmp, sbias, used = _cmp_select(zm, kv_c, _overlap_t(n_chunk, n_sel), t_sel, sl_nsa, min(NSA_TOPN, n_sel))
    o_slc = _sel(zm, sbias, used[:, :, :, 0], t_sel, sl_nsa)
    (o_win,) = _banded((zm, zm), (U_NSA_Q // 4, U_WIN // 2), sl_nsa, NSA_WINDOW - 1, NSA_WINDOW,
                       True, False, "nsa_win_attn")

    o_swa, l_swa = _banded((zm, zm, zm), (U_SWA_Q // 2, U_SWA_K // 2, U_SWA_V // 2), sl_swa, SWA_WINDOW - 1,
                           SWA_WINDOW, False, True, "swa_attn")

    o_dil, l_dil = [], []
    for (window, dd), zd in zip(DIL_PAIRS, zds):
        zz = zd.reshape(bsz * dd, s // dd, N_DIL * LANES)
        o, l = _banded((zz, zz, zz), (0, 1, 2), sl_dil * dd, window // dd, LANES, False, True, "dil%d_attn" % dd)
        o_dil.append(o.reshape(bsz, dd, s // dd, 2 * LANES))
        l_dil.append(l.reshape(bsz, dd, s // dd, LANES))

    sinks = jnp.pad(swa_sinks.reshape(1, 4), ((0, 0), (0, LANES - 4)))
    heads = (o_fox, o_cmp, o_slc, o_win, o_swa, *o_dil)
    return _outproj(x, heads, zs, (l_swa, *l_dil), sinks, w_out.reshape(4, 2 * LANES, d).astype(BF16))


def kernel(x, norm_ffn1, ffn1_w_gate, ffn1_w_up, ffn1_w_down, norm_mix, w_in, fox_b_f, nsa_cmp_pe, nsa_cmp_w1,
           nsa_cmp_w2, swa_sinks, w_out, norm_ffn2, ffn2_w_gate, ffn2_w_up, ffn2_w_down, norm_final):
    bsz, s, d = x.shape
    depth = norm_ffn1.shape[0]
    h = x.reshape(bsz * s, d)
    for l in range(depth):
        h = _ffn(h, norm_ffn1[l], ffn1_w_gate[l].astype(BF16), ffn1_w_up[l].astype(BF16), ffn1_w_down[l].astype(BF16))
        h = _mixer(h.reshape(bsz, s, d), norm_mix[l], w_in[l], fox_b_f[l], nsa_cmp_pe[l], nsa_cmp_w1[l],
                   nsa_cmp_w2[l], swa_sinks[l], w_out[l]).reshape(bsz * s, d)
        h = _ffn(h, norm_ffn2[l], ffn2_w_gate[l].astype(BF16), ffn2_w_up[l].astype(BF16), ffn2_w_down[l].astype(BF16),
                 norm_final if l == depth - 1 else None)
    return h.reshape(bsz, s, d)
```
